```python
import math
import jax, jax.numpy as jnp
from jax import lax
import numpy as np

D_MODEL = 2048
BATCH = 4
SEQ = 8192
DEPTH = 2

N_BRANCHES = 3
BRANCH_WIDTH = D_MODEL // 2
MLSTM_HEADS = 4
MLSTM_DV = BRANCH_WIDTH // MLSTM_HEADS
MLSTM_DQK = MLSTM_DV // 2
MLSTM_CHUNK = 64
CONV_WIDTH = 4
NSA_HEADS = 8
NSA_KV_GROUPS = 2
NSA_HPG = NSA_HEADS // NSA_KV_GROUPS
NSA_DH = BRANCH_WIDTH // NSA_HEADS
CMP_LEN = 32
CMP_STRIDE = 16
SLC_LEN = 64
NSA_N_SELECT = 16
WINDOW = 512
Q_BLOCK = 128
NUM_BUCKETS = 32
MAX_DISTANCE = 1024
GMLP_CHUNK = 128
GMLP_GROUPS = 8
GMLP_GW = BRANCH_WIDTH // GMLP_GROUPS
N_EXPERTS = 32
TOP_K = 4
D_FF = 3 * D_MODEL // 4
SWIGLU_LIMIT = 7.0
SWIGLU_ALPHA = 1.702
EXPERT_BLOCK = 256
NEG_INF = -1e30
FORCE_SELECT = 1e4
IN_SIZES = (
    MLSTM_HEADS * MLSTM_DQK, MLSTM_HEADS * MLSTM_DQK, MLSTM_HEADS * MLSTM_DV,
    MLSTM_HEADS, MLSTM_HEADS, BRANCH_WIDTH,
    NSA_HEADS * NSA_DH,
    NSA_KV_GROUPS * NSA_DH, NSA_KV_GROUPS * NSA_DH,
    NSA_KV_GROUPS * NSA_DH, NSA_KV_GROUPS * NSA_DH,
    NSA_KV_GROUPS * NSA_DH, NSA_KV_GROUPS * NSA_DH,
    NSA_HEADS * 3,
    BRANCH_WIDTH, BRANCH_WIDTH,
)
IN_TOTAL = sum(IN_SIZES)

kernel_name = 'hybrid_mlstm_nsa_gmlp_moe_block'


def rms_norm(x, g, eps=1e-6):
    xf = x.astype(jnp.float32)
    y = xf * lax.rsqrt(jnp.mean(xf * xf, axis=-1, keepdims=True) + eps)
    return (y * g.astype(jnp.float32)).astype(x.dtype)


def layer_norm(x, g, eps=1e-6):
    xf = x.astype(jnp.float32)
    mu = jnp.mean(xf, axis=-1, keepdims=True)
    var = jnp.mean(jnp.square(xf - mu), axis=-1, keepdims=True)
    return ((xf - mu) * lax.rsqrt(var + eps) * g.astype(jnp.float32)).astype(x.dtype)


def masked_softmax(s, mask):
    p = jax.nn.softmax(jnp.where(mask, s.astype(jnp.float32), NEG_INF), axis=-1)
    return jnp.where(mask, p, 0.0)


def t5_bucket(dist):
    dist = jnp.maximum(dist, 0)
    max_exact = NUM_BUCKETS // 2
    log_ratio = jnp.log(jnp.maximum(dist, 1).astype(jnp.float32) / max_exact) / math.log(MAX_DISTANCE / max_exact)
    large = jnp.minimum(max_exact + (log_ratio * (NUM_BUCKETS - max_exact)).astype(jnp.int32), NUM_BUCKETS - 1)
    return jnp.where(dist < max_exact, dist, large)


def causal_conv(x, w, b):
    S = x.shape[1]
    K = w.shape[0]
    xp = jnp.pad(x, ((0, 0), (K - 1, 0), (0, 0)))
    y = b
    for j in range(K):
        y = y + w[j] * xp[:, j:j + S]
    return y


def mlstm_chunkwise(q, k, v, logi, logf):
    B, S, H, DQK = q.shape
    DV = v.shape[-1]
    L = MLSTM_CHUNK
    NCH = S // L

    def to_chunks(t):
        return t.reshape((B, NCH, L) + t.shape[2:]).swapaxes(0, 1)

    xs = tuple(to_chunks(t) for t in (q, k, v, logi, logf))
    causal = jnp.tril(jnp.ones((L, L), dtype=bool))[None, :, :, None]

    def step(carry, xs_c):
        C, n, m = carry
        qb, kb, vb, ib, fb = xs_c
        b = jnp.cumsum(fb, axis=1)
        a = b + m[:, None, :]
        D = jnp.where(causal, b[:, :, None, :] - b[:, None, :, :] + ib[:, None, :, :], -jnp.inf)
        m_t = jnp.maximum(a, D.max(axis=2))
        w_intra = jnp.exp(D - m_t[:, :, None, :])
        w_inter = jnp.exp(a - m_t)
        qk = jnp.einsum('bthd,bshd->btsh', qb, kb) * w_intra
        num = jnp.einsum('btsh,bshv->bthv', qk, vb) + w_inter[..., None] * jnp.einsum('bhvd,bthd->bthv', C, qb)
        den = qk.sum(axis=2) + w_inter * jnp.einsum('bhd,bthd->bth', n, qb)
        h = num / jnp.maximum(jnp.abs(den), jnp.exp(-m_t))[..., None]
        bL = b[:, -1, :]
        g = bL[:, None, :] - b + ib
        m_new = jnp.maximum(bL + m, g.max(axis=1))
        ws = jnp.exp(g - m_new[:, None, :])
        wc = jnp.exp(bL + m - m_new)
        C = wc[..., None, None] * C + jnp.einsum('bsh,bshv,bshd->bhvd', ws, vb, kb)
        n = wc[..., None] * n + jnp.einsum('bsh,bshd->bhd', ws, kb)
        return (C, n, m_new), h

    init = (jnp.zeros((B, H, DV, DQK), jnp.float32), jnp.zeros((B, H, DQK), jnp.float32),
            jnp.full((B, H), NEG_INF, jnp.float32))
    _, hs = lax.scan(step, init, xs)
    return hs.swapaxes(0, 1).reshape(B, S, H, DV)


def nsa_attention(q, k_c, v_c, k_s, v_s, k_w, v_w, gates, cmp_pos, cmp_k_w1, cmp_k_w2,
                  cmp_v_w1, cmp_v_w2, qnorm_g, knorm_g, rel_bias):
    B, S = q.shape[:2]
    G, HPG, DH = NSA_KV_GROUPS, NSA_HPG, NSA_DH
    f32 = jnp.float32
    q = (rms_norm(q.reshape(B, S, G, HPG, DH), qnorm_g) * DH ** -0.5).astype(f32)

    NC = (S - CMP_LEN) // CMP_STRIDE + 1
    cidx = jnp.arange(NC)[:, None] * CMP_STRIDE + jnp.arange(CMP_LEN)[None, :]

    def compress(t, w1, w2):
        blk = t.reshape(B, S, G, DH)[:, cidx] + cmp_pos[None, None, :, None, :]
        flat = blk.transpose(0, 1, 3, 2, 4).reshape(B, NC, G, CMP_LEN * DH)
        return jax.nn.gelu(flat @ w1) @ w2

    k_cmp = rms_norm(compress(k_c, cmp_k_w1, cmp_k_w2), knorm_g).astype(f32)
    v_cmp = compress(v_c, cmp_v_w1, cmp_v_w2).astype(f32)
    cstart = jnp.arange(NC) * CMP_STRIDE
    cend = cstart + CMP_LEN - 1

    NS = S // SLC_LEN
    n_sel = min(NSA_N_SELECT, NS)
    ks_blk = rms_norm(k_s.reshape(B, S, G, DH), knorm_g).astype(f32).reshape(B, NS, SLC_LEN, G, DH).transpose(0, 3, 1, 2, 4)
    vs_blk = v_s.astype(f32).reshape(B, NS, SLC_LEN, G, DH).transpose(0, 3, 1, 2, 4)
    sstart = jnp.arange(NS) * SLC_LEN
    overlap = jnp.clip(jnp.minimum(cstart[:, None] + CMP_LEN, sstart[None, :] + SLC_LEN)
                       - jnp.maximum(cstart[:, None], sstart[None, :]), 0, None).astype(f32) / CMP_LEN

    pad = ((0, 0), (WINDOW, 0), (0, 0), (0, 0))
    kw_pad = jnp.pad(rms_norm(k_w.reshape(B, S, G, DH), knorm_g).astype(f32), pad)
    vw_pad = jnp.pad(v_w.reshape(B, S, G, DH).astype(f32), pad)

    rb = rel_bias.astype(f32).reshape(NUM_BUCKETS, G, HPG)
    g = jax.nn.sigmoid(gates.astype(f32)).reshape(B, S, G, HPG, 3)
    NQB = S // Q_BLOCK
    q_blocks = q.reshape(B, NQB, Q_BLOCK, G, HPG, DH).swapaxes(0, 1)
    g_blocks = g.reshape(B, NQB, Q_BLOCK, G, HPG, 3).swapaxes(0, 1)
    bi = jnp.arange(B)[:, None, None, None]
    gi = jnp.arange(G)[None, :, None, None]
    jn = jnp.arange(NS)

    def one_block(args):
        qb, gb, j = args
        start = j * Q_BLOCK
        tpos = start + jnp.arange(Q_BLOCK)
        s_c = jnp.einsum('bqghd,bcgd->bghqc', qb, k_cmp)
        s_c = s_c + rb[t5_bucket(tpos[:, None] - cend[None, :])].transpose(2, 3, 0, 1)
        p_c = masked_softmax(s_c, cend[None, :] <= tpos[:, None])
        o_c = jnp.einsum('bghqc,bcgd->bqghd', p_c, v_cmp)
        imp = jnp.einsum('bghqc,cn->bgqn', p_c, overlap)
        blk_t = tpos // SLC_LEN
        valid = sstart[None, :] <= tpos[:, None]
        forced = (jn[None, :] == 0) | (jn[None, :] == blk_t[:, None]) | (jn[None, :] == blk_t[:, None] - 1)
        score = jnp.where(valid, imp + jnp.where(forced, FORCE_SELECT, 0.0), NEG_INF)
        _, sel = lax.top_k(score, n_sel)
        k_sel = ks_blk[bi, gi, sel].reshape(B, G, Q_BLOCK, n_sel * SLC_LEN, DH)
        v_sel = vs_blk[bi, gi, sel].reshape(B, G, Q_BLOCK, n_sel * SLC_LEN, DH)
        kpos = (sel[..., None] * SLC_LEN + jnp.arange(SLC_LEN)).reshape(B, G, Q_BLOCK, n_sel * SLC_LEN)
        dist_s = tpos[None, None, :, None] - kpos
        s_s = jnp.einsum('bqghd,bgqkd->bghqk', qb, k_sel) + rb[t5_bucket(dist_s), gi].transpose(0, 1, 4, 2, 3)
        p_s = masked_softmax(s_s, (dist_s >= 0)[:, :, None])
        o_s = jnp.einsum('bghqk,bgqkd->bqghd', p_s, v_sel)
        kwin = lax.dynamic_slice_in_dim(kw_pad, start, Q_BLOCK + WINDOW, axis=1)
        vwin = lax.dynamic_slice_in_dim(vw_pad, start, Q_BLOCK + WINDOW, axis=1)
        wpos = start - WINDOW + jnp.arange(Q_BLOCK + WINDOW)
        dist_w = tpos[:, None] - wpos[None, :]
        mask_w = (dist_w >= 0) & (dist_w < WINDOW) & (wpos[None, :] >= 0)
        s_w = jnp.einsum('bqghd,bkgd->bghqk', qb, kwin) + rb[t5_bucket(dist_w)].transpose(2, 3, 0, 1)
        p_w = masked_softmax(s_w, mask_w)
        o_w = jnp.einsum('bghqk,bkgd->bqghd', p_w, vwin)
        return gb[..., 0:1] * o_c + gb[..., 1:2] * o_s + gb[..., 2:3] * o_w

    out = lax.map(one_block, (q_blocks, g_blocks, jnp.arange(NQB)))
    return out.swapaxes(0, 1).reshape(B, S, NSA_HEADS * DH)


def gmlp_spatial(u, v, norm_g, ws, b):
    B, S, _ = u.shape
    u = jax.nn.gelu(u)
    v = layer_norm(jax.nn.gelu(v), norm_g)
    vc = v.reshape(B, S // GMLP_CHUNK, GMLP_CHUNK, GMLP_GROUPS, GMLP_GW)
    ws_causal = ws * jnp.tril(jnp.ones((GMLP_CHUNK, GMLP_CHUNK), ws.dtype))
    s = jnp.einsum('gts,bnsgd->bntgd', ws_causal, vc) + b.T[None, None, :, :, None]
    return u * s.reshape(B, S, BRANCH_WIDTH)


def token_mixers(h, w_in, conv_w, conv_b, mlstm_gate_b, mlstm_norm_g, cmp_pos, cmp_k_w1, cmp_k_w2,
                 cmp_v_w1, cmp_v_w2, qnorm_g, knorm_g, rel_bias, gmlp_norm_g, gmlp_ws, gmlp_b,
                 w_branch, w_gate, w_out):
    B, S, _ = h.shape
    z = h @ w_in
    points = [int(p) for p in np.cumsum(IN_SIZES)[:-1]]
    (qm, km, vm, ig, fg, og, qn, kc, vc, ks, vs, kw, vw, gn, gu, gv) = jnp.split(z, points, axis=-1)

    qk = jax.nn.silu(causal_conv(jnp.concatenate([qm, km], axis=-1), conv_w, conv_b)).astype(jnp.float32)
    qm, km = jnp.split(qk, 2, axis=-1)
    q = qm.reshape(B, S, MLSTM_HEADS, MLSTM_DQK) * MLSTM_DQK ** -0.5
    k = km.reshape(B, S, MLSTM_HEADS, MLSTM_DQK)
    v = vm.astype(jnp.float32).reshape(B, S, MLSTM_HEADS, MLSTM_DV)
    logi = (ig + mlstm_gate_b[:MLSTM_HEADS]).astype(jnp.float32)
    logf = jax.nn.log_sigmoid((fg + mlstm_gate_b[MLSTM_HEADS:]).astype(jnp.float32))
    hm = mlstm_chunkwise(q, k, v, logi, logf)
    hm = rms_norm(hm, mlstm_norm_g.reshape(MLSTM_HEADS, MLSTM_DV)).reshape(B, S, BRANCH_WIDTH)
    y_a = (jax.nn.sigmoid(og.astype(jnp.float32)) * hm).astype(h.dtype)

    y_b = nsa_attention(qn, kc, vc, ks, vs, kw, vw, gn, cmp_pos, cmp_k_w1, cmp_k_w2, cmp_v_w1, cmp_v_w2,
                        qnorm_g, knorm_g, rel_bias).astype(h.dtype)

    y_c = gmlp_spatial(gu, gv, gmlp_norm_g, gmlp_ws, gmlp_b)

    merged = 0.0
    for i, y in enumerate((y_a, y_b, y_c)):
        merged = merged + jax.nn.sigmoid(h @ w_gate[i]) * (y @ w_branch[i])
    return merged @ w_out


def moe_ffn(h, router_w, router_b, w1, b1, w2, b2):
    B, S, D = h.shape
    N = B * S
    t = h.reshape(N, D)
    logits = (t @ router_w + router_b).astype(jnp.float32)
    top_val, top_idx = lax.top_k(logits, TOP_K)
    probs = jax.nn.softmax(top_val, axis=-1)
    A = N * TOP_K
    e_flat = top_idx.reshape(A)
    order = jnp.argsort(e_flat)
    e_sorted = e_flat[order]
    tok_sorted = order // TOP_K
    counts = jnp.bincount(e_flat, length=N_EXPERTS)
    padded = (counts + EXPERT_BLOCK - 1) // EXPERT_BLOCK * EXPERT_BLOCK
    pend = jnp.cumsum(padded)
    pstart = pend - padded
    cstart = jnp.cumsum(counts) - counts
    dest = pstart[e_sorted] + (jnp.arange(A) - cstart[e_sorted])
    NB = A // EXPERT_BLOCK + N_EXPERTS
    buf_tok = jnp.zeros((NB * EXPERT_BLOCK,), jnp.int32).at[dest].set(tok_sorted)
    block_exp = jnp.minimum(jnp.searchsorted(pend, jnp.arange(NB) * EXPERT_BLOCK, side='right'), N_EXPERTS - 1)
    xb = t[buf_tok].reshape(NB, EXPERT_BLOCK, D)

    def expert_block(args):
        xblk, e = args
        gu = xblk @ w1[e] + b1[e]
        gate, up = jnp.split(gu, 2, axis=-1)
        gate = jnp.minimum(gate, SWIGLU_LIMIT)
        up = jnp.clip(up, -SWIGLU_LIMIT, SWIGLU_LIMIT)
        act = (up + 1.0) * (gate * jax.nn.sigmoid(SWIGLU_ALPHA * gate))
        return act @ w2[e] + b2[e]

    yb = lax.map(expert_block, (xb, block_exp)).reshape(NB * EXPERT_BLOCK, D)
    y_sorted = yb[dest] * probs.reshape(A)[order][:, None].astype(yb.dtype)
    return jax.ops.segment_sum(y_sorted, tok_sorted, num_segments=N).reshape(B, S, D)


def setup_inputs(seed: int = 0) -> dict:
    key = jax.random.key(seed)
    ks = iter(jax.random.split(key, 40))
    L, D, BW, DH, E, F = DEPTH, D_MODEL, BRANCH_WIDTH, NSA_DH, N_EXPERTS, D_FF

    def nrm(shape, scale):
        return jax.random.normal(next(ks), shape, jnp.float32) * scale

    def gain(shape):
        return 1.0 + nrm(shape, 0.02)

    return {
        'x': nrm((BATCH, SEQ, D), 1.0),
        'c': nrm((BATCH, D), 1.0),
        'ada_w': nrm((L, D, 6 * D), 0.5 * D ** -0.5),
        'ada_b': nrm((L, 6 * D), 0.02),
        'norm1_g': gain((L, D)),
        'norm2_g': gain((L, D)),
        'w_in': nrm((L, D, IN_TOTAL), D ** -0.5),
        'conv_w': nrm((L, CONV_WIDTH, 2 * MLSTM_HEADS * MLSTM_DQK), CONV_WIDTH ** -0.5),
        'conv_b': nrm((L, 2 * MLSTM_HEADS * MLSTM_DQK), 0.02),
        'mlstm_gate_b': jnp.concatenate([nrm((L, MLSTM_HEADS), 0.1), 3.0 + nrm((L, MLSTM_HEADS), 0.5)], axis=-1),
        'mlstm_norm_g': gain((L, BW)),
        'cmp_pos': nrm((L, CMP_LEN, DH), 0.1),
        'cmp_k_w1': nrm((L, CMP_LEN * DH, DH), (CMP_LEN * DH) ** -0.5),
        'cmp_k_w2': nrm((L, DH, DH), DH ** -0.5),
        'cmp_v_w1': nrm((L, CMP_LEN * DH, DH), (CMP_LEN * DH) ** -0.5),
        'cmp_v_w2': nrm((L, DH, DH), DH ** -0.5),
        'qnorm_g': gain((L, DH)),
        'knorm_g': gain((L, DH)),
        'rel_bias': nrm((NUM_BUCKETS, NSA_HEADS), 0.5),
        'gmlp_norm_g': gain((L, BW)),
        'gmlp_ws': nrm((L, GMLP_GROUPS, GMLP_CHUNK, GMLP_CHUNK), GMLP_CHUNK ** -0.5),
        'gmlp_b': 1.0 + nrm((L, GMLP_GROUPS, GMLP_CHUNK), 0.02),
        'w_branch': nrm((L, N_BRANCHES, BW, D), BW ** -0.5),
        'w_gate': nrm((L, N_BRANCHES, D, D), D ** -0.5),
        'w_out': nrm((L, D, D), D ** -0.5),
        'router_w': nrm((L, D, E), D ** -0.5),
        'router_b': nrm((L, E), 0.01),
        'exp_w1': nrm((L, E, D, 2 * F), D ** -0.5),
        'exp_b1': nrm((L, E, 2 * F), 0.02),
        'exp_w2': nrm((L, E, F, D), F ** -0.5),
        'exp_b2': nrm((L, E, D), 0.02),
    }


def reference(x, c, ada_w, ada_b, norm1_g, norm2_g, w_in, conv_w, conv_b, mlstm_gate_b, mlstm_norm_g,
              cmp_pos, cmp_k_w1, cmp_k_w2, cmp_v_w1, cmp_v_w2, qnorm_g, knorm_g, rel_bias,
              gmlp_norm_g, gmlp_ws, gmlp_b, w_branch, w_gate, w_out, router_w, router_b,
              exp_w1, exp_b1, exp_w2, exp_b2):
    for l in range(DEPTH):
        mod = jax.nn.silu(c) @ ada_w[l] + ada_b[l]
        sh1, sc1, g1, sh2, sc2, g2 = [m[:, None, :] for m in jnp.split(mod, 6, axis=-1)]
        h = rms_norm(x, norm1_g[l]) * (1.0 + sc1) + sh1
        x = x + g1 * token_mixers(h, w_in[l], conv_w[l], conv_b[l], mlstm_gate_b[l], mlstm_norm_g[l],
                                  cmp_pos[l], cmp_k_w1[l], cmp_k_w2[l], cmp_v_w1[l], cmp_v_w2[l],
                                  qnorm_g[l], knorm_g[l], rel_bias, gmlp_norm_g[l], gmlp_ws[l], gmlp_b[l],
                                  w_branch[l], w_gate[l], w_out[l])
        h = rms_norm(x, norm2_g[l]) * (1.0 + sc2) + sh2
        x = x + g2 * moe_ffn(h, router_w[l], router_b[l], exp_w1[l], exp_b1[l], exp_w2[l], exp_b2[l])
    return x
```

```python
import functools
import math

import jax
import jax.numpy as jnp
import numpy as np
from jax import lax
from jax.experimental import pallas as pl
from jax.experimental.pallas import tpu as pltpu

F32 = jnp.float32
BF16 = jnp.bfloat16

LANES = 128
VMEM_LIMIT = 56 * 1024 * 1024

D_MODEL = 2048
BRANCH_WIDTH = 1024
MLSTM_HEADS = 4
MLSTM_DV = 256
MLSTM_DQK = 128
MLSTM_TILE = 256
CONV_WIDTH = 4
NSA_HEADS = 8
NSA_G = 2
NSA_HPG = 4
NSA_DH = 128
CMP_LEN = 32
CMP_STRIDE = 16
SLC_LEN = 64
NSA_N_SELECT = 16
WINDOW = 512
Q_BLOCK = 128
NUM_BUCKETS = 32
MAX_DISTANCE = 1024
GMLP_CHUNK = 128
GMLP_GROUPS = 8
N_EXPERTS = 32
TOP_K = 4
D_FF = 1536
SWIGLU_LIMIT = 7.0
SWIGLU_ALPHA = 1.702
NEG_INF = -1e30
FORCE_SELECT = 1e4
EPS = 1e-6

SEL_TILE = 512
N_BIAS_TILES = 10

Z_QK, Z_V, Z_OG, Z_NQ, Z_GU, Z_GV = 0, 1024, 2048, 3072, 4096, 5120
Z_KV = 6144
Z_GATE = 7680
Z_SMALL = 13824
Z_WIDTH = 14336


def _cp(*sem):
    return pltpu.CompilerParams(dimension_semantics=sem, vmem_limit_bytes=VMEM_LIMIT)


def _ada_kernel(c_ref, w_ref, b_ref, o_ref):
    a = jax.nn.silu(c_ref[...]).astype(BF16)
    o_ref[...] = jnp.dot(a, w_ref[...].astype(BF16), preferred_element_type=F32) + b_ref[...]


def ada_mod(c, w, b):
    bsz, d = c.shape
    n = w.shape[1]
    tn = 1024
    cp = jnp.zeros((8, d), F32).at[:bsz].set(c)
    out = pl.pallas_call(
        _ada_kernel,
        grid=(n // tn,),
        in_specs=[pl.BlockSpec((8, d), lambda j: (0, 0)),
                  pl.BlockSpec((d, tn), lambda j: (0, j)),
                  pl.BlockSpec((1, tn), lambda j: (0, j))],
        out_specs=pl.BlockSpec((8, tn), lambda j: (0, j)),
        out_shape=jax.ShapeDtypeStruct((8, n), F32),
        compiler_params=_cp("arbitrary"),
        name="ada_mod",
    )(cp, w, b.reshape(1, n))
    return out[:bsz]


def _in_kernel(x_ref, g_ref, sc_ref, sh_ref, w_ref, o_ref, h_ref):
    @pl.when(pl.program_id(1) == 0)
    def _():
        x = x_ref[...]
        y = x * lax.rsqrt(jnp.mean(x * x, axis=-1, keepdims=True) + EPS)
        h = (y * g_ref[...]) * (1.0 + sc_ref[0]) + sh_ref[0]
        h_ref[...] = h.astype(BF16)

    o_ref[...] = jnp.dot(h_ref[...], w_ref[...], preferred_element_type=F32)


def in_proj(x2d, g, sc, sh, wcat, seq, tm=1024, tn=512):
    n, d = x2d.shape
    w = wcat.shape[1]
    per_b = seq // tm
    return pl.pallas_call(
        _in_kernel,
        grid=(n // tm, w // tn),
        in_specs=[pl.BlockSpec((tm, d), lambda i, j: (i, 0)),
                  pl.BlockSpec((1, d), lambda i, j: (0, 0)),
                  pl.BlockSpec((1, 1, d), lambda i, j: (i // per_b, 0, 0)),
                  pl.BlockSpec((1, 1, d), lambda i, j: (i // per_b, 0, 0)),
                  pl.BlockSpec((d, tn), lambda i, j: (0, j))],
        out_specs=pl.BlockSpec((tm, tn), lambda i, j: (i, j)),
        out_shape=jax.ShapeDtypeStruct((n, w), F32),
        scratch_shapes=[pltpu.VMEM((tm, d), BF16)],
        compiler_params=_cp("arbitrary", "arbitrary"),
        name="in_proj",
    )(x2d, g.reshape(1, d), sc[:, None, :], sh[:, None, :], wcat)


def _cumsum_rows(x, n):
    row = lax.broadcasted_iota(jnp.int32, x.shape, 0)
    s = 1
    while s < n:
        x = x + jnp.where(row >= s, pltpu.roll(x, s, 0), 0.0)
        s *= 2
    return x


def _mlstm_kernel(qk_ref, v_ref, og_ref, sm_ref, cw_ref, cb_ref, gb_ref, ng_ref, o_ref,
                  buf_ref, ct_ref, n_ref, m_ref):
    L = qk_ref.shape[0]
    H, DQK, DV = MLSTM_HEADS, MLSTM_DQK, MLSTM_DV

    @pl.when(pl.program_id(1) == 0)
    def _():
        buf_ref[0:8, :] = jnp.zeros((8, buf_ref.shape[1]), F32)
        ct_ref[...] = jnp.zeros(ct_ref.shape, F32)
        n_ref[...] = jnp.zeros(n_ref.shape, F32)
        m_ref[...] = jnp.full(m_ref.shape, NEG_INF, F32)

    buf_ref[8:8 + L, :] = qk_ref[...]
    conv = cb_ref[...] + cw_ref[3:4, :] * buf_ref[8:8 + L, :]
    for j in range(CONV_WIDTH - 1):
        conv = conv + cw_ref[j:j + 1, :] * buf_ref[5 + j:5 + j + L, :]
    qk = jax.nn.silu(conv)
    buf_ref[0:8, :] = buf_ref[L:L + 8, :]

    sm = sm_ref[...] + gb_ref[...]
    bcum = _cumsum_rows(jax.nn.log_sigmoid(sm), L)
    sm_t = sm.T
    bcum_t = bcum.T
    row = lax.broadcasted_iota(jnp.int32, (L, L), 0)
    col = lax.broadcasted_iota(jnp.int32, (L, L), 1)
    causal = row >= col

    for h in range(H):
        qf = qk[:, h * DQK:(h + 1) * DQK] * DQK ** -0.5
        q = qf.astype(BF16)
        kf = qk[:, (H + h) * DQK:(H + h + 1) * DQK]
        k = kf.astype(BF16)
        v = v_ref[:, h * DV:(h + 1) * DV].astype(BF16)
        b_col = bcum[:, H + h:H + h + 1]
        i_col = sm[:, h:h + 1]
        b_row = bcum_t[H + h:H + h + 1, :]
        i_row = sm_t[h:h + 1, :]
        m_prev = m_ref[h:h + 1, 0:1]
        a = b_col + m_prev
        dmat = jnp.where(causal, b_col - b_row + i_row, -jnp.inf)
        m_t = jnp.maximum(a, jnp.max(dmat, axis=1, keepdims=True))
        w_intra = jnp.exp(dmat - m_t)
        w_inter = jnp.exp(a - m_t)
        s = lax.dot_general(q, k, (((1,), (1,)), ((), ())), preferred_element_type=F32) * w_intra
        ct = ct_ref[h]
        nvec = n_ref[h:h + 1, :]
        num = jnp.dot(s.astype(BF16), v, preferred_element_type=F32) + w_inter * jnp.dot(
            q, ct.astype(BF16), preferred_element_type=F32)
        qn = jnp.sum(qf * nvec, axis=1, keepdims=True)
        den = jnp.sum(s, axis=1, keepdims=True) + w_inter * qn
        hm = num / jnp.maximum(jnp.abs(den), jnp.exp(-m_t))
        b_last = b_col[L - 1:L, :]
        gdec = b_last - b_col + i_col
        m_new = jnp.maximum(b_last + m_prev, jnp.max(gdec, axis=0, keepdims=True))
        ws = jnp.exp(gdec - m_new)
        wc = jnp.exp(b_last + m_prev - m_new)
        kw = ws * kf
        ct_ref[h] = wc * ct + jnp.dot(kw.T.astype(BF16), v, preferred_element_type=F32)
        n_ref[h:h + 1, :] = wc * nvec + jnp.sum(kw, axis=0, keepdims=True)
        m_ref[h:h + 1, :] = jnp.broadcast_to(m_new, (1, LANES))
        y = hm * lax.rsqrt(jnp.mean(hm * hm, axis=-1, keepdims=True) + EPS) * ng_ref[:, h * DV:(h + 1) * DV]
        og = og_ref[:, h * DV:(h + 1) * DV]
        o_ref[:, h * DV:(h + 1) * DV] = (jax.nn.sigmoid(og) * y).astype(o_ref.dtype)


def mlstm(z, bsz, seq, conv_w, conv_b, gate_b, norm_g):
    L = MLSTM_TILE
    nch = seq // L
    gb = jnp.zeros((1, LANES), F32).at[0, :2 * MLSTM_HEADS].set(gate_b)
    wide = BRANCH_WIDTH
    rowmap = lambda col: (lambda b, c: (b * nch + c, col))
    const = lambda b, c: (0, 0)
    return pl.pallas_call(
        _mlstm_kernel,
        grid=(bsz, nch),
        in_specs=[pl.BlockSpec((L, wide), rowmap(Z_QK // wide)),
                  pl.BlockSpec((L, wide), rowmap(Z_V // wide)),
                  pl.BlockSpec((L, wide), rowmap(Z_OG // wide)),
                  pl.BlockSpec((L, LANES), rowmap(Z_SMALL // LANES)),
                  pl.BlockSpec((CONV_WIDTH, wide), const),
                  pl.BlockSpec((1, wide), const),
                  pl.BlockSpec((1, LANES), const),
                  pl.BlockSpec((1, wide), const)],
        out_specs=pl.BlockSpec((L, wide), lambda b, c: (b * nch + c, 0)),
        out_shape=jax.ShapeDtypeStruct((bsz * seq, wide), BF16),
        scratch_shapes=[pltpu.VMEM((L + 8, wide), F32),
                        pltpu.VMEM((MLSTM_HEADS, MLSTM_DQK, MLSTM_DV), F32),
                        pltpu.VMEM((8, MLSTM_DQK), F32),
                        pltpu.VMEM((8, LANES), F32)],
        compiler_params=_cp("arbitrary", "arbitrary"),
        name="mlstm",
    )(z, z, z, z, conv_w, conv_b.reshape(1, wide), gb, norm_g.reshape(1, wide))


def _gmlp_kernel(u_ref, v_ref, ng_ref, ws_ref, b_ref, o_ref):
    T = u_ref.shape[0]
    C = GMLP_CHUNK
    u = jax.nn.gelu(u_ref[...])
    vg = jax.nn.gelu(v_ref[...])
    mu = jnp.mean(vg, axis=-1, keepdims=True)
    var = jnp.mean(jnp.square(vg - mu), axis=-1, keepdims=True)
    v = ((vg - mu) * lax.rsqrt(var + EPS) * ng_ref[...]).astype(BF16)
    row = lax.broadcasted_iota(jnp.int32, (C, C), 0)
    col = lax.broadcasted_iota(jnp.int32, (C, C), 1)
    for g in range(GMLP_GROUPS):
        wsc = jnp.where(row >= col, ws_ref[g], 0.0).astype(BF16)
        bcol = b_ref[:, g:g + 1]
        for n in range(T // C):
            vv = v[n * C:(n + 1) * C, g * C:(g + 1) * C]
            s = jnp.dot(wsc, vv, preferred_element_type=F32) + bcol
            o_ref[n * C:(n + 1) * C, g * C:(g + 1) * C] = (
                u[n * C:(n + 1) * C, g * C:(g + 1) * C] * s).astype(o_ref.dtype)


def gmlp(z, norm_g, ws, b, tg=512):
    n = z.shape[0]
    wide = BRANCH_WIDTH
    return pl.pallas_call(
        _gmlp_kernel,
        grid=(n // tg,),
        in_specs=[pl.BlockSpec((tg, wide), lambda i: (i, Z_GU // wide)),
                  pl.BlockSpec((tg, wide), lambda i: (i, Z_GV // wide)),
                  pl.BlockSpec((1, wide), lambda i: (0, 0)),
                  pl.BlockSpec((GMLP_GROUPS, GMLP_CHUNK, GMLP_CHUNK), lambda i: (0, 0, 0)),
                  pl.BlockSpec((GMLP_CHUNK, GMLP_GROUPS), lambda i: (0, 0))],
        out_specs=pl.BlockSpec((tg, wide), lambda i: (i, 0)),
        out_shape=jax.ShapeDtypeStruct((n, wide), BF16),
        compiler_params=_cp("arbitrary"),
        name="gmlp",
    )(z, z, norm_g.reshape(1, wide), ws, b.T)


def _prep_kernel(norm_cols, x_ref, g_ref, o_ref):
    c = pl.program_id(2)
    x = x_ref[...]
    y = x * lax.rsqrt(jnp.mean(x * x, axis=-1, keepdims=True) + EPS) * g_ref[0]
    flag = functools.reduce(jnp.logical_or, [c == nc for nc in norm_cols])
    o_ref[0, 0] = jnp.where(flag, y, x).astype(o_ref.dtype)


def _prep_pad_kernel(norm_cols, npad, x_ref, g_ref, o_ref):
    @pl.when(pl.program_id(1) < npad)
    def _():
        o_ref[...] = jnp.zeros(o_ref.shape, o_ref.dtype)

    @pl.when(pl.program_id(1) >= npad)
    def _():
        _prep_kernel(norm_cols, x_ref, g_ref, o_ref)


def nsa_prep(z, bsz, seq, qnorm_g, knorm_g, ts=512):
    nblk = seq // ts
    gq = (qnorm_g * NSA_DH ** -0.5).reshape(1, 1, LANES)
    gk = knorm_g.reshape(1, 1, LANES)
    gains_a = jnp.concatenate([jnp.tile(gq, (8, 1, 1)), jnp.tile(gk, (4, 1, 1))], axis=0)
    q0, ks0 = Z_NQ // LANES, (Z_KV + 512) // LANES
    pa = pl.pallas_call(
        functools.partial(_prep_kernel, tuple(range(10))),
        grid=(bsz, nblk, 12),
        in_specs=[pl.BlockSpec((ts, LANES), lambda b, s, c: (b * nblk + s, jnp.where(c < 8, q0 + c, ks0 + c - 8))),
                  pl.BlockSpec((1, 1, LANES), lambda b, s, c: (c, 0, 0))],
        out_specs=pl.BlockSpec((1, 1, ts, LANES), lambda b, s, c: (b, c, s, 0)),
        out_shape=jax.ShapeDtypeStruct((bsz, 12, seq, LANES), BF16),
        compiler_params=_cp("arbitrary", "arbitrary", "arbitrary"),
        name="nsa_prep_qks",
    )(z, gains_a)
    gains_w = jnp.tile(gk, (4, 1, 1))
    kw0 = (Z_KV + 1024) // LANES
    npad = WINDOW // ts
    pw = pl.pallas_call(
        functools.partial(_prep_pad_kernel, (0, 1), npad),
        grid=(bsz, nblk + npad, 4),
        in_specs=[pl.BlockSpec((ts, LANES), lambda b, s, c: (b * nblk + jnp.maximum(s - npad, 0), kw0 + c)),
                  pl.BlockSpec((1, 1, LANES), lambda b, s, c: (c, 0, 0))],
        out_specs=pl.BlockSpec((1, 1, ts, LANES), lambda b, s, c: (b, c, s, 0)),
        out_shape=jax.ShapeDtypeStruct((bsz, 4, seq + WINDOW, LANES), BF16),
        compiler_params=_cp("arbitrary", "arbitrary", "arbitrary"),
        name="nsa_prep_win",
    )(z, gains_w)
    return pa, pw


def _compress_kernel(x_ref, pos_ref, w1_ref, w2_ref, g_ref, o_ref):
    half = x_ref.shape[-1]
    x = x_ref[0, 0, 0]
    lo = jnp.dot((x + pos_ref[0, :, 0:half]).astype(BF16), w1_ref[0, 0:half, :], preferred_element_type=F32)
    hi = jnp.dot((x + pos_ref[0, :, half:2 * half]).astype(BF16), w1_ref[0, half:2 * half, :],
                 preferred_element_type=F32)
    ncp = x.shape[0]
    pre = lo + pltpu.roll(hi, ncp - 1, 0)
    y = jnp.dot(jax.nn.gelu(pre).astype(BF16), w2_ref[0], preferred_element_type=F32)
    yn = y * lax.rsqrt(jnp.mean(y * y, axis=-1, keepdims=True) + EPS) * g_ref[...]
    o_ref[0, 0, 0] = jnp.where(pl.program_id(0) == 0, yn, y).astype(o_ref.dtype)


def nsa_compress(z, bsz, seq, cmp_pos, w1k, w2k, w1v, w2v, knorm_g):
    ncp = seq // CMP_STRIDE
    kv = z[:, Z_KV:Z_KV + 512].reshape(bsz, seq, 2, NSA_G, NSA_DH)
    kv = kv.transpose(2, 0, 3, 1, 4).reshape(2, bsz, NSA_G, ncp, CMP_STRIDE * NSA_DH)
    pos = cmp_pos.reshape(1, 1, CMP_LEN * NSA_DH)
    w1 = jnp.stack([w1k, w1v]).astype(BF16)
    w2 = jnp.stack([w2k, w2v]).astype(BF16)
    kdim = CMP_STRIDE * NSA_DH
    return pl.pallas_call(
        _compress_kernel,
        grid=(2, bsz, NSA_G),
        in_specs=[pl.BlockSpec((1, 1, 1, ncp, kdim), lambda t, b, g: (t, b, g, 0, 0)),
                  pl.BlockSpec((1, 1, 2 * kdim), lambda t, b, g: (0, 0, 0)),
                  pl.BlockSpec((1, 2 * kdim, NSA_DH), lambda t, b, g: (t, 0, 0)),
                  pl.BlockSpec((1, NSA_DH, NSA_DH), lambda t, b, g: (t, 0, 0)),
                  pl.BlockSpec((1, NSA_DH), lambda t, b, g: (0, 0))],
        out_specs=pl.BlockSpec((1, 1, 1, ncp, NSA_DH), lambda t, b, g: (t, b, g, 0, 0)),
        out_shape=jax.ShapeDtypeStruct((2, bsz, NSA_G, ncp, NSA_DH), BF16),
        compiler_params=_cp("arbitrary", "arbitrary", "arbitrary"),
        name="nsa_compress",
    )(kv, pos, w1, w2, knorm_g.reshape(1, NSA_DH))


def _dist_tile(bt_ref, h, rho):
    return bt_ref[0, h, jnp.clip(rho, 0, N_BIAS_TILES - 1)]


def _nsa_kernel(q_ref, kc_ref, vc_ref, bc_ref, ov_ref, ks_ref, vs_ref, kw_ref, vw_ref, bt_ref, gt_ref,
                o_ref, m_ref, l_ref, acc_ref):
    j = pl.program_id(2)
    QB, HPG, DH = Q_BLOCK, NSA_HPG, NSA_DH
    R = HPG * QB
    ncp = kc_ref.shape[3]
    nsel = ov_ref.shape[1]
    q = q_ref[0].reshape(R, DH)
    nt = (((1,), (1,)), ((), ()))
    qi = lax.broadcasted_iota(jnp.int32, (QB, 1), 0)
    tpos = j * QB + qi

    cidx = lax.broadcasted_iota(jnp.int32, (QB, ncp), 1)
    mask_c = jnp.logical_and(cidx * CMP_STRIDE + (CMP_LEN - 1) <= tpos, cidx < ncp - 1)
    s_c = lax.dot_general(q, kc_ref[0, 0, 0], nt, preferred_element_type=F32).reshape(HPG, QB, ncp)
    s_c = jnp.where(mask_c[None], s_c + bc_ref[...], NEG_INF)
    mx = jnp.max(s_c, axis=-1, keepdims=True)
    e = jnp.where(mask_c[None], jnp.exp(s_c - mx), 0.0)
    den = jnp.sum(e, axis=-1, keepdims=True)
    p_c = e * jnp.where(den > 0.0, 1.0 / den, 0.0)
    o_c = jnp.dot(p_c.reshape(R, ncp).astype(BF16), vc_ref[0, 0, 0], preferred_element_type=F32)

    psum = jnp.sum(p_c, axis=0)
    p_hi = psum.astype(BF16)
    p_lo = (psum - p_hi.astype(F32)).astype(BF16)
    imp = (jnp.dot(p_hi, ov_ref[...], preferred_element_type=F32)
           + jnp.dot(p_lo, ov_ref[...], preferred_element_type=F32))
    nidx = lax.broadcasted_iota(jnp.int32, (QB, nsel), 1)
    blk_t = tpos // SLC_LEN
    valid = nidx * SLC_LEN <= tpos
    forced = (nidx == 0) | (nidx == blk_t) | (nidx == blk_t - 1)
    score = jnp.where(valid, imp + jnp.where(forced, FORCE_SELECT, 0.0), NEG_INF)
    nidx_f = nidx.astype(F32)
    sel_neg = jnp.full((QB, nsel), NEG_INF, F32)
    for _ in range(min(NSA_N_SELECT, nsel)):
        best = jnp.max(score, axis=-1, keepdims=True)
        first = jnp.min(jnp.where(score == best, nidx_f, float(nsel)), axis=-1, keepdims=True)
        hit = nidx_f == first
        sel_neg = jnp.where(hit, 0.0, sel_neg)
        score = jnp.where(hit, -jnp.inf, score)
    sel_neg = sel_neg.astype(BF16)

    m_ref[...] = jnp.full(m_ref.shape, -jnp.inf, F32)
    l_ref[...] = jnp.zeros(l_ref.shape, F32)
    acc_ref[...] = jnp.zeros(acc_ref.shape, F32)
    KT = SEL_TILE
    sub = KT // QB
    bpt = KT // SLC_LEN
    kk_i = lax.broadcasted_iota(jnp.int32, (QB, KT), 1)
    qq_i = lax.broadcasted_iota(jnp.int32, (QB, KT), 0)
    e_row = lax.broadcasted_iota(jnp.int32, (nsel, KT), 0)
    e_col = lax.broadcasted_iota(jnp.int32, (nsel, KT), 1) // SLC_LEN

    def sel_step(t, carry):
        k0 = pl.multiple_of(t * KT, KT)
        k_t = ks_ref[0, 0, pl.ds(k0, KT), :]
        v_t = vs_ref[0, 0, pl.ds(k0, KT), :]
        expand = jnp.where(e_row == e_col + t * bpt, 1.0, 0.0).astype(BF16)
        madd = jnp.dot(sel_neg, expand, preferred_element_type=F32)
        madd = jnp.where(j * QB + qq_i - (k0 + kk_i) >= 0, madd, NEG_INF)
        s = lax.dot_general(q, k_t, nt, preferred_element_type=F32)
        for h in range(HPG):
            bias = jnp.concatenate([_dist_tile(bt_ref, h, j - t * sub - u) for u in range(sub)], axis=1)
            sh = s[h * QB:(h + 1) * QB, :] + bias + madd
            m_old = m_ref[h * QB:(h + 1) * QB, :]
            m_new = jnp.maximum(m_old, jnp.max(sh, axis=-1, keepdims=True))
            alpha = jnp.exp(m_old - m_new)
            p = jnp.exp(sh - m_new)
            l_ref[h * QB:(h + 1) * QB, :] = alpha * l_ref[h * QB:(h + 1) * QB, :] + jnp.sum(p, axis=-1, keepdims=True)
            acc_ref[h * QB:(h + 1) * QB, :] = alpha * acc_ref[h * QB:(h + 1) * QB, :] + jnp.dot(
                p.astype(BF16), v_t, preferred_element_type=F32)
            m_ref[h * QB:(h + 1) * QB, :] = m_new
        return carry

    lax.fori_loop(0, (j * QB) // KT + 1, sel_step, 0)
    o_s = acc_ref[...] / l_ref[...]

    WK = WINDOW + QB
    w0 = pl.multiple_of(j * QB, QB)
    k_w = kw_ref[0, 0, pl.ds(w0, WK), :]
    v_w = vw_ref[0, 0, pl.ds(w0, WK), :]
    wk_i = lax.broadcasted_iota(jnp.int32, (QB, WK), 1)
    wq_i = lax.broadcasted_iota(jnp.int32, (QB, WK), 0)
    dist_w = wq_i - wk_i + WINDOW
    mask_w = (dist_w >= 0) & (dist_w < WINDOW) & (j * QB - WINDOW + wk_i >= 0)
    madd_w = jnp.where(mask_w, 0.0, NEG_INF)
    s_w = lax.dot_general(q, k_w, nt, preferred_element_type=F32)
    nwt = WK // QB
    o_w = []
    for h in range(HPG):
        bias = jnp.concatenate([_dist_tile(bt_ref, h, nwt - 1 - u) for u in range(nwt)], axis=1)
        sh = s_w[h * QB:(h + 1) * QB, :] + bias + madd_w
        mxw = jnp.max(sh, axis=-1, keepdims=True)
        pw = jnp.exp(sh - mxw)
        lw = jnp.sum(pw, axis=-1, keepdims=True)
        o_w.append(jnp.dot(pw.astype(BF16), v_w, preferred_element_type=F32) / lw)

    gate = jax.nn.sigmoid(gt_ref[...])
    for h in range(HPG):
        gc = gate[:, 8 + 3 * h:9 + 3 * h]
        gs = gate[:, 9 + 3 * h:10 + 3 * h]
        gw = gate[:, 10 + 3 * h:11 + 3 * h]
        y = gc * o_c[h * QB:(h + 1) * QB, :] + gs * o_s[h * QB:(h + 1) * QB, :] + gw * o_w[h]
        o_ref[:, h * DH:(h + 1) * DH] = y.astype(o_ref.dtype)


def nsa_tables(rel_bias, seq):
    d = jnp.arange(seq)
    max_exact = NUM_BUCKETS // 2
    log_ratio = jnp.log(jnp.maximum(d, 1).astype(F32) / max_exact) / math.log(MAX_DISTANCE / max_exact)
    large = jnp.minimum(max_exact + (log_ratio * (NUM_BUCKETS - max_exact)).astype(jnp.int32), NUM_BUCKETS - 1)
    bucket = jnp.where(d < max_exact, d, large)
    tab = rel_bias.astype(F32)[bucket].T
    i = jnp.arange(Q_BLOCK)
    dist = (jnp.arange(N_BIAS_TILES)[:, None, None] * Q_BLOCK + i[None, :, None] - i[None, None, :])
    bt = tab[:, jnp.clip(dist, 0, seq - 1)]
    bt = bt.reshape(NSA_G, NSA_HPG, N_BIAS_TILES, Q_BLOCK, Q_BLOCK)
    ncp = seq // CMP_STRIDE
    cend = jnp.arange(ncp) * CMP_STRIDE + CMP_LEN - 1
    bc = tab[:, jnp.clip(jnp.arange(seq)[:, None] - cend[None, :], 0, seq - 1)]
    return bt, bc


def nsa_overlap(seq):
    ncp = seq // CMP_STRIDE
    ns = seq // SLC_LEN
    cstart = np.arange(ncp) * CMP_STRIDE
    sstart = np.arange(ns) * SLC_LEN
    ov = np.clip(np.minimum(cstart[:, None] + CMP_LEN, sstart[None, :] + SLC_LEN)
                 - np.maximum(cstart[:, None], sstart[None, :]), 0, None).astype(np.float32) / CMP_LEN
    ov[ncp - 1] = 0.0
    return jnp.asarray(ov, BF16)


def nsa_attend(z, pa, pw, cmp, bt, bc, ov, bsz, seq):
    nqb = seq // Q_BLOCK
    ncp = seq // CMP_STRIDE
    nsel = seq // SLC_LEN
    R = NSA_HPG * Q_BLOCK
    return pl.pallas_call(
        _nsa_kernel,
        grid=(bsz, NSA_G, nqb),
        in_specs=[pl.BlockSpec((1, NSA_HPG, Q_BLOCK, NSA_DH), lambda b, g, j: (b, g, j, 0)),
                  pl.BlockSpec((1, 1, 1, ncp, NSA_DH), lambda b, g, j: (0, b, g, 0, 0)),
                  pl.BlockSpec((1, 1, 1, ncp, NSA_DH), lambda b, g, j: (1, b, g, 0, 0)),
                  pl.BlockSpec((NSA_HPG, Q_BLOCK, ncp), lambda b, g, j: (g, j, 0)),
                  pl.BlockSpec((ncp, nsel), lambda b, g, j: (0, 0)),
                  pl.BlockSpec((1, 1, seq, NSA_DH), lambda b, g, j: (b, 8 + g, 0, 0)),
                  pl.BlockSpec((1, 1, seq, NSA_DH), lambda b, g, j: (b, 10 + g, 0, 0)),
                  pl.BlockSpec((1, 1, seq + WINDOW, NSA_DH), lambda b, g, j: (b, g, 0, 0)),
                  pl.BlockSpec((1, 1, seq + WINDOW, NSA_DH), lambda b, g, j: (b, 2 + g, 0, 0)),
                  pl.BlockSpec((1, NSA_HPG, N_BIAS_TILES, Q_BLOCK, Q_BLOCK), lambda b, g, j: (g, 0, 0, 0, 0)),
                  pl.BlockSpec((Q_BLOCK, LANES), lambda b, g, j: (b * nqb + j, Z_SMALL // LANES + g))],
        out_specs=pl.BlockSpec((Q_BLOCK, NSA_HPG * NSA_DH), lambda b, g, j: (b * nqb + j, g)),
        out_shape=jax.ShapeDtypeStruct((bsz * seq, BRANCH_WIDTH), BF16),
        scratch_shapes=[pltpu.VMEM((R, 1), F32), pltpu.VMEM((R, 1), F32), pltpu.VMEM((R, NSA_DH), F32)],
        compiler_params=_cp("arbitrary", "arbitrary", "arbitrary"),
        name="nsa_attend",
    )(pa, cmp, cmp, bc, ov, pa, pa, pw, pw, bt, z)


def _merge_kernel(ya_ref, yb_ref, yc_ref, w_ref, ga_ref, gb_ref, gc_ref, o_ref):
    acc = jax.nn.sigmoid(ga_ref[...]) * jnp.dot(ya_ref[...], w_ref[0], preferred_element_type=F32)
    acc = acc + jax.nn.sigmoid(gb_ref[...]) * jnp.dot(yb_ref[...], w_ref[1], preferred_element_type=F32)
    acc = acc + jax.nn.sigmoid(gc_ref[...]) * jnp.dot(yc_ref[...], w_ref[2], preferred_element_type=F32)
    o_ref[...] = acc.astype(o_ref.dtype)


def merge(ya, yb, yc, w_branch, z, tm=1024, tn=512):
    n = ya.shape[0]
    d = w_branch.shape[2]
    bw = ya.shape[1]
    ymap = lambda i, j: (i, 0)
    gmap = lambda k: (lambda i, j: (i, (Z_GATE + k * d) // tn + j))
    return pl.pallas_call(
        _merge_kernel,
        grid=(n // tm, d // tn),
        in_specs=[pl.BlockSpec((tm, bw), ymap), pl.BlockSpec((tm, bw), ymap), pl.BlockSpec((tm, bw), ymap),
                  pl.BlockSpec((3, bw, tn), lambda i, j: (0, 0, j)),
                  pl.BlockSpec((tm, tn), gmap(0)), pl.BlockSpec((tm, tn), gmap(1)), pl.BlockSpec((tm, tn), gmap(2))],
        out_specs=pl.BlockSpec((tm, tn), lambda i, j: (i, j)),
        out_shape=jax.ShapeDtypeStruct((n, d), BF16),
        compiler_params=_cp("arbitrary", "arbitrary"),
        name="merge",
    )(ya, yb, yc, w_branch, z, z, z)


def _outproj_kernel(a_ref, w_ref, x_ref, g_ref, o_ref):
    o_ref[...] = x_ref[...] + g_ref[0] * jnp.dot(a_ref[...], w_ref[...], preferred_element_type=F32)


def out_proj(a, w, x2d, gate, seq, tm=1024, tn=512):
    n, k = a.shape
    d = w.shape[1]
    per_b = seq // tm
    return pl.pallas_call(
        _outproj_kernel,
        grid=(n // tm, d // tn),
        in_specs=[pl.BlockSpec((tm, k), lambda i, j: (i, 0)),
                  pl.BlockSpec((k, tn), lambda i, j: (0, j)),
                  pl.BlockSpec((tm, tn), lambda i, j: (i, j)),
                  pl.BlockSpec((1, 1, tn), lambda i, j: (i // per_b, 0, j))],
        out_specs=pl.BlockSpec((tm, tn), lambda i, j: (i, j)),
        out_shape=jax.ShapeDtypeStruct((n, d), F32),
        compiler_params=_cp("arbitrary", "arbitrary"),
        name="out_proj",
    )(a, w, x2d, gate[:, None, :])


def _router_kernel(x_ref, g_ref, sc_ref, sh_ref, rw_ref, rwl_ref, rb_ref, h_ref, idx_ref, p_ref):
    x = x_ref[...]
    y = x * lax.rsqrt(jnp.mean(x * x, axis=-1, keepdims=True) + EPS)
    hf = (y * g_ref[...]) * (1.0 + sc_ref[0]) + sh_ref[0]
    h = hf.astype(BF16)
    h_ref[...] = h
    h_lo = (hf - h.astype(F32)).astype(BF16)
    logits = (jnp.dot(h, rw_ref[...], preferred_element_type=F32)
              + jnp.dot(h_lo, rw_ref[...], preferred_element_type=F32)
              + jnp.dot(h, rwl_ref[...], preferred_element_type=F32)) + rb_ref[...]
    lane = lax.broadcasted_iota(jnp.int32, logits.shape, 1)
    lane_f = lane.astype(F32)
    idx_out = jnp.zeros(logits.shape, F32)
    val_out = jnp.full(logits.shape, -jnp.inf, F32)
    for k in range(TOP_K):
        best = jnp.max(logits, axis=-1, keepdims=True)
        first = jnp.min(jnp.where(logits == best, lane_f, float(LANES)), axis=-1, keepdims=True)
        idx_out = jnp.where(lane == k, first, idx_out)
        val_out = jnp.where(lane == k, best, val_out)
        logits = jnp.where(lane_f == first, -jnp.inf, logits)
    e = jnp.exp(val_out - jnp.max(val_out, axis=-1, keepdims=True))
    idx_ref[...] = idx_out.astype(jnp.int32)
    p_ref[...] = e / jnp.sum(e, axis=-1, keepdims=True)


def router(x2d, g, sc, sh, rw, rb, seq, tm=1024):
    n, d = x2d.shape
    per_b = seq // tm
    rw_hi = rw.astype(BF16)
    rw_lo = (rw - rw_hi.astype(F32)).astype(BF16)
    rwp = jnp.zeros((d, LANES), BF16).at[:, :N_EXPERTS].set(rw_hi)
    rwl = jnp.zeros((d, LANES), BF16).at[:, :N_EXPERTS].set(rw_lo)
    rbp = jnp.full((1, LANES), -jnp.inf, F32).at[0, :N_EXPERTS].set(rb)
    return pl.pallas_call(
        _router_kernel,
        grid=(n // tm,),
        in_specs=[pl.BlockSpec((tm, d), lambda i: (i, 0)),
                  pl.BlockSpec((1, d), lambda i: (0, 0)),
                  pl.BlockSpec((1, 1, d), lambda i: (i // per_b, 0, 0)),
                  pl.BlockSpec((1, 1, d), lambda i: (i // per_b, 0, 0)),
                  pl.BlockSpec((d, LANES), lambda i: (0, 0)),
                  pl.BlockSpec((d, LANES), lambda i: (0, 0)),
                  pl.BlockSpec((1, LANES), lambda i: (0, 0))],
        out_specs=[pl.BlockSpec((tm, d), lambda i: (i, 0)),
                   pl.BlockSpec((tm, LANES), lambda i: (i, 0)),
                   pl.BlockSpec((tm, LANES), lambda i: (i, 0))],
        out_shape=[jax.ShapeDtypeStruct((n, d), BF16),
                   jax.ShapeDtypeStruct((n, LANES), jnp.int32),
                   jax.ShapeDtypeStruct((n, LANES), F32)],
        compiler_params=_cp("arbitrary"),
        name="router",
    )(x2d, g.reshape(1, d), sc[:, None, :], sh[:, None, :], rwp, rwl, rbp)


def _expert_kernel(be_ref, x_ref, wg_ref, wu_ref, bg_ref, bu_ref, w2_ref, b2_ref, p_ref, o_ref, acc_ref):
    f = pl.program_id(1)

    @pl.when(f == 0)
    def _():
        acc_ref[...] = jnp.zeros(acc_ref.shape, F32)

    x = x_ref[...]
    gate = jnp.dot(x, wg_ref[0].astype(BF16), preferred_element_type=F32) + bg_ref[0]
    up = jnp.dot(x, wu_ref[0].astype(BF16), preferred_element_type=F32) + bu_ref[0]
    gate = jnp.minimum(gate, SWIGLU_LIMIT)
    up = jnp.clip(up, -SWIGLU_LIMIT, SWIGLU_LIMIT)
    act = (up + 1.0) * (gate * jax.nn.sigmoid(SWIGLU_ALPHA * gate))
    acc_ref[...] += jnp.dot(act.astype(BF16), w2_ref[0].astype(BF16), preferred_element_type=F32)

    @pl.when(f == pl.num_programs(1) - 1)
    def _():
        o_ref[...] = ((acc_ref[...] + b2_ref[0]) * p_ref[...]).astype(o_ref.dtype)


def experts(xb, block_exp, w1, b1, w2, b2, prow, tm, tf=256):
    rows, d = xb.shape
    nf = D_FF // tf
    nblk = rows // tm
    grid_spec = pltpu.PrefetchScalarGridSpec(
        num_scalar_prefetch=1,
        grid=(nblk, nf),
        in_specs=[pl.BlockSpec((tm, d), lambda i, f, be: (i, 0)),
                  pl.BlockSpec((1, d, tf), lambda i, f, be: (be[i], 0, f)),
                  pl.BlockSpec((1, d, tf), lambda i, f, be: (be[i], 0, nf + f)),
                  pl.BlockSpec((1, 1, tf), lambda i, f, be: (be[i], 0, f)),
                  pl.BlockSpec((1, 1, tf), lambda i, f, be: (be[i], 0, nf + f)),
                  pl.BlockSpec((1, tf, d), lambda i, f, be: (be[i], f, 0)),
                  pl.BlockSpec((1, 1, d), lambda i, f, be: (be[i], 0, 0)),
                  pl.BlockSpec((tm, 1), lambda i, f, be: (i, 0))],
        out_specs=pl.BlockSpec((tm, d), lambda i, f, be: (i, 0)),
        scratch_shapes=[pltpu.VMEM((tm, d), F32)],
    )
    return pl.pallas_call(
        _expert_kernel,
        grid_spec=grid_spec,
        out_shape=jax.ShapeDtypeStruct((rows, d), F32),
        compiler_params=_cp("arbitrary", "arbitrary"),
        name="experts",
    )(block_exp, xb, w1, w1, b1[:, None, :], b1[:, None, :], w2, b2[:, None, :], prow)


def _combine_kernel(y_ref, x_ref, g_ref, o_ref):
    d = x_ref.shape[1]
    y = y_ref[:, 0:d]
    for k in range(1, TOP_K):
        y = y + y_ref[:, k * d:(k + 1) * d]
    o_ref[...] = x_ref[...] + g_ref[0] * y


def combine(yg, x2d, gate, seq, tm=256):
    n, d = x2d.shape
    per_b = seq // tm
    return pl.pallas_call(
        _combine_kernel,
        grid=(n // tm,),
        in_specs=[pl.BlockSpec((tm, TOP_K * d), lambda i: (i, 0)),
                  pl.BlockSpec((tm, d), lambda i: (i, 0)),
                  pl.BlockSpec((1, 1, d), lambda i: (i // per_b, 0, 0))],
        out_specs=pl.BlockSpec((tm, d), lambda i: (i, 0)),
        out_shape=jax.ShapeDtypeStruct((n, d), F32),
        compiler_params=_cp("arbitrary"),
        name="combine",
    )(yg, x2d, gate[:, None, :])


def moe(x2d, g, sc, sh, gate, rw, rb, w1, b1, w2, b2, seq, tm_e=1024):
    n, d = x2d.shape
    h, idx, prob = router(x2d, g, sc, sh, rw, rb, seq)
    e_flat = idx[:, :TOP_K].reshape(-1)
    p_flat = prob[:, :TOP_K].reshape(-1)
    na = n * TOP_K
    onehot = (e_flat[:, None] == jnp.arange(N_EXPERTS)[None, :]).astype(jnp.int32)
    csum = jnp.cumsum(onehot, axis=0)
    counts = csum[-1]
    rank = jnp.take_along_axis(csum, e_flat[:, None], axis=1)[:, 0] - 1
    padded = (counts + tm_e - 1) // tm_e * tm_e
    pend = jnp.cumsum(padded)
    pstart = pend - padded
    dest = pstart[e_flat] + rank
    nblk = na // tm_e + N_EXPERTS
    rows = nblk * tm_e
    buf_tok = jnp.zeros((rows,), jnp.int32).at[dest].set(jnp.arange(na, dtype=jnp.int32) // TOP_K)
    prow = jnp.zeros((rows,), F32).at[dest].set(p_flat)
    block_exp = jnp.minimum(jnp.searchsorted(pend, jnp.arange(nblk) * tm_e, side='right'),
                            N_EXPERTS - 1).astype(jnp.int32)
    xb = h[buf_tok]
    yb = experts(xb, block_exp, w1, b1, w2, b2, prow[:, None], tm_e)
    yg = yb[dest].reshape(n, TOP_K * d)
    return combine(yg, x2d, gate, seq)


def fused_in_weight(w_in, w_gate):
    sizes = (512, 512, 1024, 4, 4, 1024, 1024, 256, 256, 256, 256, 256, 256, 24, 1024, 1024)
    pts = np.cumsum(sizes)[:-1].tolist()
    (qm, km, vm, ig, fg, og, qn, kc, vc, ks, vs, kw, vw, gn, gu, gv) = jnp.split(w_in, pts, axis=1)
    d = w_in.shape[0]
    pad = lambda k: jnp.zeros((d, k), w_in.dtype)
    per_group = NSA_HPG * 3
    small0 = jnp.concatenate([ig, fg, gn[:, :per_group], pad(LANES - 8 - per_group)], axis=1)
    small1 = jnp.concatenate([pad(8), gn[:, per_group:], pad(LANES - 8 - per_group)], axis=1)
    cols = [qm, km, vm, og, qn, gu, gv, kc, vc, ks, vs, kw, vw, w_gate[0], w_gate[1], w_gate[2], small0, small1]
    w = jnp.concatenate(cols, axis=1)
    w = jnp.concatenate([w, pad(Z_WIDTH - w.shape[1])], axis=1)
    return w.astype(BF16)


def kernel(x, c, ada_w, ada_b, norm1_g, norm2_g, w_in, conv_w, conv_b, mlstm_gate_b, mlstm_norm_g, cmp_pos,
           cmp_k_w1, cmp_k_w2, cmp_v_w1, cmp_v_w2, qnorm_g, knorm_g, rel_bias, gmlp_norm_g, gmlp_ws, gmlp_b,
           w_branch, w_gate, w_out, router_w, router_b, exp_w1, exp_b1, exp_w2, exp_b2):
    bsz, seq, d = x.shape
    depth = ada_w.shape[0]
    x2d = x.reshape(bsz * seq, d)
    bt, bc = nsa_tables(rel_bias, seq)
    ov = nsa_overlap(seq)
    for l in range(depth):
        mod = ada_mod(c, ada_w[l], ada_b[l])
        sh1, sc1, g1, sh2, sc2, g2 = jnp.split(mod, 6, axis=-1)
        z = in_proj(x2d, norm1_g[l], sc1, sh1, fused_in_weight(w_in[l], w_gate[l]), seq)
        ya = mlstm(z, bsz, seq, conv_w[l], conv_b[l], mlstm_gate_b[l], mlstm_norm_g[l])
        pa, pw = nsa_prep(z, bsz, seq, qnorm_g[l], knorm_g[l])
        cmp = nsa_compress(z, bsz, seq, cmp_pos[l], cmp_k_w1[l], cmp_k_w2[l], cmp_v_w1[l], cmp_v_w2[l], knorm_g[l])
        yb = nsa_attend(z, pa, pw, cmp, bt, bc, ov, bsz, seq)
        yc = gmlp(z, gmlp_norm_g[l], gmlp_ws[l], gmlp_b[l])
        merged = merge(ya, yb, yc, w_branch[l].astype(BF16), z)
        x2d = out_proj(merged, w_out[l].astype(BF16), x2d, g1, seq)
        x2d = moe(x2d, norm2_g[l], sc2, sh2, g2, router_w[l], router_b[l], exp_w1[l], exp_b1[l],
                  exp_w2[l], exp_b2[l], seq)
    return x2d.reshape(bsz, seq, d)
```

```python
import functools
import math

import jax
import jax.numpy as jnp
import numpy as np
from jax import lax
from jax.experimental import pallas as pl
from jax.experimental.pallas import tpu as pltpu

F32 = jnp.float32
BF16 = jnp.bfloat16

LANES = 128
VMEM_LIMIT = 56 * 1024 * 1024

D_MODEL = 2048
BRANCH_WIDTH = 1024
MLSTM_HEADS = 4
MLSTM_DV = 256
MLSTM_DQK = 128
MLSTM_TILE = 256
CONV_WIDTH = 4
NSA_HEADS = 8
NSA_G = 2
NSA_HPG = 4
NSA_DH = 128
CMP_LEN = 32
CMP_STRIDE = 16
SLC_LEN = 64
NSA_N_SELECT = 16
WINDOW = 512
Q_BLOCK = 128
NUM_BUCKETS = 32
MAX_DISTANCE = 1024
GMLP_CHUNK = 128
GMLP_GROUPS = 8
N_EXPERTS = 32
TOP_K = 4
D_FF = 1536
SWIGLU_LIMIT = 7.0
SWIGLU_ALPHA = 1.702
NEG_INF = -1e30
FORCE_SELECT = 1e4
EPS = 1e-6

SEL_TILE = 512
N_BIAS_TILES = 10

Z_QK, Z_V, Z_OG, Z_NQ, Z_GU, Z_GV = 0, 1024, 2048, 3072, 4096, 5120
Z_KV = 6144
Z_GATE = 7680
Z_SMALL = 13824
Z_WIDTH = 14336


def _cp(*sem):
    return pltpu.CompilerParams(dimension_semantics=sem, vmem_limit_bytes=VMEM_LIMIT)


def _ada_kernel(c_ref, w_ref, b_ref, o_ref):
    a = jax.nn.silu(c_ref[...]).astype(BF16)
    o_ref[...] = jnp.dot(a, w_ref[0].astype(BF16), preferred_element_type=F32) + b_ref[0]


def ada_mod(c, w, b, layer):
    bsz, d = c.shape
    n = w.shape[2]
    tn = 1024
    cp = jnp.zeros((8, d), F32).at[:bsz].set(c)
    out = pl.pallas_call(
        _ada_kernel,
        grid=(n // tn,),
        in_specs=[pl.BlockSpec((8, d), lambda j: (0, 0)),
                  pl.BlockSpec((1, d, tn), lambda j: (layer, 0, j)),
                  pl.BlockSpec((1, 1, tn), lambda j: (layer, 0, j))],
        out_specs=pl.BlockSpec((8, tn), lambda j: (0, j)),
        out_shape=jax.ShapeDtypeStruct((8, n), F32),
        compiler_params=_cp("arbitrary"),
        name="ada_mod",
    )(cp, w, b[:, None, :])
    return out[:bsz]


def _in_kernel(x_ref, g_ref, sc_ref, sh_ref, w_ref, o_ref, h_ref):
    @pl.when(pl.program_id(1) == 0)
    def _():
        x = x_ref[...]
        y = x * lax.rsqrt(jnp.mean(x * x, axis=-1, keepdims=True) + EPS)
        h = (y * g_ref[...]) * (1.0 + sc_ref[0]) + sh_ref[0]
        h_ref[...] = h.astype(BF16)

    o_ref[...] = jnp.dot(h_ref[...], w_ref[...], preferred_element_type=F32)


def in_proj(x2d, g, sc, sh, wcat, seq, tm=1024, tn=512):
    n, d = x2d.shape
    w = wcat.shape[1]
    per_b = seq // tm
    return pl.pallas_call(
        _in_kernel,
        grid=(n // tm, w // tn),
        in_specs=[pl.BlockSpec((tm, d), lambda i, j: (i, 0)),
                  pl.BlockSpec((1, d), lambda i, j: (0, 0)),
                  pl.BlockSpec((1, 1, d), lambda i, j: (i // per_b, 0, 0)),
                  pl.BlockSpec((1, 1, d), lambda i, j: (i // per_b, 0, 0)),
                  pl.BlockSpec((d, tn), lambda i, j: (0, j))],
        out_specs=pl.BlockSpec((tm, tn), lambda i, j: (i, j)),
        out_shape=jax.ShapeDtypeStruct((n, w), F32),
        scratch_shapes=[pltpu.VMEM((tm, d), BF16)],
        compiler_params=_cp("arbitrary", "arbitrary"),
        name="in_proj",
    )(x2d, g.reshape(1, d), sc[:, None, :], sh[:, None, :], wcat)


def _cumsum_rows(x, n):
    row = lax.broadcasted_iota(jnp.int32, x.shape, 0)
    s = 1
    while s < n:
        x = x + jnp.where(row >= s, pltpu.roll(x, s, 0), 0.0)
        s *= 2
    return x


def _mlstm_kernel(qk_ref, v_ref, og_ref, sm_ref, cw_ref, cb_ref, gb_ref, ng_ref, o_ref,
                  buf_ref, ct_ref, n_ref, m_ref):
    L = qk_ref.shape[0]
    H, DQK, DV = MLSTM_HEADS, MLSTM_DQK, MLSTM_DV

    @pl.when(pl.program_id(1) == 0)
    def _():
        buf_ref[0:8, :] = jnp.zeros((8, buf_ref.shape[1]), F32)
        ct_ref[...] = jnp.zeros(ct_ref.shape, F32)
        n_ref[...] = jnp.zeros(n_ref.shape, F32)
        m_ref[...] = jnp.full(m_ref.shape, NEG_INF, F32)

    buf_ref[8:8 + L, :] = qk_ref[...]
    conv = cb_ref[...] + cw_ref[3:4, :] * buf_ref[8:8 + L, :]
    for j in range(CONV_WIDTH - 1):
        conv = conv + cw_ref[j:j + 1, :] * buf_ref[5 + j:5 + j + L, :]
    qk = jax.nn.silu(conv)
    buf_ref[0:8, :] = buf_ref[L:L + 8, :]

    sm = sm_ref[...] + gb_ref[...]
    bcum = _cumsum_rows(jax.nn.log_sigmoid(sm), L)
    sm_t = sm.T
    bcum_t = bcum.T
    row = lax.broadcasted_iota(jnp.int32, (L, L), 0)
    col = lax.broadcasted_iota(jnp.int32, (L, L), 1)
    causal = row >= col

    for h in range(H):
        qf = qk[:, h * DQK:(h + 1) * DQK] * DQK ** -0.5
        q = qf.astype(BF16)
        kf = qk[:, (H + h) * DQK:(H + h + 1) * DQK]
        k = kf.astype(BF16)
        v = v_ref[:, h * DV:(h + 1) * DV].astype(BF16)
        b_col = bcum[:, H + h:H + h + 1]
        i_col = sm[:, h:h + 1]
        b_row = bcum_t[H + h:H + h + 1, :]
        i_row = sm_t[h:h + 1, :]
        m_prev = m_ref[h:h + 1, 0:1]
        a = b_col + m_prev
        dmat = jnp.where(causal, b_col - b_row + i_row, -jnp.inf)
        m_t = jnp.maximum(a, jnp.max(dmat, axis=1, keepdims=True))
        w_intra = jnp.exp(dmat - m_t)
        w_inter = jnp.exp(a - m_t)
        s = lax.dot_general(q, k, (((1,), (1,)), ((), ())), preferred_element_type=F32) * w_intra
        ct = ct_ref[h]
        nvec = n_ref[h:h + 1, :]
        num = jnp.dot(s.astype(BF16), v, preferred_element_type=F32) + w_inter * jnp.dot(
            q, ct.astype(BF16), preferred_element_type=F32)
        qn = jnp.sum(qf * nvec, axis=1, keepdims=True)
        den = jnp.sum(s, axis=1, keepdims=True) + w_inter * qn
        hm = num / jnp.maximum(jnp.abs(den), jnp.exp(-m_t))
        b_last = b_col[L - 1:L, :]
        gdec = b_last - b_col + i_col
        m_new = jnp.maximum(b_last + m_prev, jnp.max(gdec, axis=0, keepdims=True))
        ws = jnp.exp(gdec - m_new)
        wc = jnp.exp(b_last + m_prev - m_new)
        kw = ws * kf
        ct_ref[h] = wc * ct + jnp.dot(kw.T.astype(BF16), v, preferred_element_type=F32)
        n_ref[h:h + 1, :] = wc * nvec + jnp.sum(kw, axis=0, keepdims=True)
        m_ref[h:h + 1, :] = jnp.broadcast_to(m_new, (1, LANES))
        y = hm * lax.rsqrt(jnp.mean(hm * hm, axis=-1, keepdims=True) + EPS) * ng_ref[:, h * DV:(h + 1) * DV]
        og = og_ref[:, h * DV:(h + 1) * DV]
        o_ref[:, h * DV:(h + 1) * DV] = (jax.nn.sigmoid(og) * y).astype(o_ref.dtype)


def mlstm(z, bsz, seq, conv_w, conv_b, gate_b, norm_g):
    L = MLSTM_TILE
    nch = seq // L
    gb = jnp.zeros((1, LANES), F32).at[0, :2 * MLSTM_HEADS].set(gate_b)
    wide = BRANCH_WIDTH
    rowmap = lambda col: (lambda b, c: (b * nch + c, col))
    const = lambda b, c: (0, 0)
    return pl.pallas_call(
        _mlstm_kernel,
        grid=(bsz, nch),
        in_specs=[pl.BlockSpec((L, wide), rowmap(Z_QK // wide)),
                  pl.BlockSpec((L, wide), rowmap(Z_V // wide)),
                  pl.BlockSpec((L, wide), rowmap(Z_OG // wide)),
                  pl.BlockSpec((L, LANES), rowmap(Z_SMALL // LANES)),
                  pl.BlockSpec((CONV_WIDTH, wide), const),
                  pl.BlockSpec((1, wide), const),
                  pl.BlockSpec((1, LANES), const),
                  pl.BlockSpec((1, wide), const)],
        out_specs=pl.BlockSpec((L, wide), lambda b, c: (b * nch + c, 0)),
        out_shape=jax.ShapeDtypeStruct((bsz * seq, wide), BF16),
        scratch_shapes=[pltpu.VMEM((L + 8, wide), F32),
                        pltpu.VMEM((MLSTM_HEADS, MLSTM_DQK, MLSTM_DV), F32),
                        pltpu.VMEM((8, MLSTM_DQK), F32),
                        pltpu.VMEM((8, LANES), F32)],
        compiler_params=_cp("arbitrary", "arbitrary"),
        name="mlstm",
    )(z, z, z, z, conv_w, conv_b.reshape(1, wide), gb, norm_g.reshape(1, wide))


def _gmlp_kernel(u_ref, v_ref, ng_ref, ws_ref, b_ref, o_ref):
    T = u_ref.shape[0]
    C = GMLP_CHUNK
    u = jax.nn.gelu(u_ref[...])
    vg = jax.nn.gelu(v_ref[...])
    mu = jnp.mean(vg, axis=-1, keepdims=True)
    var = jnp.mean(jnp.square(vg - mu), axis=-1, keepdims=True)
    v = ((vg - mu) * lax.rsqrt(var + EPS) * ng_ref[...]).astype(BF16)
    row = lax.broadcasted_iota(jnp.int32, (C, C), 0)
    col = lax.broadcasted_iota(jnp.int32, (C, C), 1)
    for g in range(GMLP_GROUPS):
        wsc = jnp.where(row >= col, ws_ref[g], 0.0).astype(BF16)
        bcol = b_ref[:, g:g + 1]
        for n in range(T // C):
            vv = v[n * C:(n + 1) * C, g * C:(g + 1) * C]
            s = jnp.dot(wsc, vv, preferred_element_type=F32) + bcol
            o_ref[n * C:(n + 1) * C, g * C:(g + 1) * C] = (
                u[n * C:(n + 1) * C, g * C:(g + 1) * C] * s).astype(o_ref.dtype)


def gmlp(z, norm_g, ws, b, tg=512):
    n = z.shape[0]
    wide = BRANCH_WIDTH
    return pl.pallas_call(
        _gmlp_kernel,
        grid=(n // tg,),
        in_specs=[pl.BlockSpec((tg, wide), lambda i: (i, Z_GU // wide)),
                  pl.BlockSpec((tg, wide), lambda i: (i, Z_GV // wide)),
                  pl.BlockSpec((1, wide), lambda i: (0, 0)),
                  pl.BlockSpec((GMLP_GROUPS, GMLP_CHUNK, GMLP_CHUNK), lambda i: (0, 0, 0)),
                  pl.BlockSpec((GMLP_CHUNK, GMLP_GROUPS), lambda i: (0, 0))],
        out_specs=pl.BlockSpec((tg, wide), lambda i: (i, 0)),
        out_shape=jax.ShapeDtypeStruct((n, wide), BF16),
        compiler_params=_cp("arbitrary"),
        name="gmlp",
    )(z, z, norm_g.reshape(1, wide), ws, b.T)


def _prep_kernel(norm_cols, x_ref, g_ref, o_ref):
    c = pl.program_id(2)
    x = x_ref[...]
    y = x * lax.rsqrt(jnp.mean(x * x, axis=-1, keepdims=True) + EPS) * g_ref[0]
    flag = functools.reduce(jnp.logical_or, [c == nc for nc in norm_cols])
    o_ref[0, 0] = jnp.where(flag, y, x).astype(o_ref.dtype)


def _prep_pad_kernel(norm_cols, npad, x_ref, g_ref, o_ref):
    @pl.when(pl.program_id(1) < npad)
    def _():
        o_ref[...] = jnp.zeros(o_ref.shape, o_ref.dtype)

    @pl.when(pl.program_id(1) >= npad)
    def _():
        _prep_kernel(norm_cols, x_ref, g_ref, o_ref)


def nsa_prep(z, bsz, seq, qnorm_g, knorm_g, ts=512):
    nblk = seq // ts
    gq = (qnorm_g * NSA_DH ** -0.5).reshape(1, 1, LANES)
    gk = knorm_g.reshape(1, 1, LANES)
    gains_a = jnp.concatenate([jnp.tile(gq, (8, 1, 1)), jnp.tile(gk, (4, 1, 1))], axis=0)
    q0, ks0 = Z_NQ // LANES, (Z_KV + 512) // LANES
    pa = pl.pallas_call(
        functools.partial(_prep_kernel, tuple(range(10))),
        grid=(bsz, nblk, 12),
        in_specs=[pl.BlockSpec((ts, LANES), lambda b, s, c: (b * nblk + s, jnp.where(c < 8, q0 + c, ks0 + c - 8))),
                  pl.BlockSpec((1, 1, LANES), lambda b, s, c: (c, 0, 0))],
        out_specs=pl.BlockSpec((1, 1, ts, LANES), lambda b, s, c: (b, c, s, 0)),
        out_shape=jax.ShapeDtypeStruct((bsz, 12, seq, LANES), BF16),
        compiler_params=_cp("arbitrary", "arbitrary", "arbitrary"),
        name="nsa_prep_qks",
    )(z, gains_a)
    gains_w = jnp.tile(gk, (4, 1, 1))
    kw0 = (Z_KV + 1024) // LANES
    npad = WINDOW // ts
    pw = pl.pallas_call(
        functools.partial(_prep_pad_kernel, (0, 1), npad),
        grid=(bsz, nblk + npad, 4),
        in_specs=[pl.BlockSpec((ts, LANES), lambda b, s, c: (b * nblk + jnp.maximum(s - npad, 0), kw0 + c)),
                  pl.BlockSpec((1, 1, LANES), lambda b, s, c: (c, 0, 0))],
        out_specs=pl.BlockSpec((1, 1, ts, LANES), lambda b, s, c: (b, c, s, 0)),
        out_shape=jax.ShapeDtypeStruct((bsz, 4, seq + WINDOW, LANES), BF16),
        compiler_params=_cp("arbitrary", "arbitrary", "arbitrary"),
        name="nsa_prep_win",
    )(z, gains_w)
    return pa, pw


def _compress_kernel(x_ref, pos_ref, w1_ref, w2_ref, g_ref, o_ref):
    half = x_ref.shape[-1]
    x = x_ref[0, 0, 0]
    lo = jnp.dot((x + pos_ref[0, :, 0:half]).astype(BF16), w1_ref[0, 0:half, :], preferred_element_type=F32)
    hi = jnp.dot((x + pos_ref[0, :, half:2 * half]).astype(BF16), w1_ref[0, half:2 * half, :],
                 preferred_element_type=F32)
    ncp = x.shape[0]
    pre = lo + pltpu.roll(hi, ncp - 1, 0)
    y = jnp.dot(jax.nn.gelu(pre).astype(BF16), w2_ref[0], preferred_element_type=F32)
    yn = y * lax.rsqrt(jnp.mean(y * y, axis=-1, keepdims=True) + EPS) * g_ref[...]
    o_ref[0, 0, 0] = jnp.where(pl.program_id(0) == 0, yn, y).astype(o_ref.dtype)


def nsa_compress(z, bsz, seq, cmp_pos, w1k, w2k, w1v, w2v, knorm_g):
    ncp = seq // CMP_STRIDE
    kv = z[:, Z_KV:Z_KV + 512].reshape(bsz, seq, 2, NSA_G, NSA_DH)
    kv = kv.transpose(2, 0, 3, 1, 4).reshape(2, bsz, NSA_G, ncp, CMP_STRIDE * NSA_DH)
    pos = cmp_pos.reshape(1, 1, CMP_LEN * NSA_DH)
    w1 = jnp.stack([w1k, w1v]).astype(BF16)
    w2 = jnp.stack([w2k, w2v]).astype(BF16)
    kdim = CMP_STRIDE * NSA_DH
    return pl.pallas_call(
        _compress_kernel,
        grid=(2, bsz, NSA_G),
        in_specs=[pl.BlockSpec((1, 1, 1, ncp, kdim), lambda t, b, g: (t, b, g, 0, 0)),
                  pl.BlockSpec((1, 1, 2 * kdim), lambda t, b, g: (0, 0, 0)),
                  pl.BlockSpec((1, 2 * kdim, NSA_DH), lambda t, b, g: (t, 0, 0)),
                  pl.BlockSpec((1, NSA_DH, NSA_DH), lambda t, b, g: (t, 0, 0)),
                  pl.BlockSpec((1, NSA_DH), lambda t, b, g: (0, 0))],
        out_specs=pl.BlockSpec((1, 1, 1, ncp, NSA_DH), lambda t, b, g: (t, b, g, 0, 0)),
        out_shape=jax.ShapeDtypeStruct((2, bsz, NSA_G, ncp, NSA_DH), BF16),
        compiler_params=_cp("arbitrary", "arbitrary", "arbitrary"),
        name="nsa_compress",
    )(kv, pos, w1, w2, knorm_g.reshape(1, NSA_DH))


def _dist_tile(bt_ref, h, rho):
    return bt_ref[0, h, jnp.clip(rho, 0, N_BIAS_TILES - 1)]


def _nsa_kernel(q_ref, kc_ref, vc_ref, rel_ref, ovt_ref, ks_ref, vs_ref, kw_ref, vw_ref, bt_ref, gt_ref,
                o_ref, sc_ref, *state):
    m_refs, l_refs, acc_refs = state[0:NSA_HPG], state[NSA_HPG:2 * NSA_HPG], state[2 * NSA_HPG:3 * NSA_HPG]
    j = pl.program_id(2)
    QB, HPG, DH = Q_BLOCK, NSA_HPG, NSA_DH
    R = HPG * QB
    ncp = kc_ref.shape[3]
    nsel = ovt_ref.shape[0]
    q = q_ref[0].reshape(R, DH)
    nt = (((1,), (1,)), ((), ()))
    qi = lax.broadcasted_iota(jnp.int32, (QB, 1), 0)
    tpos = j * QB + qi

    cidx = lax.broadcasted_iota(jnp.int32, (QB, ncp), 1)
    mask_c = jnp.logical_and(cidx * CMP_STRIDE + (CMP_LEN - 1) <= tpos, cidx < ncp - 1)
    band0 = (j * (QB // CMP_STRIDE) + ncp - (LANES - QB // CMP_STRIDE)) % ncp
    s_c = lax.dot_general(q, kc_ref[0, 0, 0], nt, preferred_element_type=F32)
    p_c = []
    for h in range(HPG):
        sat = bt_ref[0, h, N_BIAS_TILES - 1]
        bias = jnp.concatenate([rel_ref[0, h]] + [sat] * (ncp // LANES - 1), axis=1)
        bias = pltpu.roll(bias, band0, 1)
        sh = jnp.where(mask_c, s_c[h * QB:(h + 1) * QB, :] + bias, NEG_INF)
        mx = jnp.max(sh, axis=-1, keepdims=True)
        e = jnp.where(mask_c, jnp.exp(sh - mx), 0.0)
        den = jnp.sum(e, axis=-1, keepdims=True)
        p_c.append(e * jnp.where(den > 0.0, 1.0 / den, 0.0))
    o_c = jnp.dot(jnp.concatenate(p_c, axis=0).astype(BF16), vc_ref[0, 0, 0], preferred_element_type=F32)

    psum = p_c[0] + p_c[1] + p_c[2] + p_c[3]
    p_hi = psum.astype(BF16)
    p_lo = (psum - p_hi.astype(F32)).astype(BF16)
    imp = (lax.dot_general(ovt_ref[...], p_hi, nt, preferred_element_type=F32)
           + lax.dot_general(ovt_ref[...], p_lo, nt, preferred_element_type=F32))
    nidx = lax.broadcasted_iota(jnp.int32, (nsel, QB), 0)
    tpos_l = j * QB + lax.broadcasted_iota(jnp.int32, (1, QB), 1)
    blk_t = tpos_l // SLC_LEN
    valid = nidx * SLC_LEN <= tpos_l
    forced = (nidx == 0) | (nidx == blk_t) | (nidx == blk_t - 1)
    score = jnp.where(valid, imp + jnp.where(forced, FORCE_SELECT, 0.0), NEG_INF)
    nidx_f = nidx.astype(F32)
    sel_t = jnp.full((nsel, QB), NEG_INF, F32)
    for _ in range(min(NSA_N_SELECT, nsel)):
        best = jnp.max(score, axis=0, keepdims=True)
        first = jnp.min(jnp.where(score == best, nidx_f, float(nsel)), axis=0, keepdims=True)
        hit = nidx_f == first
        sel_t = jnp.where(hit, 0.0, sel_t)
        score = jnp.where(hit, -jnp.inf, score)
    sel_neg = sel_t.T.astype(BF16)

    for h in range(HPG):
        m_refs[h][...] = jnp.full((QB, LANES), -jnp.inf, F32)
        l_refs[h][...] = jnp.zeros((QB, LANES), F32)
        acc_refs[h][...] = jnp.zeros((QB, DH), F32)
    KT = SEL_TILE
    sub = KT // QB
    bpt = KT // SLC_LEN
    kk_i = lax.broadcasted_iota(jnp.int32, (QB, KT), 1)
    qq_i = lax.broadcasted_iota(jnp.int32, (QB, KT), 0)
    e_row = lax.broadcasted_iota(jnp.int32, (nsel, KT), 0)
    e_col = lax.broadcasted_iota(jnp.int32, (nsel, KT), 1) // SLC_LEN

    def lane_fold(x, op):
        parts = [x[:, u * LANES:(u + 1) * LANES] for u in range(KT // LANES)]
        while len(parts) > 1:
            parts = [op(parts[a], parts[a + 1]) for a in range(0, len(parts), 2)]
        return parts[0]

    def score_step(t, carry):
        k0 = pl.multiple_of(t * KT, KT)
        k_t = ks_ref[0, 0, pl.ds(k0, KT), :]
        expand = jnp.where(e_row == e_col + t * bpt, 1.0, 0.0).astype(BF16)
        madd = jnp.dot(sel_neg, expand, preferred_element_type=F32)
        madd = jnp.where(j * QB + qq_i - (k0 + kk_i) >= 0, madd, NEG_INF)
        s = lax.dot_general(q, k_t, nt, preferred_element_type=F32)
        for h in range(HPG):
            bias = jnp.concatenate([_dist_tile(bt_ref, h, j - t * sub - u) for u in range(sub)], axis=1)
            sh = s[h * QB:(h + 1) * QB, :] + bias + madd
            sc_ref[t, h * QB:(h + 1) * QB, :] = sh
            m_refs[h][...] = jnp.maximum(m_refs[h][...], lane_fold(sh, jnp.maximum))
        return carry

    def value_step(t, carry):
        k0 = pl.multiple_of(t * KT, KT)
        v_t = vs_ref[0, 0, pl.ds(k0, KT), :]
        for h in range(HPG):
            mb = m_refs[h][...]
            p = jnp.exp(sc_ref[t, h * QB:(h + 1) * QB, :] - jnp.concatenate([mb] * (KT // LANES), axis=1))
            l_refs[h][...] += lane_fold(p, jnp.add)
            acc_refs[h][...] += jnp.dot(p.astype(BF16), v_t, preferred_element_type=F32)
        return carry

    n_tiles = (j * QB) // KT + 1
    lax.fori_loop(0, n_tiles, score_step, 0)
    for h in range(HPG):
        m_refs[h][...] = jnp.broadcast_to(jnp.max(m_refs[h][...], axis=-1, keepdims=True), (QB, LANES))
    lax.fori_loop(0, n_tiles, value_step, 0)
    o_s = [acc_refs[h][...] / jnp.sum(l_refs[h][...], axis=-1, keepdims=True) for h in range(HPG)]

    WK = WINDOW + QB
    w0 = pl.multiple_of(j * QB, QB)
    k_w = kw_ref[0, 0, pl.ds(w0, WK), :]
    v_w = vw_ref[0, 0, pl.ds(w0, WK), :]
    wk_i = lax.broadcasted_iota(jnp.int32, (QB, WK), 1)
    wq_i = lax.broadcasted_iota(jnp.int32, (QB, WK), 0)
    dist_w = wq_i - wk_i + WINDOW
    mask_w = (dist_w >= 0) & (dist_w < WINDOW) & (j * QB - WINDOW + wk_i >= 0)
    madd_w = jnp.where(mask_w, 0.0, NEG_INF)
    s_w = lax.dot_general(q, k_w, nt, preferred_element_type=F32)
    nwt = WK // QB
    o_w = []
    for h in range(HPG):
        bias = jnp.concatenate([_dist_tile(bt_ref, h, nwt - 1 - u) for u in range(nwt)], axis=1)
        sh = s_w[h * QB:(h + 1) * QB, :] + bias + madd_w
        mxw = jnp.max(sh, axis=-1, keepdims=True)
        pw = jnp.exp(sh - mxw)
        lw = jnp.sum(pw, axis=-1, keepdims=True)
        o_w.append(jnp.dot(pw.astype(BF16), v_w, preferred_element_type=F32) / lw)

    gate = jax.nn.sigmoid(gt_ref[...])
    for h in range(HPG):
        gc = gate[:, 8 + 3 * h:9 + 3 * h]
        gs = gate[:, 9 + 3 * h:10 + 3 * h]
        gw = gate[:, 10 + 3 * h:11 + 3 * h]
        y = gc * o_c[h * QB:(h + 1) * QB, :] + gs * o_s[h] + gw * o_w[h]
        o_ref[:, h * DH:(h + 1) * DH] = y.astype(o_ref.dtype)


def nsa_tables(rel_bias, seq):
    def bias_of(dist):
        dist = jnp.maximum(dist, 0)
        max_exact = NUM_BUCKETS // 2
        log_ratio = jnp.log(jnp.maximum(dist, 1).astype(F32) / max_exact) / math.log(MAX_DISTANCE / max_exact)
        large = jnp.minimum(max_exact + (log_ratio * (NUM_BUCKETS - max_exact)).astype(jnp.int32), NUM_BUCKETS - 1)
        bucket = jnp.where(dist < max_exact, dist, large)
        onehot = (bucket[..., None] == jnp.arange(NUM_BUCKETS)).astype(F32)
        out = jnp.dot(onehot, rel_bias.astype(F32), precision=lax.Precision.HIGHEST)
        return jnp.moveaxis(out, -1, 0)

    assert seq // CMP_STRIDE >= LANES and Q_BLOCK * (N_BIAS_TILES - 1) - (Q_BLOCK - 1) >= MAX_DISTANCE
    i = jnp.arange(Q_BLOCK)
    dist = jnp.arange(N_BIAS_TILES)[:, None, None] * Q_BLOCK + i[None, :, None] - i[None, None, :]
    bt = bias_of(dist).reshape(NSA_G, NSA_HPG, N_BIAS_TILES, Q_BLOCK, Q_BLOCK)
    per_tile = Q_BLOCK // CMP_STRIDE
    assert (LANES - per_tile) * CMP_STRIDE - (CMP_LEN - 1) >= MAX_DISTANCE
    dist_c = i[:, None] - ((jnp.arange(LANES)[None, :] - (LANES - per_tile)) * CMP_STRIDE + CMP_LEN - 1)
    rel = bias_of(dist_c).reshape(NSA_G, NSA_HPG, Q_BLOCK, LANES)
    return bt, rel


def nsa_overlap(seq):
    ncp = seq // CMP_STRIDE
    ns = seq // SLC_LEN
    cstart = np.arange(ncp) * CMP_STRIDE
    sstart = np.arange(ns) * SLC_LEN
    ov = np.clip(np.minimum(cstart[:, None] + CMP_LEN, sstart[None, :] + SLC_LEN)
                 - np.maximum(cstart[:, None], sstart[None, :]), 0, None).astype(np.float32) / CMP_LEN
    ov[ncp - 1] = 0.0
    return jnp.asarray(ov.T, BF16)


def nsa_attend(z, pa, pw, cmp, bt, rel, ovt, bsz, seq):
    nqb = seq // Q_BLOCK
    ncp = seq // CMP_STRIDE
    nsel = seq // SLC_LEN
    R = NSA_HPG * Q_BLOCK
    return pl.pallas_call(
        _nsa_kernel,
        grid=(bsz, NSA_G, nqb),
        in_specs=[pl.BlockSpec((1, NSA_HPG, Q_BLOCK, NSA_DH), lambda b, g, j: (b, g, j, 0)),
                  pl.BlockSpec((1, 1, 1, ncp, NSA_DH), lambda b, g, j: (0, b, g, 0, 0)),
                  pl.BlockSpec((1, 1, 1, ncp, NSA_DH), lambda b, g, j: (1, b, g, 0, 0)),
                  pl.BlockSpec((1, NSA_HPG, Q_BLOCK, LANES), lambda b, g, j: (g, 0, 0, 0)),
                  pl.BlockSpec((nsel, ncp), lambda b, g, j: (0, 0)),
                  pl.BlockSpec((1, 1, seq, NSA_DH), lambda b, g, j: (b, 8 + g, 0, 0)),
                  pl.BlockSpec((1, 1, seq, NSA_DH), lambda b, g, j: (b, 10 + g, 0, 0)),
                  pl.BlockSpec((1, 1, seq + WINDOW, NSA_DH), lambda b, g, j: (b, g, 0, 0)),
                  pl.BlockSpec((1, 1, seq + WINDOW, NSA_DH), lambda b, g, j: (b, 2 + g, 0, 0)),
                  pl.BlockSpec((1, NSA_HPG, N_BIAS_TILES, Q_BLOCK, Q_BLOCK), lambda b, g, j: (g, 0, 0, 0, 0)),
                  pl.BlockSpec((Q_BLOCK, LANES), lambda b, g, j: (b * nqb + j, Z_SMALL // LANES + g))],
        out_specs=pl.BlockSpec((Q_BLOCK, NSA_HPG * NSA_DH), lambda b, g, j: (b * nqb + j, g)),
        out_shape=jax.ShapeDtypeStruct((bsz * seq, BRANCH_WIDTH), BF16),
        scratch_shapes=([pltpu.VMEM((seq // SEL_TILE, R, SEL_TILE), F32)]
                        + [pltpu.VMEM((Q_BLOCK, LANES), F32)] * (2 * NSA_HPG)
                        + [pltpu.VMEM((Q_BLOCK, NSA_DH), F32)] * NSA_HPG),
        compiler_params=_cp("arbitrary", "arbitrary", "arbitrary"),
        name="nsa_attend",
    )(pa, cmp, cmp, rel, ovt, pa, pa, pw, pw, bt, z)


def _merge_kernel(ya_ref, yb_ref, yc_ref, w_ref, ga_ref, gb_ref, gc_ref, o_ref):
    acc = jax.nn.sigmoid(ga_ref[...]) * jnp.dot(ya_ref[...], w_ref[0], preferred_element_type=F32)
    acc = acc + jax.nn.sigmoid(gb_ref[...]) * jnp.dot(yb_ref[...], w_ref[1], preferred_element_type=F32)
    acc = acc + jax.nn.sigmoid(gc_ref[...]) * jnp.dot(yc_ref[...], w_ref[2], preferred_element_type=F32)
    o_ref[...] = acc.astype(o_ref.dtype)


def merge(ya, yb, yc, w_branch, z, tm=1024, tn=512):
    n = ya.shape[0]
    d = w_branch.shape[2]
    bw = ya.shape[1]
    ymap = lambda i, j: (i, 0)
    gmap = lambda k: (lambda i, j: (i, (Z_GATE + k * d) // tn + j))
    return pl.pallas_call(
        _merge_kernel,
        grid=(n // tm, d // tn),
        in_specs=[pl.BlockSpec((tm, bw), ymap), pl.BlockSpec((tm, bw), ymap), pl.BlockSpec((tm, bw), ymap),
                  pl.BlockSpec((3, bw, tn), lambda i, j: (0, 0, j)),
                  pl.BlockSpec((tm, tn), gmap(0)), pl.BlockSpec((tm, tn), gmap(1)), pl.BlockSpec((tm, tn), gmap(2))],
        out_specs=pl.BlockSpec((tm, tn), lambda i, j: (i, j)),
        out_shape=jax.ShapeDtypeStruct((n, d), BF16),
        compiler_params=_cp("arbitrary", "arbitrary"),
        name="merge",
    )(ya, yb, yc, w_branch, z, z, z)


def _outproj_kernel(a_ref, w_ref, x_ref, g_ref, o_ref):
    o_ref[...] = x_ref[...] + g_ref[0] * jnp.dot(a_ref[...], w_ref[...], preferred_element_type=F32)


def out_proj(a, w, x2d, gate, seq, tm=1024, tn=512):
    n, k = a.shape
    d = w.shape[1]
    per_b = seq // tm
    return pl.pallas_call(
        _outproj_kernel,
        grid=(n // tm, d // tn),
        in_specs=[pl.BlockSpec((tm, k), lambda i, j: (i, 0)),
                  pl.BlockSpec((k, tn), lambda i, j: (0, j)),
                  pl.BlockSpec((tm, tn), lambda i, j: (i, j)),
                  pl.BlockSpec((1, 1, tn), lambda i, j: (i // per_b, 0, j))],
        out_specs=pl.BlockSpec((tm, tn), lambda i, j: (i, j)),
        out_shape=jax.ShapeDtypeStruct((n, d), F32),
        compiler_params=_cp("arbitrary", "arbitrary"),
        name="out_proj",
    )(a, w, x2d, gate[:, None, :])


def _router_kernel(x_ref, g_ref, sc_ref, sh_ref, rw_ref, rwl_ref, rb_ref, h_ref, idx_ref, p_ref):
    x = x_ref[...]
    y = x * lax.rsqrt(jnp.mean(x * x, axis=-1, keepdims=True) + EPS)
    hf = (y * g_ref[...]) * (1.0 + sc_ref[0]) + sh_ref[0]
    h = hf.astype(BF16)
    h_ref[...] = h
    h_lo = (hf - h.astype(F32)).astype(BF16)
    logits = (jnp.dot(h, rw_ref[...], preferred_element_type=F32)
              + jnp.dot(h_lo, rw_ref[...], preferred_element_type=F32)
              + jnp.dot(h, rwl_ref[...], preferred_element_type=F32)) + rb_ref[...]
    lane = lax.broadcasted_iota(jnp.int32, logits.shape, 1)
    lane_f = lane.astype(F32)
    idx_out = jnp.zeros(logits.shape, F32)
    val_out = jnp.full(logits.shape, -jnp.inf, F32)
    for k in range(TOP_K):
        best = jnp.max(logits, axis=-1, keepdims=True)
        first = jnp.min(jnp.where(logits == best, lane_f, float(LANES)), axis=-1, keepdims=True)
        idx_out = jnp.where(lane == k, first, idx_out)
        val_out = jnp.where(lane == k, best, val_out)
        logits = jnp.where(lane_f == first, -jnp.inf, logits)
    e = jnp.exp(val_out - jnp.max(val_out, axis=-1, keepdims=True))
    idx_ref[...] = idx_out.astype(jnp.int32)
    p_ref[...] = e / jnp.sum(e, axis=-1, keepdims=True)


def router(x2d, g, sc, sh, rw, rb, seq, tm=1024):
    n, d = x2d.shape
    per_b = seq // tm
    rw_hi = rw.astype(BF16)
    rw_lo = (rw - rw_hi.astype(F32)).astype(BF16)
    rwp = jnp.zeros((d, LANES), BF16).at[:, :N_EXPERTS].set(rw_hi)
    rwl = jnp.zeros((d, LANES), BF16).at[:, :N_EXPERTS].set(rw_lo)
    rbp = jnp.full((1, LANES), -jnp.inf, F32).at[0, :N_EXPERTS].set(rb)
    return pl.pallas_call(
        _router_kernel,
        grid=(n // tm,),
        in_specs=[pl.BlockSpec((tm, d), lambda i: (i, 0)),
                  pl.BlockSpec((1, d), lambda i: (0, 0)),
                  pl.BlockSpec((1, 1, d), lambda i: (i // per_b, 0, 0)),
                  pl.BlockSpec((1, 1, d), lambda i: (i // per_b, 0, 0)),
                  pl.BlockSpec((d, LANES), lambda i: (0, 0)),
                  pl.BlockSpec((d, LANES), lambda i: (0, 0)),
                  pl.BlockSpec((1, LANES), lambda i: (0, 0))],
        out_specs=[pl.BlockSpec((tm, d), lambda i: (i, 0)),
                   pl.BlockSpec((tm, LANES), lambda i: (i, 0)),
                   pl.BlockSpec((tm, LANES), lambda i: (i, 0))],
        out_shape=[jax.ShapeDtypeStruct((n, d), BF16),
                   jax.ShapeDtypeStruct((n, LANES), jnp.int32),
                   jax.ShapeDtypeStruct((n, LANES), F32)],
        compiler_params=_cp("arbitrary"),
        name="router",
    )(x2d, g.reshape(1, d), sc[:, None, :], sh[:, None, :], rwp, rwl, rbp)


def _expert_kernel(be_ref, nv_ref, x_ref, wg_ref, wu_ref, bg_ref, bu_ref, w2_ref, b2_ref, o_ref, acc_ref):
    i = pl.program_id(0)
    f = pl.program_id(1)

    @pl.when(i < nv_ref[0])
    def _():
        @pl.when(f == 0)
        def _():
            acc_ref[...] = jnp.zeros(acc_ref.shape, F32)

        x = x_ref[...]
        gate = jnp.dot(x, wg_ref[0, 0].astype(BF16), preferred_element_type=F32) + bg_ref[0, 0]
        up = jnp.dot(x, wu_ref[0, 0].astype(BF16), preferred_element_type=F32) + bu_ref[0, 0]
        gate = jnp.minimum(gate, SWIGLU_LIMIT)
        up = jnp.clip(up, -SWIGLU_LIMIT, SWIGLU_LIMIT)
        act = (up + 1.0) * (gate * jax.nn.sigmoid(SWIGLU_ALPHA * gate))
        acc_ref[...] += jnp.dot(act.astype(BF16), w2_ref[0, 0].astype(BF16), preferred_element_type=F32)

        @pl.when(f == pl.num_programs(1) - 1)
        def _():
            o_ref[...] = (acc_ref[...] + b2_ref[0, 0]).astype(o_ref.dtype)

    @pl.when(i >= nv_ref[0])
    def _():
        o_ref[...] = jnp.zeros(o_ref.shape, o_ref.dtype)


def experts(xb, block_exp, n_valid, w1, b1, w2, b2, layer, tm, tf=256):
    rows, d = xb.shape
    nf = D_FF // tf
    nblk = rows // tm
    n_exp = w1.shape[1]
    fe = lambda i, f, nv: jnp.where(i < nv[0], f, nf - 1)
    grid_spec = pltpu.PrefetchScalarGridSpec(
        num_scalar_prefetch=2,
        grid=(nblk, nf),
        in_specs=[pl.BlockSpec((tm, d), lambda i, f, be, nv: (jnp.minimum(i, nv[0] - 1), 0)),
                  pl.BlockSpec((1, 1, d, tf), lambda i, f, be, nv: (layer, be[i], 0, fe(i, f, nv))),
                  pl.BlockSpec((1, 1, d, tf), lambda i, f, be, nv: (layer, be[i], 0, nf + fe(i, f, nv))),
                  pl.BlockSpec((1, 1, 1, tf), lambda i, f, be, nv: (layer * n_exp + be[i], 0, 0, fe(i, f, nv))),
                  pl.BlockSpec((1, 1, 1, tf), lambda i, f, be, nv: (layer * n_exp + be[i], 0, 0, nf + fe(i, f, nv))),
                  pl.BlockSpec((1, 1, tf, d), lambda i, f, be, nv: (layer, be[i], fe(i, f, nv), 0)),
                  pl.BlockSpec((1, 1, 1, d), lambda i, f, be, nv: (layer * n_exp + be[i], 0, 0, 0))],
        out_specs=pl.BlockSpec((tm, d), lambda i, f, be, nv: (i, 0)),
        scratch_shapes=[pltpu.VMEM((tm, d), F32)],
    )
    b1r = b1.reshape(-1, 1, 1, b1.shape[-1])
    b2r = b2.reshape(-1, 1, 1, d)
    return pl.pallas_call(
        _expert_kernel,
        grid_spec=grid_spec,
        out_shape=jax.ShapeDtypeStruct((rows, d), BF16),
        compiler_params=_cp("arbitrary", "arbitrary"),
        name="experts",
    )(block_exp, n_valid, xb, w1, w1, b1r, b1r, w2, b2r)


def _combine_kernel(y0_ref, y1_ref, y2_ref, y3_ref, p_ref, x_ref, g_ref, o_ref):
    p = p_ref[...]
    y = p[:, 0:1] * y0_ref[...].astype(F32)
    for k, y_ref in enumerate((y1_ref, y2_ref, y3_ref), start=1):
        y = y + p[:, k:k + 1] * y_ref[...].astype(F32)
    o_ref[...] = x_ref[...] + g_ref[0] * y


def combine(yg, prob, x2d, gate, seq, tm=512):
    n, d = x2d.shape
    per_b = seq // tm
    nb = n // tm
    ymap = lambda k: (lambda i: (k * nb + i, 0))
    return pl.pallas_call(
        _combine_kernel,
        grid=(nb,),
        in_specs=[pl.BlockSpec((tm, d), ymap(k)) for k in range(TOP_K)] + [
            pl.BlockSpec((tm, LANES), lambda i: (i, 0)),
            pl.BlockSpec((tm, d), lambda i: (i, 0)),
            pl.BlockSpec((1, 1, d), lambda i: (i // per_b, 0, 0))],
        out_specs=pl.BlockSpec((tm, d), lambda i: (i, 0)),
        out_shape=jax.ShapeDtypeStruct((n, d), F32),
        compiler_params=_cp("arbitrary"),
        name="combine",
    )(yg, yg, yg, yg, prob, x2d, gate[:, None, :])


def moe(x2d, g, sc, sh, gate, rw, rb, w1, b1, w2, b2, layer, seq, tm_e=1024):
    n, d = x2d.shape
    h, idx, prob = router(x2d, g, sc, sh, rw, rb, seq)
    e_flat = idx[:, :TOP_K].T.reshape(-1)
    na = n * TOP_K
    onehot = (e_flat[:, None] == jnp.arange(N_EXPERTS)[None, :]).astype(jnp.int32)
    csum = jnp.cumsum(onehot, axis=0)
    counts = csum[-1]
    rank = jnp.take_along_axis(csum, e_flat[:, None], axis=1)[:, 0] - 1
    padded = (counts + tm_e - 1) // tm_e * tm_e
    pend = jnp.cumsum(padded)
    pstart = pend - padded
    dest = pstart[e_flat] + rank
    nblk = na // tm_e + N_EXPERTS
    rows = nblk * tm_e
    buf_tok = jnp.zeros((rows,), jnp.int32).at[dest].set(jnp.arange(na, dtype=jnp.int32) % n)
    starts = jnp.arange(nblk, dtype=jnp.int32) * tm_e
    block_exp = jnp.minimum(jnp.sum(pend[None, :] <= starts[:, None], axis=1), N_EXPERTS - 1).astype(jnp.int32)
    n_valid = (pend[-1:] // tm_e).astype(jnp.int32)
    block_exp = jnp.where(starts < pend[-1], block_exp, block_exp[jnp.maximum(n_valid[0] - 1, 0)])
    xb = h[buf_tok]
    yb = experts(xb, block_exp, n_valid, w1, b1, w2, b2, layer, tm_e)
    yg = yb[dest]
    return combine(yg, prob, x2d, gate, seq)


def fused_in_weight(w_in, w_gate):
    sizes = (512, 512, 1024, 4, 4, 1024, 1024, 256, 256, 256, 256, 256, 256, 24, 1024, 1024)
    pts = np.cumsum(sizes)[:-1].tolist()
    (qm, km, vm, ig, fg, og, qn, kc, vc, ks, vs, kw, vw, gn, gu, gv) = jnp.split(w_in, pts, axis=1)
    d = w_in.shape[0]
    pad = lambda k: jnp.zeros((d, k), w_in.dtype)
    per_group = NSA_HPG * 3
    small0 = jnp.concatenate([ig, fg, gn[:, :per_group], pad(LANES - 8 - per_group)], axis=1)
    small1 = jnp.concatenate([pad(8), gn[:, per_group:], pad(LANES - 8 - per_group)], axis=1)
    cols = [qm, km, vm, og, qn, gu, gv, kc, vc, ks, vs, kw, vw, w_gate[0], w_gate[1], w_gate[2], small0, small1]
    w = jnp.concatenate(cols, axis=1)
    w = jnp.concatenate([w, pad(Z_WIDTH - w.shape[1])], axis=1)
    return w.astype(BF16)


def kernel(x, c, ada_w, ada_b, norm1_g, norm2_g, w_in, conv_w, conv_b, mlstm_gate_b, mlstm_norm_g, cmp_pos,
           cmp_k_w1, cmp_k_w2, cmp_v_w1, cmp_v_w2, qnorm_g, knorm_g, rel_bias, gmlp_norm_g, gmlp_ws, gmlp_b,
           w_branch, w_gate, w_out, router_w, router_b, exp_w1, exp_b1, exp_w2, exp_b2):
    bsz, seq, d = x.shape
    depth = ada_w.shape[0]
    x2d = x.reshape(bsz * seq, d)
    bt, rel = nsa_tables(rel_bias, seq)
    ovt = nsa_overlap(seq)
    for l in range(depth):
        mod = ada_mod(c, ada_w, ada_b, l)
        sh1, sc1, g1, sh2, sc2, g2 = jnp.split(mod, 6, axis=-1)
        z = in_proj(x2d, norm1_g[l], sc1, sh1, fused_in_weight(w_in[l], w_gate[l]), seq)
        ya = mlstm(z, bsz, seq, conv_w[l], conv_b[l], mlstm_gate_b[l], mlstm_norm_g[l])
        pa, pw = nsa_prep(z, bsz, seq, qnorm_g[l], knorm_g[l])
        cmp = nsa_compress(z, bsz, seq, cmp_pos[l], cmp_k_w1[l], cmp_k_w2[l], cmp_v_w1[l], cmp_v_w2[l], knorm_g[l])
        yb = nsa_attend(z, pa, pw, cmp, bt, rel, ovt, bsz, seq)
        yc = gmlp(z, gmlp_norm_g[l], gmlp_ws[l], gmlp_b[l])
        merged = merge(ya, yb, yc, w_branch[l].astype(BF16), z)
        x2d = out_proj(merged, w_out[l].astype(BF16), x2d, g1, seq)
        x2d = moe(x2d, norm2_g[l], sc2, sh2, g2, router_w[l], router_b[l], exp_w1, exp_b1, exp_w2, exp_b2, l, seq)
    return x2d.reshape(bsz, seq, d)
```

```python
import functools
import math

import jax
import jax.numpy as jnp
import numpy as np
from jax import lax
from jax.experimental import pallas as pl
from jax.experimental.pallas import tpu as pltpu

F32 = jnp.float32
BF16 = jnp.bfloat16

LANES = 128
VMEM_LIMIT = 56 * 1024 * 1024

D_MODEL = 2048
BRANCH_WIDTH = 1024
MLSTM_HEADS = 4
MLSTM_DV = 256
MLSTM_DQK = 128
MLSTM_TILE = 256
CONV_WIDTH = 4
NSA_HEADS = 8
NSA_G = 2
NSA_HPG = 4
NSA_DH = 128
CMP_LEN = 32
CMP_STRIDE = 16
SLC_LEN = 64
NSA_N_SELECT = 16
WINDOW = 512
Q_BLOCK = 128
NUM_BUCKETS = 32
MAX_DISTANCE = 1024
GMLP_CHUNK = 128
GMLP_GROUPS = 8
N_EXPERTS = 32
TOP_K = 4
D_FF = 1536
SWIGLU_LIMIT = 7.0
SWIGLU_ALPHA = 1.702
NEG_INF = -1e30
FORCE_SELECT = 1e4
EPS = 1e-6

SEL_TILE = 512
N_BIAS_TILES = 10

Z_QK, Z_V, Z_OG, Z_NQ, Z_GU, Z_GV = 0, 1024, 2048, 3072, 4096, 5120
Z_KV = 6144
Z_GATE = 7680
Z_SMALL = 13824
Z_WIDTH = 14336


def _cp(*sem):
    return pltpu.CompilerParams(dimension_semantics=sem, vmem_limit_bytes=VMEM_LIMIT)


def _ada_kernel(c_ref, w_ref, b_ref, o_ref):
    a = jax.nn.silu(c_ref[...]).astype(BF16)
    o_ref[...] = jnp.dot(a, w_ref[0].astype(BF16), preferred_element_type=F32) + b_ref[0]


def ada_mod(c, w, b, layer):
    bsz, d = c.shape
    n = w.shape[2]
    tn = 1024
    cp = jnp.zeros((8, d), F32).at[:bsz].set(c)
    out = pl.pallas_call(
        _ada_kernel,
        grid=(n // tn,),
        in_specs=[pl.BlockSpec((8, d), lambda j: (0, 0)),
                  pl.BlockSpec((1, d, tn), lambda j: (layer, 0, j)),
                  pl.BlockSpec((1, 1, tn), lambda j: (layer, 0, j))],
        out_specs=pl.BlockSpec((8, tn), lambda j: (0, j)),
        out_shape=jax.ShapeDtypeStruct((8, n), F32),
        compiler_params=_cp("arbitrary"),
        name="ada_mod",
    )(cp, w, b[:, None, :])
    return out[:bsz]


def _in_kernel(x_ref, g_ref, sc_ref, sh_ref, w_ref, o_ref, h_ref):
    @pl.when(pl.program_id(1) == 0)
    def _():
        x = x_ref[...]
        y = x * lax.rsqrt(jnp.mean(x * x, axis=-1, keepdims=True) + EPS)
        h = (y * g_ref[...]) * (1.0 + sc_ref[0]) + sh_ref[0]
        h_ref[...] = h.astype(BF16)

    o_ref[...] = jnp.dot(h_ref[...], w_ref[...], preferred_element_type=F32)


def in_proj(x2d, g, sc, sh, wcat, seq, tm=1024, tn=512):
    n, d = x2d.shape
    w = wcat.shape[1]
    per_b = seq // tm
    return pl.pallas_call(
        _in_kernel,
        grid=(n // tm, w // tn),
        in_specs=[pl.BlockSpec((tm, d), lambda i, j: (i, 0)),
                  pl.BlockSpec((1, d), lambda i, j: (0, 0)),
                  pl.BlockSpec((1, 1, d), lambda i, j: (i // per_b, 0, 0)),
                  pl.BlockSpec((1, 1, d), lambda i, j: (i // per_b, 0, 0)),
                  pl.BlockSpec((d, tn), lambda i, j: (0, j))],
        out_specs=pl.BlockSpec((tm, tn), lambda i, j: (i, j)),
        out_shape=jax.ShapeDtypeStruct((n, w), F32),
        scratch_shapes=[pltpu.VMEM((tm, d), BF16)],
        compiler_params=_cp("arbitrary", "arbitrary"),
        name="in_proj",
    )(x2d, g.reshape(1, d), sc[:, None, :], sh[:, None, :], wcat)


def _cumsum_rows(x, n):
    row = lax.broadcasted_iota(jnp.int32, x.shape, 0)
    s = 1
    while s < n:
        x = x + jnp.where(row >= s, pltpu.roll(x, s, 0), 0.0)
        s *= 2
    return x


def _mlstm_kernel(qk_ref, v_ref, og_ref, sm_ref, cw_ref, cb_ref, gb_ref, ng_ref, o_ref,
                  buf_ref, ct_ref, n_ref, m_ref):
    L = qk_ref.shape[0]
    H, DQK, DV = MLSTM_HEADS, MLSTM_DQK, MLSTM_DV

    @pl.when(pl.program_id(1) == 0)
    def _():
        buf_ref[0:8, :] = jnp.zeros((8, buf_ref.shape[1]), F32)
        ct_ref[...] = jnp.zeros(ct_ref.shape, F32)
        n_ref[...] = jnp.zeros(n_ref.shape, F32)
        m_ref[...] = jnp.full(m_ref.shape, NEG_INF, F32)

    buf_ref[8:8 + L, :] = qk_ref[...]
    conv = cb_ref[...] + cw_ref[3:4, :] * buf_ref[8:8 + L, :]
    for j in range(CONV_WIDTH - 1):
        conv = conv + cw_ref[j:j + 1, :] * buf_ref[5 + j:5 + j + L, :]
    qk = jax.nn.silu(conv)
    buf_ref[0:8, :] = buf_ref[L:L + 8, :]

    sm = sm_ref[...] + gb_ref[...]
    bcum = _cumsum_rows(jax.nn.log_sigmoid(sm), L)
    sm_t = sm.T
    bcum_t = bcum.T
    row = lax.broadcasted_iota(jnp.int32, (L, L), 0)
    col = lax.broadcasted_iota(jnp.int32, (L, L), 1)
    causal = row >= col

    for h in range(H):
        qf = qk[:, h * DQK:(h + 1) * DQK] * DQK ** -0.5
        q = qf.astype(BF16)
        kf = qk[:, (H + h) * DQK:(H + h + 1) * DQK]
        k = kf.astype(BF16)
        v = v_ref[:, h * DV:(h + 1) * DV].astype(BF16)
        b_col = bcum[:, H + h:H + h + 1]
        i_col = sm[:, h:h + 1]
        b_row = bcum_t[H + h:H + h + 1, :]
        i_row = sm_t[h:h + 1, :]
        m_prev = m_ref[h:h + 1, 0:1]
        a = b_col + m_prev
        dmat = jnp.where(causal, b_col - b_row + i_row, -jnp.inf)
        m_t = jnp.maximum(a, jnp.max(dmat, axis=1, keepdims=True))
        w_intra = jnp.exp(dmat - m_t)
        w_inter = jnp.exp(a - m_t)
        s = lax.dot_general(q, k, (((1,), (1,)), ((), ())), preferred_element_type=F32) * w_intra
        ct = ct_ref[h]
        nvec = n_ref[h:h + 1, :]
        num = jnp.dot(s.astype(BF16), v, preferred_element_type=F32) + w_inter * jnp.dot(
            q, ct.astype(BF16), preferred_element_type=F32)
        qn = jnp.sum(qf * nvec, axis=1, keepdims=True)
        den = jnp.sum(s, axis=1, keepdims=True) + w_inter * qn
        hm = num / jnp.maximum(jnp.abs(den), jnp.exp(-m_t))
        b_last = b_col[L - 1:L, :]
        gdec = b_last - b_col + i_col
        m_new = jnp.maximum(b_last + m_prev, jnp.max(gdec, axis=0, keepdims=True))
        ws = jnp.exp(gdec - m_new)
        wc = jnp.exp(b_last + m_prev - m_new)
        kw = ws * kf
        ct_ref[h] = wc * ct + jnp.dot(kw.T.astype(BF16), v, preferred_element_type=F32)
        n_ref[h:h + 1, :] = wc * nvec + jnp.sum(kw, axis=0, keepdims=True)
        m_ref[h:h + 1, :] = jnp.broadcast_to(m_new, (1, LANES))
        y = hm * lax.rsqrt(jnp.mean(hm * hm, axis=-1, keepdims=True) + EPS) * ng_ref[:, h * DV:(h + 1) * DV]
        og = og_ref[:, h * DV:(h + 1) * DV]
        o_ref[:, h * DV:(h + 1) * DV] = (jax.nn.sigmoid(og) * y).astype(o_ref.dtype)


def mlstm(z, bsz, seq, conv_w, conv_b, gate_b, norm_g):
    L = MLSTM_TILE
    nch = seq // L
    gb = jnp.zeros((1, LANES), F32).at[0, :2 * MLSTM_HEADS].set(gate_b)
    wide = BRANCH_WIDTH
    rowmap = lambda col: (lambda b, c: (b * nch + c, col))
    const = lambda b, c: (0, 0)
    return pl.pallas_call(
        _mlstm_kernel,
        grid=(bsz, nch),
        in_specs=[pl.BlockSpec((L, wide), rowmap(Z_QK // wide)),
                  pl.BlockSpec((L, wide), rowmap(Z_V // wide)),
                  pl.BlockSpec((L, wide), rowmap(Z_OG // wide)),
                  pl.BlockSpec((L, LANES), rowmap(Z_SMALL // LANES)),
                  pl.BlockSpec((CONV_WIDTH, wide), const),
                  pl.BlockSpec((1, wide), const),
                  pl.BlockSpec((1, LANES), const),
                  pl.BlockSpec((1, wide), const)],
        out_specs=pl.BlockSpec((L, wide), lambda b, c: (b * nch + c, 0)),
        out_shape=jax.ShapeDtypeStruct((bsz * seq, wide), BF16),
        scratch_shapes=[pltpu.VMEM((L + 8, wide), F32),
                        pltpu.VMEM((MLSTM_HEADS, MLSTM_DQK, MLSTM_DV), F32),
                        pltpu.VMEM((8, MLSTM_DQK), F32),
                        pltpu.VMEM((8, LANES), F32)],
        compiler_params=_cp("arbitrary", "arbitrary"),
        name="mlstm",
    )(z, z, z, z, conv_w, conv_b.reshape(1, wide), gb, norm_g.reshape(1, wide))


def _gmlp_kernel(u_ref, v_ref, ng_ref, ws_ref, b_ref, o_ref):
    T = u_ref.shape[0]
    C = GMLP_CHUNK
    u = jax.nn.gelu(u_ref[...])
    vg = jax.nn.gelu(v_ref[...])
    mu = jnp.mean(vg, axis=-1, keepdims=True)
    var = jnp.mean(jnp.square(vg - mu), axis=-1, keepdims=True)
    v = ((vg - mu) * lax.rsqrt(var + EPS) * ng_ref[...]).astype(BF16)
    row = lax.broadcasted_iota(jnp.int32, (C, C), 0)
    col = lax.broadcasted_iota(jnp.int32, (C, C), 1)
    for g in range(GMLP_GROUPS):
        wsc = jnp.where(row >= col, ws_ref[g], 0.0).astype(BF16)
        bcol = b_ref[:, g:g + 1]
        for n in range(T // C):
            vv = v[n * C:(n + 1) * C, g * C:(g + 1) * C]
            s = jnp.dot(wsc, vv, preferred_element_type=F32) + bcol
            o_ref[n * C:(n + 1) * C, g * C:(g + 1) * C] = (
                u[n * C:(n + 1) * C, g * C:(g + 1) * C] * s).astype(o_ref.dtype)


def gmlp(z, norm_g, ws, b, tg=512):
    n = z.shape[0]
    wide = BRANCH_WIDTH
    return pl.pallas_call(
        _gmlp_kernel,
        grid=(n // tg,),
        in_specs=[pl.BlockSpec((tg, wide), lambda i: (i, Z_GU // wide)),
                  pl.BlockSpec((tg, wide), lambda i: (i, Z_GV // wide)),
                  pl.BlockSpec((1, wide), lambda i: (0, 0)),
                  pl.BlockSpec((GMLP_GROUPS, GMLP_CHUNK, GMLP_CHUNK), lambda i: (0, 0, 0)),
                  pl.BlockSpec((GMLP_CHUNK, GMLP_GROUPS), lambda i: (0, 0))],
        out_specs=pl.BlockSpec((tg, wide), lambda i: (i, 0)),
        out_shape=jax.ShapeDtypeStruct((n, wide), BF16),
        compiler_params=_cp("arbitrary"),
        name="gmlp",
    )(z, z, norm_g.reshape(1, wide), ws, b.T)


def _prep_kernel(norm_cols, x_ref, g_ref, o_ref):
    c = pl.program_id(2)
    x = x_ref[...]
    y = x * lax.rsqrt(jnp.mean(x * x, axis=-1, keepdims=True) + EPS) * g_ref[0]
    flag = functools.reduce(jnp.logical_or, [c == nc for nc in norm_cols])
    o_ref[0, 0] = jnp.where(flag, y, x).astype(o_ref.dtype)


def _prep_pad_kernel(norm_cols, npad, x_ref, g_ref, o_ref):
    @pl.when(pl.program_id(1) < npad)
    def _():
        o_ref[...] = jnp.zeros(o_ref.shape, o_ref.dtype)

    @pl.when(pl.program_id(1) >= npad)
    def _():
        _prep_kernel(norm_cols, x_ref, g_ref, o_ref)


def nsa_prep(z, bsz, seq, qnorm_g, knorm_g, ts=512):
    nblk = seq // ts
    gq = (qnorm_g * NSA_DH ** -0.5).reshape(1, 1, LANES)
    gk = knorm_g.reshape(1, 1, LANES)
    gains_a = jnp.concatenate([jnp.tile(gq, (8, 1, 1)), jnp.tile(gk, (4, 1, 1))], axis=0)
    q0, ks0 = Z_NQ // LANES, (Z_KV + 512) // LANES
    pa = pl.pallas_call(
        functools.partial(_prep_kernel, tuple(range(10))),
        grid=(bsz, nblk, 12),
        in_specs=[pl.BlockSpec((ts, LANES), lambda b, s, c: (b * nblk + s, jnp.where(c < 8, q0 + c, ks0 + c - 8))),
                  pl.BlockSpec((1, 1, LANES), lambda b, s, c: (c, 0, 0))],
        out_specs=pl.BlockSpec((1, 1, ts, LANES), lambda b, s, c: (b, c, s, 0)),
        out_shape=jax.ShapeDtypeStruct((bsz, 12, seq, LANES), BF16),
        compiler_params=_cp("arbitrary", "arbitrary", "arbitrary"),
        name="nsa_prep_qks",
    )(z, gains_a)
    gains_w = jnp.tile(gk, (4, 1, 1))
    kw0 = (Z_KV + 1024) // LANES
    npad = WINDOW // ts
    pw = pl.pallas_call(
        functools.partial(_prep_pad_kernel, (0, 1), npad),
        grid=(bsz, nblk + npad, 4),
        in_specs=[pl.BlockSpec((ts, LANES), lambda b, s, c: (b * nblk + jnp.maximum(s - npad, 0), kw0 + c)),
                  pl.BlockSpec((1, 1, LANES), lambda b, s, c: (c, 0, 0))],
        out_specs=pl.BlockSpec((1, 1, ts, LANES), lambda b, s, c: (b, c, s, 0)),
        out_shape=jax.ShapeDtypeStruct((bsz, 4, seq + WINDOW, LANES), BF16),
        compiler_params=_cp("arbitrary", "arbitrary", "arbitrary"),
        name="nsa_prep_win",
    )(z, gains_w)
    return pa, pw


def _compress_kernel(x_ref, pos_ref, w1_ref, w2_ref, g_ref, o_ref):
    half = x_ref.shape[-1]
    x = x_ref[0, 0, 0]
    lo = jnp.dot((x + pos_ref[0, :, 0:half]).astype(BF16), w1_ref[0, 0:half, :], preferred_element_type=F32)
    hi = jnp.dot((x + pos_ref[0, :, half:2 * half]).astype(BF16), w1_ref[0, half:2 * half, :],
                 preferred_element_type=F32)
    ncp = x.shape[0]
    pre = lo + pltpu.roll(hi, ncp - 1, 0)
    y = jnp.dot(jax.nn.gelu(pre).astype(BF16), w2_ref[0], preferred_element_type=F32)
    yn = y * lax.rsqrt(jnp.mean(y * y, axis=-1, keepdims=True) + EPS) * g_ref[...]
    o_ref[0, 0, 0] = jnp.where(pl.program_id(0) == 0, yn, y).astype(o_ref.dtype)


def nsa_compress(z, bsz, seq, cmp_pos, w1k, w2k, w1v, w2v, knorm_g):
    ncp = seq // CMP_STRIDE
    kv = z[:, Z_KV:Z_KV + 512].reshape(bsz, seq, 2, NSA_G, NSA_DH)
    kv = kv.transpose(2, 0, 3, 1, 4).reshape(2, bsz, NSA_G, ncp, CMP_STRIDE * NSA_DH)
    pos = cmp_pos.reshape(1, 1, CMP_LEN * NSA_DH)
    w1 = jnp.stack([w1k, w1v]).astype(BF16)
    w2 = jnp.stack([w2k, w2v]).astype(BF16)
    kdim = CMP_STRIDE * NSA_DH
    return pl.pallas_call(
        _compress_kernel,
        grid=(2, bsz, NSA_G),
        in_specs=[pl.BlockSpec((1, 1, 1, ncp, kdim), lambda t, b, g: (t, b, g, 0, 0)),
                  pl.BlockSpec((1, 1, 2 * kdim), lambda t, b, g: (0, 0, 0)),
                  pl.BlockSpec((1, 2 * kdim, NSA_DH), lambda t, b, g: (t, 0, 0)),
                  pl.BlockSpec((1, NSA_DH, NSA_DH), lambda t, b, g: (t, 0, 0)),
                  pl.BlockSpec((1, NSA_DH), lambda t, b, g: (0, 0))],
        out_specs=pl.BlockSpec((1, 1, 1, ncp, NSA_DH), lambda t, b, g: (t, b, g, 0, 0)),
        out_shape=jax.ShapeDtypeStruct((2, bsz, NSA_G, ncp, NSA_DH), BF16),
        compiler_params=_cp("arbitrary", "arbitrary", "arbitrary"),
        name="nsa_compress",
    )(kv, pos, w1, w2, knorm_g.reshape(1, NSA_DH))


def _dist_tile(bt_ref, h, rho):
    return bt_ref[0, h, jnp.clip(rho, 0, N_BIAS_TILES - 1)]


def _nsa_kernel(q_ref, kc_ref, vc_ref, rel_ref, ovt_ref, ks_ref, vs_ref, kw_ref, vw_ref, bt_ref, gt_ref,
                o_ref, sc_ref, *state):
    m_refs, l_refs, acc_refs = state[0:NSA_HPG], state[NSA_HPG:2 * NSA_HPG], state[2 * NSA_HPG:3 * NSA_HPG]
    j = pl.program_id(2)
    QB, HPG, DH = Q_BLOCK, NSA_HPG, NSA_DH
    R = HPG * QB
    ncp = kc_ref.shape[3]
    nsel = ovt_ref.shape[0]
    q = q_ref[0].reshape(R, DH)
    nt = (((1,), (1,)), ((), ()))
    qi = lax.broadcasted_iota(jnp.int32, (QB, 1), 0)
    tpos = j * QB + qi

    cidx = lax.broadcasted_iota(jnp.int32, (QB, ncp), 1)
    mask_c = jnp.logical_and(cidx * CMP_STRIDE + (CMP_LEN - 1) <= tpos, cidx < ncp - 1)
    band0 = (j * (QB // CMP_STRIDE) + ncp - (LANES - QB // CMP_STRIDE)) % ncp
    s_c = lax.dot_general(q, kc_ref[0, 0, 0], nt, preferred_element_type=F32)
    p_c = []
    for h in range(HPG):
        sat = bt_ref[0, h, N_BIAS_TILES - 1]
        bias = jnp.concatenate([rel_ref[0, h]] + [sat] * (ncp // LANES - 1), axis=1)
        bias = pltpu.roll(bias, band0, 1)
        sh = jnp.where(mask_c, s_c[h * QB:(h + 1) * QB, :] + bias, NEG_INF)
        mx = jnp.max(sh, axis=-1, keepdims=True)
        e = jnp.where(mask_c, jnp.exp(sh - mx), 0.0)
        den = jnp.sum(e, axis=-1, keepdims=True)
        p_c.append(e * jnp.where(den > 0.0, 1.0 / den, 0.0))
    o_c = jnp.dot(jnp.concatenate(p_c, axis=0).astype(BF16), vc_ref[0, 0, 0], preferred_element_type=F32)

    psum = p_c[0] + p_c[1] + p_c[2] + p_c[3]
    p_hi = psum.astype(BF16)
    p_lo = (psum - p_hi.astype(F32)).astype(BF16)
    imp = (lax.dot_general(ovt_ref[...], p_hi, nt, preferred_element_type=F32)
           + lax.dot_general(ovt_ref[...], p_lo, nt, preferred_element_type=F32))
    nidx = lax.broadcasted_iota(jnp.int32, (nsel, QB), 0)
    tpos_l = j * QB + lax.broadcasted_iota(jnp.int32, (1, QB), 1)
    blk_t = tpos_l // SLC_LEN
    valid = nidx * SLC_LEN <= tpos_l
    forced = (nidx == 0) | (nidx == blk_t) | (nidx == blk_t - 1)
    score = jnp.where(valid, imp + jnp.where(forced, FORCE_SELECT, 0.0), NEG_INF)
    nidx_f = nidx.astype(F32)
    sel_t = jnp.full((nsel, QB), NEG_INF, F32)
    for _ in range(min(NSA_N_SELECT, nsel)):
        best = jnp.max(score, axis=0, keepdims=True)
        first = jnp.min(jnp.where(score == best, nidx_f, float(nsel)), axis=0, keepdims=True)
        hit = nidx_f == first
        sel_t = jnp.where(hit, 0.0, sel_t)
        score = jnp.where(hit, -jnp.inf, score)
    sel_neg = sel_t.T.astype(BF16)

    for h in range(HPG):
        m_refs[h][...] = jnp.full((QB, LANES), -jnp.inf, F32)
        l_refs[h][...] = jnp.zeros((QB, LANES), F32)
        acc_refs[h][...] = jnp.zeros((QB, DH), F32)
    KT = SEL_TILE
    sub = KT // QB
    bpt = KT // SLC_LEN
    kk_i = lax.broadcasted_iota(jnp.int32, (QB, KT), 1)
    qq_i = lax.broadcasted_iota(jnp.int32, (QB, KT), 0)
    e_row = lax.broadcasted_iota(jnp.int32, (nsel, KT), 0)
    e_col = lax.broadcasted_iota(jnp.int32, (nsel, KT), 1) // SLC_LEN

    def lane_fold(x, op):
        parts = [x[:, u * LANES:(u + 1) * LANES] for u in range(KT // LANES)]
        while len(parts) > 1:
            parts = [op(parts[a], parts[a + 1]) for a in range(0, len(parts), 2)]
        return parts[0]

    def score_step(t, carry):
        k0 = pl.multiple_of(t * KT, KT)
        k_t = ks_ref[0, 0, pl.ds(k0, KT), :]
        expand = jnp.where(e_row == e_col + t * bpt, 1.0, 0.0).astype(BF16)
        madd = jnp.dot(sel_neg, expand, preferred_element_type=F32)
        madd = jnp.where(j * QB + qq_i - (k0 + kk_i) >= 0, madd, NEG_INF)
        s = lax.dot_general(q, k_t, nt, preferred_element_type=F32)
        for h in range(HPG):
            bias = jnp.concatenate([_dist_tile(bt_ref, h, j - t * sub - u) for u in range(sub)], axis=1)
            sh = s[h * QB:(h + 1) * QB, :] + bias + madd
            sc_ref[t, h * QB:(h + 1) * QB, :] = sh
            m_refs[h][...] = jnp.maximum(m_refs[h][...], lane_fold(sh, jnp.maximum))
        return carry

    def value_step(t, carry):
        k0 = pl.multiple_of(t * KT, KT)
        v_t = vs_ref[0, 0, pl.ds(k0, KT), :]
        for h in range(HPG):
            mb = m_refs[h][...]
            p = jnp.exp(sc_ref[t, h * QB:(h + 1) * QB, :] - jnp.concatenate([mb] * (KT // LANES), axis=1))
            l_refs[h][...] += lane_fold(p, jnp.add)
            acc_refs[h][...] += jnp.dot(p.astype(BF16), v_t, preferred_element_type=F32)
        return carry

    n_tiles = (j * QB) // KT + 1
    lax.fori_loop(0, n_tiles, score_step, 0)
    for h in range(HPG):
        m_refs[h][...] = jnp.broadcast_to(jnp.max(m_refs[h][...], axis=-1, keepdims=True), (QB, LANES))
    lax.fori_loop(0, n_tiles, value_step, 0)
    o_s = [acc_refs[h][...] / jnp.sum(l_refs[h][...], axis=-1, keepdims=True) for h in range(HPG)]

    WK = WINDOW + QB
    w0 = pl.multiple_of(j * QB, QB)
    k_w = kw_ref[0, 0, pl.ds(w0, WK), :]
    v_w = vw_ref[0, 0, pl.ds(w0, WK), :]
    wk_i = lax.broadcasted_iota(jnp.int32, (QB, WK), 1)
    wq_i = lax.broadcasted_iota(jnp.int32, (QB, WK), 0)
    dist_w = wq_i - wk_i + WINDOW
    mask_w = (dist_w >= 0) & (dist_w < WINDOW) & (j * QB - WINDOW + wk_i >= 0)
    madd_w = jnp.where(mask_w, 0.0, NEG_INF)
    s_w = lax.dot_general(q, k_w, nt, preferred_element_type=F32)
    nwt = WK // QB
    o_w = []
    for h in range(HPG):
        bias = jnp.concatenate([_dist_tile(bt_ref, h, nwt - 1 - u) for u in range(nwt)], axis=1)
        sh = s_w[h * QB:(h + 1) * QB, :] + bias + madd_w
        mxw = jnp.max(sh, axis=-1, keepdims=True)
        pw = jnp.exp(sh - mxw)
        lw = jnp.sum(pw, axis=-1, keepdims=True)
        o_w.append(jnp.dot(pw.astype(BF16), v_w, preferred_element_type=F32) / lw)

    gate = jax.nn.sigmoid(gt_ref[...])
    for h in range(HPG):
        gc = gate[:, 8 + 3 * h:9 + 3 * h]
        gs = gate[:, 9 + 3 * h:10 + 3 * h]
        gw = gate[:, 10 + 3 * h:11 + 3 * h]
        y = gc * o_c[h * QB:(h + 1) * QB, :] + gs * o_s[h] + gw * o_w[h]
        o_ref[:, h * DH:(h + 1) * DH] = y.astype(o_ref.dtype)


def nsa_tables(rel_bias, seq):
    def bias_of(dist):
        dist = jnp.maximum(dist, 0)
        max_exact = NUM_BUCKETS // 2
        log_ratio = jnp.log(jnp.maximum(dist, 1).astype(F32) / max_exact) / math.log(MAX_DISTANCE / max_exact)
        large = jnp.minimum(max_exact + (log_ratio * (NUM_BUCKETS - max_exact)).astype(jnp.int32), NUM_BUCKETS - 1)
        bucket = jnp.where(dist < max_exact, dist, large)
        onehot = (bucket[..., None] == jnp.arange(NUM_BUCKETS)).astype(F32)
        out = jnp.dot(onehot, rel_bias.astype(F32), precision=lax.Precision.HIGHEST)
        return jnp.moveaxis(out, -1, 0)

    assert seq // CMP_STRIDE >= LANES and Q_BLOCK * (N_BIAS_TILES - 1) - (Q_BLOCK - 1) >= MAX_DISTANCE
    i = jnp.arange(Q_BLOCK)
    dist = jnp.arange(N_BIAS_TILES)[:, None, None] * Q_BLOCK + i[None, :, None] - i[None, None, :]
    bt = bias_of(dist).reshape(NSA_G, NSA_HPG, N_BIAS_TILES, Q_BLOCK, Q_BLOCK)
    per_tile = Q_BLOCK // CMP_STRIDE
    assert (LANES - per_tile) * CMP_STRIDE - (CMP_LEN - 1) >= MAX_DISTANCE
    dist_c = i[:, None] - ((jnp.arange(LANES)[None, :] - (LANES - per_tile)) * CMP_STRIDE + CMP_LEN - 1)
    rel = bias_of(dist_c).reshape(NSA_G, NSA_HPG, Q_BLOCK, LANES)
    return bt, rel


def nsa_overlap(seq):
    ncp = seq // CMP_STRIDE
    ns = seq // SLC_LEN
    cstart = np.arange(ncp) * CMP_STRIDE
    sstart = np.arange(ns) * SLC_LEN
    ov = np.clip(np.minimum(cstart[:, None] + CMP_LEN, sstart[None, :] + SLC_LEN)
                 - np.maximum(cstart[:, None], sstart[None, :]), 0, None).astype(np.float32) / CMP_LEN
    ov[ncp - 1] = 0.0
    return jnp.asarray(ov.T, BF16)


def nsa_attend(z, pa, pw, cmp, bt, rel, ovt, bsz, seq):
    nqb = seq // Q_BLOCK
    ncp = seq // CMP_STRIDE
    nsel = seq // SLC_LEN
    R = NSA_HPG * Q_BLOCK
    return pl.pallas_call(
        _nsa_kernel,
        grid=(bsz, NSA_G, nqb),
        in_specs=[pl.BlockSpec((1, NSA_HPG, Q_BLOCK, NSA_DH), lambda b, g, j: (b, g, j, 0)),
                  pl.BlockSpec((1, 1, 1, ncp, NSA_DH), lambda b, g, j: (0, b, g, 0, 0)),
                  pl.BlockSpec((1, 1, 1, ncp, NSA_DH), lambda b, g, j: (1, b, g, 0, 0)),
                  pl.BlockSpec((1, NSA_HPG, Q_BLOCK, LANES), lambda b, g, j: (g, 0, 0, 0)),
                  pl.BlockSpec((nsel, ncp), lambda b, g, j: (0, 0)),
                  pl.BlockSpec((1, 1, seq, NSA_DH), lambda b, g, j: (b, 8 + g, 0, 0)),
                  pl.BlockSpec((1, 1, seq, NSA_DH), lambda b, g, j: (b, 10 + g, 0, 0)),
                  pl.BlockSpec((1, 1, seq + WINDOW, NSA_DH), lambda b, g, j: (b, g, 0, 0)),
                  pl.BlockSpec((1, 1, seq + WINDOW, NSA_DH), lambda b, g, j: (b, 2 + g, 0, 0)),
                  pl.BlockSpec((1, NSA_HPG, N_BIAS_TILES, Q_BLOCK, Q_BLOCK), lambda b, g, j: (g, 0, 0, 0, 0)),
                  pl.BlockSpec((Q_BLOCK, LANES), lambda b, g, j: (b * nqb + j, Z_SMALL // LANES + g))],
        out_specs=pl.BlockSpec((Q_BLOCK, NSA_HPG * NSA_DH), lambda b, g, j: (b * nqb + j, g)),
        out_shape=jax.ShapeDtypeStruct((bsz * seq, BRANCH_WIDTH), BF16),
        scratch_shapes=([pltpu.VMEM((seq // SEL_TILE, R, SEL_TILE), F32)]
                        + [pltpu.VMEM((Q_BLOCK, LANES), F32)] * (2 * NSA_HPG)
                        + [pltpu.VMEM((Q_BLOCK, NSA_DH), F32)] * NSA_HPG),
        compiler_params=_cp("arbitrary", "arbitrary", "arbitrary"),
        name="nsa_attend",
    )(pa, cmp, cmp, rel, ovt, pa, pa, pw, pw, bt, z)


def _merge_kernel(ya_ref, yb_ref, yc_ref, w_ref, ga_ref, gb_ref, gc_ref, o_ref):
    acc = jax.nn.sigmoid(ga_ref[...]) * jnp.dot(ya_ref[...], w_ref[0], preferred_element_type=F32)
    acc = acc + jax.nn.sigmoid(gb_ref[...]) * jnp.dot(yb_ref[...], w_ref[1], preferred_element_type=F32)
    acc = acc + jax.nn.sigmoid(gc_ref[...]) * jnp.dot(yc_ref[...], w_ref[2], preferred_element_type=F32)
    o_ref[...] = acc.astype(o_ref.dtype)


def merge(ya, yb, yc, w_branch, z, tm=1024, tn=512):
    n = ya.shape[0]
    d = w_branch.shape[2]
    bw = ya.shape[1]
    ymap = lambda i, j: (i, 0)
    gmap = lambda k: (lambda i, j: (i, (Z_GATE + k * d) // tn + j))
    return pl.pallas_call(
        _merge_kernel,
        grid=(n // tm, d // tn),
        in_specs=[pl.BlockSpec((tm, bw), ymap), pl.BlockSpec((tm, bw), ymap), pl.BlockSpec((tm, bw), ymap),
                  pl.BlockSpec((3, bw, tn), lambda i, j: (0, 0, j)),
                  pl.BlockSpec((tm, tn), gmap(0)), pl.BlockSpec((tm, tn), gmap(1)), pl.BlockSpec((tm, tn), gmap(2))],
        out_specs=pl.BlockSpec((tm, tn), lambda i, j: (i, j)),
        out_shape=jax.ShapeDtypeStruct((n, d), BF16),
        compiler_params=_cp("arbitrary", "arbitrary"),
        name="merge",
    )(ya, yb, yc, w_branch, z, z, z)


def _outproj_kernel(a_ref, w_ref, x_ref, g_ref, o_ref):
    o_ref[...] = x_ref[...] + g_ref[0] * jnp.dot(a_ref[...], w_ref[...], preferred_element_type=F32)


def out_proj(a, w, x2d, gate, seq, tm=1024, tn=512):
    n, k = a.shape
    d = w.shape[1]
    per_b = seq // tm
    return pl.pallas_call(
        _outproj_kernel,
        grid=(n // tm, d // tn),
        in_specs=[pl.BlockSpec((tm, k), lambda i, j: (i, 0)),
                  pl.BlockSpec((k, tn), lambda i, j: (0, j)),
                  pl.BlockSpec((tm, tn), lambda i, j: (i, j)),
                  pl.BlockSpec((1, 1, tn), lambda i, j: (i // per_b, 0, j))],
        out_specs=pl.BlockSpec((tm, tn), lambda i, j: (i, j)),
        out_shape=jax.ShapeDtypeStruct((n, d), F32),
        compiler_params=_cp("arbitrary", "arbitrary"),
        name="out_proj",
    )(a, w, x2d, gate[:, None, :])


def _router_kernel(x_ref, g_ref, sc_ref, sh_ref, rw_ref, rwl_ref, rb_ref, h_ref, idx_ref, p_ref):
    x = x_ref[...]
    y = x * lax.rsqrt(jnp.mean(x * x, axis=-1, keepdims=True) + EPS)
    hf = (y * g_ref[...]) * (1.0 + sc_ref[0]) + sh_ref[0]
    h = hf.astype(BF16)
    h_ref[...] = h
    h_lo = (hf - h.astype(F32)).astype(BF16)
    logits = (jnp.dot(h, rw_ref[...], preferred_element_type=F32)
              + jnp.dot(h_lo, rw_ref[...], preferred_element_type=F32)
              + jnp.dot(h, rwl_ref[...], preferred_element_type=F32)) + rb_ref[...]
    lane = lax.broadcasted_iota(jnp.int32, logits.shape, 1)
    lane_f = lane.astype(F32)
    idx_out = jnp.zeros(logits.shape, F32)
    val_out = jnp.full(logits.shape, -jnp.inf, F32)
    for k in range(TOP_K):
        best = jnp.max(logits, axis=-1, keepdims=True)
        first = jnp.min(jnp.where(logits == best, lane_f, float(LANES)), axis=-1, keepdims=True)
        idx_out = jnp.where(lane == k, first, idx_out)
        val_out = jnp.where(lane == k, best, val_out)
        logits = jnp.where(lane_f == first, -jnp.inf, logits)
    e = jnp.exp(val_out - jnp.max(val_out, axis=-1, keepdims=True))
    idx_ref[...] = idx_out.astype(jnp.int32)
    p_ref[...] = e / jnp.sum(e, axis=-1, keepdims=True)


def router(x2d, g, sc, sh, rw, rb, seq, tm=1024):
    n, d = x2d.shape
    per_b = seq // tm
    rw_hi = rw.astype(BF16)
    rw_lo = (rw - rw_hi.astype(F32)).astype(BF16)
    rwp = jnp.zeros((d, LANES), BF16).at[:, :N_EXPERTS].set(rw_hi)
    rwl = jnp.zeros((d, LANES), BF16).at[:, :N_EXPERTS].set(rw_lo)
    rbp = jnp.full((1, LANES), -jnp.inf, F32).at[0, :N_EXPERTS].set(rb)
    return pl.pallas_call(
        _router_kernel,
        grid=(n // tm,),
        in_specs=[pl.BlockSpec((tm, d), lambda i: (i, 0)),
                  pl.BlockSpec((1, d), lambda i: (0, 0)),
                  pl.BlockSpec((1, 1, d), lambda i: (i // per_b, 0, 0)),
                  pl.BlockSpec((1, 1, d), lambda i: (i // per_b, 0, 0)),
                  pl.BlockSpec((d, LANES), lambda i: (0, 0)),
                  pl.BlockSpec((d, LANES), lambda i: (0, 0)),
                  pl.BlockSpec((1, LANES), lambda i: (0, 0))],
        out_specs=[pl.BlockSpec((tm, d), lambda i: (i, 0)),
                   pl.BlockSpec((tm, LANES), lambda i: (i, 0)),
                   pl.BlockSpec((tm, LANES), lambda i: (i, 0))],
        out_shape=[jax.ShapeDtypeStruct((n, d), BF16),
                   jax.ShapeDtypeStruct((n, LANES), jnp.int32),
                   jax.ShapeDtypeStruct((n, LANES), F32)],
        compiler_params=_cp("arbitrary"),
        name="router",
    )(x2d, g.reshape(1, d), sc[:, None, :], sh[:, None, :], rwp, rwl, rbp)


def _pack_bf16_pairs(y):
    half = y.shape[1] // 2
    lo = lax.bitcast_convert_type(y[:, :half].astype(BF16).astype(F32), jnp.uint32)
    hi = lax.bitcast_convert_type(y[:, half:].astype(BF16).astype(F32), jnp.uint32)
    return lax.shift_right_logical(lo, jnp.uint32(16)) | (hi & jnp.uint32(0xFFFF0000))


def _unpack_bf16_pairs(w):
    lo = lax.bitcast_convert_type(lax.shift_left(w, jnp.uint32(16)), F32)
    hi = lax.bitcast_convert_type(w & jnp.uint32(0xFFFF0000), F32)
    return jnp.concatenate([lo, hi], axis=1)


def _expert_kernel(be_ref, nv_ref, x_ref, wg_ref, wu_ref, bg_ref, bu_ref, w2_ref, b2_ref, o_ref, acc_ref):
    i = pl.program_id(0)
    f = pl.program_id(1)

    @pl.when(i < nv_ref[0])
    def _():
        @pl.when(f == 0)
        def _():
            acc_ref[...] = jnp.zeros(acc_ref.shape, F32)

        x = x_ref[...]
        gate = jnp.dot(x, wg_ref[0, 0].astype(BF16), preferred_element_type=F32) + bg_ref[0, 0]
        up = jnp.dot(x, wu_ref[0, 0].astype(BF16), preferred_element_type=F32) + bu_ref[0, 0]
        gate = jnp.minimum(gate, SWIGLU_LIMIT)
        up = jnp.clip(up, -SWIGLU_LIMIT, SWIGLU_LIMIT)
        act = (up + 1.0) * (gate * jax.nn.sigmoid(SWIGLU_ALPHA * gate))
        acc_ref[...] += jnp.dot(act.astype(BF16), w2_ref[0, 0].astype(BF16), preferred_element_type=F32)

        @pl.when(f == pl.num_programs(1) - 1)
        def _():
            o_ref[...] = _pack_bf16_pairs(acc_ref[...] + b2_ref[0, 0])

    @pl.when(i >= nv_ref[0])
    def _():
        o_ref[...] = jnp.zeros(o_ref.shape, o_ref.dtype)


def experts(xb, block_exp, n_valid, w1, b1, w2, b2, layer, tm, tf=256):
    rows, d = xb.shape
    nf = D_FF // tf
    nblk = rows // tm
    n_exp = w1.shape[1]
    fe = lambda i, f, nv: jnp.where(i < nv[0], f, nf - 1)
    grid_spec = pltpu.PrefetchScalarGridSpec(
        num_scalar_prefetch=2,
        grid=(nblk, nf),
        in_specs=[pl.BlockSpec((tm, d), lambda i, f, be, nv: (jnp.minimum(i, nv[0] - 1), 0)),
                  pl.BlockSpec((1, 1, d, tf), lambda i, f, be, nv: (layer, be[i], 0, fe(i, f, nv))),
                  pl.BlockSpec((1, 1, d, tf), lambda i, f, be, nv: (layer, be[i], 0, nf + fe(i, f, nv))),
                  pl.BlockSpec((1, 1, 1, tf), lambda i, f, be, nv: (layer * n_exp + be[i], 0, 0, fe(i, f, nv))),
                  pl.BlockSpec((1, 1, 1, tf), lambda i, f, be, nv: (layer * n_exp + be[i], 0, 0, nf + fe(i, f, nv))),
                  pl.BlockSpec((1, 1, tf, d), lambda i, f, be, nv: (layer, be[i], fe(i, f, nv), 0)),
                  pl.BlockSpec((1, 1, 1, d), lambda i, f, be, nv: (layer * n_exp + be[i], 0, 0, 0))],
        out_specs=pl.BlockSpec((tm, d // 2), lambda i, f, be, nv: (i, 0)),
        scratch_shapes=[pltpu.VMEM((tm, d), F32)],
    )
    b1r = b1.reshape(-1, 1, 1, b1.shape[-1])
    b2r = b2.reshape(-1, 1, 1, d)
    return pl.pallas_call(
        _expert_kernel,
        grid_spec=grid_spec,
        out_shape=jax.ShapeDtypeStruct((rows, d // 2), jnp.uint32),
        compiler_params=_cp("arbitrary", "arbitrary"),
        name="experts",
    )(block_exp, n_valid, xb, w1, w1, b1r, b1r, w2, b2r)


def _combine_kernel(dest_ref, p_ref, x_ref, g_ref, yb_ref, o_ref, ybuf_ref, sem):
    tm = x_ref.shape[0]

    def row_copy(k, r):
        src = dest_ref[0, 0, k * tm + r]
        return pltpu.make_async_copy(yb_ref.at[pl.ds(src, 1), :], ybuf_ref.at[k, pl.ds(r, 1), :], sem)

    def start(r, carry):
        for k in range(TOP_K):
            row_copy(k, r).start()
        return carry

    def wait(r, carry):
        for k in range(TOP_K):
            row_copy(k, r).wait()
        return carry

    lax.fori_loop(0, tm, start, 0, unroll=8)
    lax.fori_loop(0, tm, wait, 0, unroll=8)
    p = p_ref[...]
    y = p[:, 0:1] * _unpack_bf16_pairs(ybuf_ref[0])
    for k in range(1, TOP_K):
        y = y + p[:, k:k + 1] * _unpack_bf16_pairs(ybuf_ref[k])
    o_ref[...] = x_ref[...] + g_ref[0] * y


def combine(yb, dest, prob, x2d, gate, seq, tm=512):
    n, d = x2d.shape
    per_b = seq // tm
    nb = n // tm
    dest_tiles = dest.reshape(TOP_K, nb, tm).transpose(1, 0, 2).reshape(nb, 1, TOP_K * tm)
    return pl.pallas_call(
        _combine_kernel,
        grid=(nb,),
        in_specs=[pl.BlockSpec((1, 1, TOP_K * tm), lambda i: (i, 0, 0), memory_space=pltpu.SMEM),
                  pl.BlockSpec((tm, LANES), lambda i: (i, 0)),
                  pl.BlockSpec((tm, d), lambda i: (i, 0)),
                  pl.BlockSpec((1, 1, d), lambda i: (i // per_b, 0, 0)),
                  pl.BlockSpec(memory_space=pl.ANY)],
        out_specs=pl.BlockSpec((tm, d), lambda i: (i, 0)),
        out_shape=jax.ShapeDtypeStruct((n, d), F32),
        scratch_shapes=[pltpu.VMEM((TOP_K, tm, d // 2), jnp.uint32), pltpu.SemaphoreType.DMA(())],
        compiler_params=_cp("arbitrary"),
        name="combine",
    )(dest_tiles, prob, x2d, gate[:, None, :], yb)


def moe(x2d, g, sc, sh, gate, rw, rb, w1, b1, w2, b2, layer, seq, tm_e=1024):
    n, d = x2d.shape
    h, idx, prob = router(x2d, g, sc, sh, rw, rb, seq)
    e_flat = idx[:, :TOP_K].T.reshape(-1)
    na = n * TOP_K
    onehot = (e_flat[:, None] == jnp.arange(N_EXPERTS)[None, :]).astype(jnp.int32)
    csum = jnp.cumsum(onehot, axis=0)
    counts = csum[-1]
    rank = jnp.take_along_axis(csum, e_flat[:, None], axis=1)[:, 0] - 1
    padded = (counts + tm_e - 1) // tm_e * tm_e
    pend = jnp.cumsum(padded)
    pstart = pend - padded
    dest = pstart[e_flat] + rank
    nblk = na // tm_e + N_EXPERTS
    rows = nblk * tm_e
    buf_tok = jnp.zeros((rows,), jnp.int32).at[dest].set(jnp.arange(na, dtype=jnp.int32) % n)
    starts = jnp.arange(nblk, dtype=jnp.int32) * tm_e
    block_exp = jnp.minimum(jnp.sum(pend[None, :] <= starts[:, None], axis=1), N_EXPERTS - 1).astype(jnp.int32)
    n_valid = (pend[-1:] // tm_e).astype(jnp.int32)
    block_exp = jnp.where(starts < pend[-1], block_exp, block_exp[jnp.maximum(n_valid[0] - 1, 0)])
    xb = h[buf_tok]
    yb = experts(xb, block_exp, n_valid, w1, b1, w2, b2, layer, tm_e)
    return combine(yb, dest, prob, x2d, gate, seq)


def fused_in_weight(w_in, w_gate):
    sizes = (512, 512, 1024, 4, 4, 1024, 1024, 256, 256, 256, 256, 256, 256, 24, 1024, 1024)
    pts = np.cumsum(sizes)[:-1].tolist()
    (qm, km, vm, ig, fg, og, qn, kc, vc, ks, vs, kw, vw, gn, gu, gv) = jnp.split(w_in, pts, axis=1)
    d = w_in.shape[0]
    pad = lambda k: jnp.zeros((d, k), w_in.dtype)
    per_group = NSA_HPG * 3
    small0 = jnp.concatenate([ig, fg, gn[:, :per_group], pad(LANES - 8 - per_group)], axis=1)
    small1 = jnp.concatenate([pad(8), gn[:, per_group:], pad(LANES - 8 - per_group)], axis=1)
    cols = [qm, km, vm, og, qn, gu, gv, kc, vc, ks, vs, kw, vw, w_gate[0], w_gate[1], w_gate[2], small0, small1]
    w = jnp.concatenate(cols, axis=1)
    w = jnp.concatenate([w, pad(Z_WIDTH - w.shape[1])], axis=1)
    return w.astype(BF16)


def kernel(x, c, ada_w, ada_b, norm1_g, norm2_g, w_in, conv_w, conv_b, mlstm_gate_b, mlstm_norm_g, cmp_pos,
           cmp_k_w1, cmp_k_w2, cmp_v_w1, cmp_v_w2, qnorm_g, knorm_g, rel_bias, gmlp_norm_g, gmlp_ws, gmlp_b,
           w_branch, w_gate, w_out, router_w, router_b, exp_w1, exp_b1, exp_w2, exp_b2):
    bsz, seq, d = x.shape
    depth = ada_w.shape[0]
    x2d = x.reshape(bsz * seq, d)
    bt, rel = nsa_tables(rel_bias, seq)
    ovt = nsa_overlap(seq)
    for l in range(depth):
        mod = ada_mod(c, ada_w, ada_b, l)
        sh1, sc1, g1, sh2, sc2, g2 = jnp.split(mod, 6, axis=-1)
        z = in_proj(x2d, norm1_g[l], sc1, sh1, fused_in_weight(w_in[l], w_gate[l]), seq)
        ya = mlstm(z, bsz, seq, conv_w[l], conv_b[l], mlstm_gate_b[l], mlstm_norm_g[l])
        pa, pw = nsa_prep(z, bsz, seq, qnorm_g[l], knorm_g[l])
        cmp = nsa_compress(z, bsz, seq, cmp_pos[l], cmp_k_w1[l], cmp_k_w2[l], cmp_v_w1[l], cmp_v_w2[l], knorm_g[l])
        yb = nsa_attend(z, pa, pw, cmp, bt, rel, ovt, bsz, seq)
        yc = gmlp(z, gmlp_norm_g[l], gmlp_ws[l], gmlp_b[l])
        merged = merge(ya, yb, yc, w_branch[l].astype(BF16), z)
        x2d = out_proj(merged, w_out[l].astype(BF16), x2d, g1, seq)
        x2d = moe(x2d, norm2_g[l], sc2, sh2, g2, router_w[l], router_b[l], exp_w1, exp_b1, exp_w2, exp_b2, l, seq)
    return x2d.reshape(bsz, seq, d)
```

```python
import functools
import math

import jax
import jax.numpy as jnp
import numpy as np
from jax import lax
from jax.experimental import pallas as pl
from jax.experimental.pallas import tpu as pltpu

F32 = jnp.float32
BF16 = jnp.bfloat16

LANES = 128
VMEM_LIMIT = 56 * 1024 * 1024

D_MODEL = 2048
BRANCH_WIDTH = 1024
MLSTM_HEADS = 4
MLSTM_DV = 256
MLSTM_DQK = 128
MLSTM_TILE = 256
CONV_WIDTH = 4
NSA_HEADS = 8
NSA_G = 2
NSA_HPG = 4
NSA_DH = 128
CMP_LEN = 32
CMP_STRIDE = 16
SLC_LEN = 64
NSA_N_SELECT = 16
WINDOW = 512
Q_BLOCK = 128
NUM_BUCKETS = 32
MAX_DISTANCE = 1024
GMLP_CHUNK = 128
GMLP_GROUPS = 8
N_EXPERTS = 32
TOP_K = 4
D_FF = 1536
SWIGLU_LIMIT = 7.0
SWIGLU_ALPHA = 1.702
NEG_INF = -1e30
FORCE_SELECT = 1e4
EPS = 1e-6

SEL_TILE = 512
N_BIAS_TILES = 10

Z_QK, Z_V, Z_OG, Z_NQ, Z_GU, Z_GV = 0, 1024, 2048, 3072, 4096, 5120
Z_KV = 6144
Z_GATE = 7680
Z_SMALL = 13824
Z_WIDTH = 14336


def _cp(*sem):
    return pltpu.CompilerParams(dimension_semantics=sem, vmem_limit_bytes=VMEM_LIMIT)


def _ada_kernel(c_ref, w_ref, b_ref, o_ref):
    a = jax.nn.silu(c_ref[...]).astype(BF16)
    o_ref[...] = jnp.dot(a, w_ref[0].astype(BF16), preferred_element_type=F32) + b_ref[0]


def ada_mod(c, w, b, layer):
    bsz, d = c.shape
    n = w.shape[2]
    tn = 1024
    cp = jnp.zeros((8, d), F32).at[:bsz].set(c)
    out = pl.pallas_call(
        _ada_kernel,
        grid=(n // tn,),
        in_specs=[pl.BlockSpec((8, d), lambda j: (0, 0)),
                  pl.BlockSpec((1, d, tn), lambda j: (layer, 0, j)),
                  pl.BlockSpec((1, 1, tn), lambda j: (layer, 0, j))],
        out_specs=pl.BlockSpec((8, tn), lambda j: (0, j)),
        out_shape=jax.ShapeDtypeStruct((8, n), F32),
        compiler_params=_cp("arbitrary"),
        name="ada_mod",
    )(cp, w, b[:, None, :])
    return out[:bsz]


def _in_kernel(x_ref, g_ref, sc_ref, sh_ref, w_ref, o_ref, h_ref):
    @pl.when(pl.program_id(1) == 0)
    def _():
        x = x_ref[...]
        y = x * lax.rsqrt(jnp.mean(x * x, axis=-1, keepdims=True) + EPS)
        h = (y * g_ref[...]) * (1.0 + sc_ref[0]) + sh_ref[0]
        h_ref[...] = h.astype(BF16)

    o_ref[...] = jnp.dot(h_ref[...], w_ref[...], preferred_element_type=F32)


def in_proj(x2d, g, sc, sh, wcat, seq, tm=1024, tn=512):
    n, d = x2d.shape
    w = wcat.shape[1]
    per_b = seq // tm
    return pl.pallas_call(
        _in_kernel,
        grid=(n // tm, w // tn),
        in_specs=[pl.BlockSpec((tm, d), lambda i, j: (i, 0)),
                  pl.BlockSpec((1, d), lambda i, j: (0, 0)),
                  pl.BlockSpec((1, 1, d), lambda i, j: (i // per_b, 0, 0)),
                  pl.BlockSpec((1, 1, d), lambda i, j: (i // per_b, 0, 0)),
                  pl.BlockSpec((d, tn), lambda i, j: (0, j))],
        out_specs=pl.BlockSpec((tm, tn), lambda i, j: (i, j)),
        out_shape=jax.ShapeDtypeStruct((n, w), F32),
        scratch_shapes=[pltpu.VMEM((tm, d), BF16)],
        compiler_params=_cp("arbitrary", "arbitrary"),
        name="in_proj",
    )(x2d, g.reshape(1, d), sc[:, None, :], sh[:, None, :], wcat)


def _cumsum_rows(x, n):
    row = lax.broadcasted_iota(jnp.int32, x.shape, 0)
    s = 1
    while s < n:
        x = x + jnp.where(row >= s, pltpu.roll(x, s, 0), 0.0)
        s *= 2
    return x


def _mlstm_kernel(qk_ref, v_ref, og_ref, sm_ref, cw_ref, cb_ref, gb_ref, ng_ref, o_ref,
                  buf_ref, ct_ref, n_ref, m_ref):
    L = qk_ref.shape[0]
    H, DQK, DV = MLSTM_HEADS, MLSTM_DQK, MLSTM_DV

    @pl.when(pl.program_id(1) == 0)
    def _():
        buf_ref[0:8, :] = jnp.zeros((8, buf_ref.shape[1]), F32)
        ct_ref[...] = jnp.zeros(ct_ref.shape, F32)
        n_ref[...] = jnp.zeros(n_ref.shape, F32)
        m_ref[...] = jnp.full(m_ref.shape, NEG_INF, F32)

    buf_ref[8:8 + L, :] = qk_ref[...]
    conv = cb_ref[...] + cw_ref[3:4, :] * buf_ref[8:8 + L, :]
    for j in range(CONV_WIDTH - 1):
        conv = conv + cw_ref[j:j + 1, :] * buf_ref[5 + j:5 + j + L, :]
    qk = jax.nn.silu(conv)
    buf_ref[0:8, :] = buf_ref[L:L + 8, :]

    sm = sm_ref[...] + gb_ref[...]
    bcum = _cumsum_rows(jax.nn.log_sigmoid(sm), L)
    sm_t = sm.T
    bcum_t = bcum.T
    row = lax.broadcasted_iota(jnp.int32, (L, L), 0)
    col = lax.broadcasted_iota(jnp.int32, (L, L), 1)
    causal = row >= col

    for h in range(H):
        qf = qk[:, h * DQK:(h + 1) * DQK] * DQK ** -0.5
        q = qf.astype(BF16)
        kf = qk[:, (H + h) * DQK:(H + h + 1) * DQK]
        k = kf.astype(BF16)
        v = v_ref[:, h * DV:(h + 1) * DV].astype(BF16)
        b_col = bcum[:, H + h:H + h + 1]
        i_col = sm[:, h:h + 1]
        b_row = bcum_t[H + h:H + h + 1, :]
        i_row = sm_t[h:h + 1, :]
        m_prev = m_ref[h:h + 1, 0:1]
        a = b_col + m_prev
        dmat = jnp.where(causal, b_col - b_row + i_row, -jnp.inf)
        m_t = jnp.maximum(a, jnp.max(dmat, axis=1, keepdims=True))
        w_intra = jnp.exp(dmat - m_t)
        w_inter = jnp.exp(a - m_t)
        s = lax.dot_general(q, k, (((1,), (1,)), ((), ())), preferred_element_type=F32) * w_intra
        ct = ct_ref[h]
        nvec = n_ref[h:h + 1, :]
        num = jnp.dot(s.astype(BF16), v, preferred_element_type=F32) + w_inter * jnp.dot(
            q, ct.astype(BF16), preferred_element_type=F32)
        qn = jnp.sum(qf * nvec, axis=1, keepdims=True)
        den = jnp.sum(s, axis=1, keepdims=True) + w_inter * qn
        hm = num / jnp.maximum(jnp.abs(den), jnp.exp(-m_t))
        b_last = b_col[L - 1:L, :]
        gdec = b_last - b_col + i_col
        m_new = jnp.maximum(b_last + m_prev, jnp.max(gdec, axis=0, keepdims=True))
        ws = jnp.exp(gdec - m_new)
        wc = jnp.exp(b_last + m_prev - m_new)
        kw = ws * kf
        ct_ref[h] = wc * ct + jnp.dot(kw.T.astype(BF16), v, preferred_element_type=F32)
        n_ref[h:h + 1, :] = wc * nvec + jnp.sum(kw, axis=0, keepdims=True)
        m_ref[h:h + 1, :] = jnp.broadcast_to(m_new, (1, LANES))
        y = hm * lax.rsqrt(jnp.mean(hm * hm, axis=-1, keepdims=True) + EPS) * ng_ref[:, h * DV:(h + 1) * DV]
        og = og_ref[:, h * DV:(h + 1) * DV]
        o_ref[:, h * DV:(h + 1) * DV] = (jax.nn.sigmoid(og) * y).astype(o_ref.dtype)


def mlstm(z, bsz, seq, conv_w, conv_b, gate_b, norm_g):
    L = MLSTM_TILE
    nch = seq // L
    gb = jnp.zeros((1, LANES), F32).at[0, :2 * MLSTM_HEADS].set(gate_b)
    wide = BRANCH_WIDTH
    rowmap = lambda col: (lambda b, c: (b * nch + c, col))
    const = lambda b, c: (0, 0)
    return pl.pallas_call(
        _mlstm_kernel,
        grid=(bsz, nch),
        in_specs=[pl.BlockSpec((L, wide), rowmap(Z_QK // wide)),
                  pl.BlockSpec((L, wide), rowmap(Z_V // wide)),
                  pl.BlockSpec((L, wide), rowmap(Z_OG // wide)),
                  pl.BlockSpec((L, LANES), rowmap(Z_SMALL // LANES)),
                  pl.BlockSpec((CONV_WIDTH, wide), const),
                  pl.BlockSpec((1, wide), const),
                  pl.BlockSpec((1, LANES), const),
                  pl.BlockSpec((1, wide), const)],
        out_specs=pl.BlockSpec((L, wide), lambda b, c: (b * nch + c, 0)),
        out_shape=jax.ShapeDtypeStruct((bsz * seq, wide), BF16),
        scratch_shapes=[pltpu.VMEM((L + 8, wide), F32),
                        pltpu.VMEM((MLSTM_HEADS, MLSTM_DQK, MLSTM_DV), F32),
                        pltpu.VMEM((8, MLSTM_DQK), F32),
                        pltpu.VMEM((8, LANES), F32)],
        compiler_params=_cp("arbitrary", "arbitrary"),
        name="mlstm",
    )(z, z, z, z, conv_w, conv_b.reshape(1, wide), gb, norm_g.reshape(1, wide))


def _gmlp_kernel(u_ref, v_ref, ng_ref, ws_ref, b_ref, o_ref):
    T = u_ref.shape[0]
    C = GMLP_CHUNK
    u = jax.nn.gelu(u_ref[...])
    vg = jax.nn.gelu(v_ref[...])
    mu = jnp.mean(vg, axis=-1, keepdims=True)
    var = jnp.mean(jnp.square(vg - mu), axis=-1, keepdims=True)
    v = ((vg - mu) * lax.rsqrt(var + EPS) * ng_ref[...]).astype(BF16)
    row = lax.broadcasted_iota(jnp.int32, (C, C), 0)
    col = lax.broadcasted_iota(jnp.int32, (C, C), 1)
    for g in range(GMLP_GROUPS):
        wsc = jnp.where(row >= col, ws_ref[g], 0.0).astype(BF16)
        bcol = b_ref[:, g:g + 1]
        for n in range(T // C):
            vv = v[n * C:(n + 1) * C, g * C:(g + 1) * C]
            s = jnp.dot(wsc, vv, preferred_element_type=F32) + bcol
            o_ref[n * C:(n + 1) * C, g * C:(g + 1) * C] = (
                u[n * C:(n + 1) * C, g * C:(g + 1) * C] * s).astype(o_ref.dtype)


def gmlp(z, norm_g, ws, b, tg=512):
    n = z.shape[0]
    wide = BRANCH_WIDTH
    return pl.pallas_call(
        _gmlp_kernel,
        grid=(n // tg,),
        in_specs=[pl.BlockSpec((tg, wide), lambda i: (i, Z_GU // wide)),
                  pl.BlockSpec((tg, wide), lambda i: (i, Z_GV // wide)),
                  pl.BlockSpec((1, wide), lambda i: (0, 0)),
                  pl.BlockSpec((GMLP_GROUPS, GMLP_CHUNK, GMLP_CHUNK), lambda i: (0, 0, 0)),
                  pl.BlockSpec((GMLP_CHUNK, GMLP_GROUPS), lambda i: (0, 0))],
        out_specs=pl.BlockSpec((tg, wide), lambda i: (i, 0)),
        out_shape=jax.ShapeDtypeStruct((n, wide), BF16),
        compiler_params=_cp("arbitrary"),
        name="gmlp",
    )(z, z, norm_g.reshape(1, wide), ws, b.T)


def _prep_kernel(norm_cols, x_ref, g_ref, o_ref):
    c = pl.program_id(2)
    x = x_ref[...]
    y = x * lax.rsqrt(jnp.mean(x * x, axis=-1, keepdims=True) + EPS) * g_ref[0]
    flag = functools.reduce(jnp.logical_or, [c == nc for nc in norm_cols])
    o_ref[0, 0] = jnp.where(flag, y, x).astype(o_ref.dtype)


def _prep_pad_kernel(norm_cols, npad, x_ref, g_ref, o_ref):
    @pl.when(pl.program_id(1) < npad)
    def _():
        o_ref[...] = jnp.zeros(o_ref.shape, o_ref.dtype)

    @pl.when(pl.program_id(1) >= npad)
    def _():
        _prep_kernel(norm_cols, x_ref, g_ref, o_ref)


def nsa_prep(z, bsz, seq, qnorm_g, knorm_g, ts=512):
    nblk = seq // ts
    gq = (qnorm_g * NSA_DH ** -0.5).reshape(1, 1, LANES)
    gk = knorm_g.reshape(1, 1, LANES)
    gains_a = jnp.concatenate([jnp.tile(gq, (8, 1, 1)), jnp.tile(gk, (4, 1, 1))], axis=0)
    q0, ks0 = Z_NQ // LANES, (Z_KV + 512) // LANES
    pa = pl.pallas_call(
        functools.partial(_prep_kernel, tuple(range(10))),
        grid=(bsz, nblk, 12),
        in_specs=[pl.BlockSpec((ts, LANES), lambda b, s, c: (b * nblk + s, jnp.where(c < 8, q0 + c, ks0 + c - 8))),
                  pl.BlockSpec((1, 1, LANES), lambda b, s, c: (c, 0, 0))],
        out_specs=pl.BlockSpec((1, 1, ts, LANES), lambda b, s, c: (b, c, s, 0)),
        out_shape=jax.ShapeDtypeStruct((bsz, 12, seq, LANES), BF16),
        compiler_params=_cp("arbitrary", "arbitrary", "arbitrary"),
        name="nsa_prep_qks",
    )(z, gains_a)
    gains_w = jnp.tile(gk, (4, 1, 1))
    kw0 = (Z_KV + 1024) // LANES
    npad = WINDOW // ts
    pw = pl.pallas_call(
        functools.partial(_prep_pad_kernel, (0, 1), npad),
        grid=(bsz, nblk + npad, 4),
        in_specs=[pl.BlockSpec((ts, LANES), lambda b, s, c: (b * nblk + jnp.maximum(s - npad, 0), kw0 + c)),
                  pl.BlockSpec((1, 1, LANES), lambda b, s, c: (c, 0, 0))],
        out_specs=pl.BlockSpec((1, 1, ts, LANES), lambda b, s, c: (b, c, s, 0)),
        out_shape=jax.ShapeDtypeStruct((bsz, 4, seq + WINDOW, LANES), BF16),
        compiler_params=_cp("arbitrary", "arbitrary", "arbitrary"),
        name="nsa_prep_win",
    )(z, gains_w)
    return pa, pw


def _compress_kernel(x_ref, pos_ref, w1_ref, w2_ref, g_ref, o_ref):
    half = x_ref.shape[-1]
    x = x_ref[0, 0, 0]
    lo = jnp.dot((x + pos_ref[0, :, 0:half]).astype(BF16), w1_ref[0, 0:half, :], preferred_element_type=F32)
    hi = jnp.dot((x + pos_ref[0, :, half:2 * half]).astype(BF16), w1_ref[0, half:2 * half, :],
                 preferred_element_type=F32)
    ncp = x.shape[0]
    pre = lo + pltpu.roll(hi, ncp - 1, 0)
    y = jnp.dot(jax.nn.gelu(pre).astype(BF16), w2_ref[0], preferred_element_type=F32)
    yn = y * lax.rsqrt(jnp.mean(y * y, axis=-1, keepdims=True) + EPS) * g_ref[...]
    o_ref[0, 0, 0] = jnp.where(pl.program_id(0) == 0, yn, y).astype(o_ref.dtype)


def nsa_compress(z, bsz, seq, cmp_pos, w1k, w2k, w1v, w2v, knorm_g):
    ncp = seq // CMP_STRIDE
    kv = z[:, Z_KV:Z_KV + 512].reshape(bsz, seq, 2, NSA_G, NSA_DH)
    kv = kv.transpose(2, 0, 3, 1, 4).reshape(2, bsz, NSA_G, ncp, CMP_STRIDE * NSA_DH)
    pos = cmp_pos.reshape(1, 1, CMP_LEN * NSA_DH)
    w1 = jnp.stack([w1k, w1v]).astype(BF16)
    w2 = jnp.stack([w2k, w2v]).astype(BF16)
    kdim = CMP_STRIDE * NSA_DH
    return pl.pallas_call(
        _compress_kernel,
        grid=(2, bsz, NSA_G),
        in_specs=[pl.BlockSpec((1, 1, 1, ncp, kdim), lambda t, b, g: (t, b, g, 0, 0)),
                  pl.BlockSpec((1, 1, 2 * kdim), lambda t, b, g: (0, 0, 0)),
                  pl.BlockSpec((1, 2 * kdim, NSA_DH), lambda t, b, g: (t, 0, 0)),
                  pl.BlockSpec((1, NSA_DH, NSA_DH), lambda t, b, g: (t, 0, 0)),
                  pl.BlockSpec((1, NSA_DH), lambda t, b, g: (0, 0))],
        out_specs=pl.BlockSpec((1, 1, 1, ncp, NSA_DH), lambda t, b, g: (t, b, g, 0, 0)),
        out_shape=jax.ShapeDtypeStruct((2, bsz, NSA_G, ncp, NSA_DH), BF16),
        compiler_params=_cp("arbitrary", "arbitrary", "arbitrary"),
        name="nsa_compress",
    )(kv, pos, w1, w2, knorm_g.reshape(1, NSA_DH))


def _dist_tile(bt_ref, h, rho):
    return bt_ref[0, h, jnp.clip(rho, 0, N_BIAS_TILES - 1)]


def _nsa_kernel(q_ref, kc_ref, vc_ref, rel_ref, ovt_ref, ks_ref, vs_ref, kw_ref, vw_ref, bt_ref, gt_ref,
                o_ref, sc_ref, *state):
    m_refs, l_refs, acc_refs = state[0:NSA_HPG], state[NSA_HPG:2 * NSA_HPG], state[2 * NSA_HPG:3 * NSA_HPG]
    j = pl.program_id(2)
    QB, HPG, DH = Q_BLOCK, NSA_HPG, NSA_DH
    R = HPG * QB
    ncp = kc_ref.shape[3]
    nsel = ovt_ref.shape[0]
    q = q_ref[0].reshape(R, DH)
    nt = (((1,), (1,)), ((), ()))
    qi = lax.broadcasted_iota(jnp.int32, (QB, 1), 0)
    tpos = j * QB + qi

    cidx = lax.broadcasted_iota(jnp.int32, (QB, ncp), 1)
    mask_c = jnp.logical_and(cidx * CMP_STRIDE + (CMP_LEN - 1) <= tpos, cidx < ncp - 1)
    band0 = (j * (QB // CMP_STRIDE) + ncp - (LANES - QB // CMP_STRIDE)) % ncp
    s_c = lax.dot_general(q, kc_ref[0, 0, 0], nt, preferred_element_type=F32)
    p_c = []
    for h in range(HPG):
        sat = bt_ref[0, h, N_BIAS_TILES - 1]
        bias = jnp.concatenate([rel_ref[0, h]] + [sat] * (ncp // LANES - 1), axis=1)
        bias = pltpu.roll(bias, band0, 1)
        sh = jnp.where(mask_c, s_c[h * QB:(h + 1) * QB, :] + bias, NEG_INF)
        mx = jnp.max(sh, axis=-1, keepdims=True)
        e = jnp.where(mask_c, jnp.exp(sh - mx), 0.0)
        den = jnp.sum(e, axis=-1, keepdims=True)
        p_c.append(e * jnp.where(den > 0.0, 1.0 / den, 0.0))
    o_c = jnp.dot(jnp.concatenate(p_c, axis=0).astype(BF16), vc_ref[0, 0, 0], preferred_element_type=F32)

    psum = p_c[0] + p_c[1] + p_c[2] + p_c[3]
    p_hi = psum.astype(BF16)
    p_lo = (psum - p_hi.astype(F32)).astype(BF16)
    imp = (lax.dot_general(ovt_ref[...], p_hi, nt, preferred_element_type=F32)
           + lax.dot_general(ovt_ref[...], p_lo, nt, preferred_element_type=F32))
    nidx = lax.broadcasted_iota(jnp.int32, (nsel, QB), 0)
    tpos_l = j * QB + lax.broadcasted_iota(jnp.int32, (1, QB), 1)
    blk_t = tpos_l // SLC_LEN
    valid = nidx * SLC_LEN <= tpos_l
    forced = (nidx == 0) | (nidx == blk_t) | (nidx == blk_t - 1)
    score = jnp.where(valid, imp + jnp.where(forced, FORCE_SELECT, 0.0), NEG_INF)
    nidx_f = nidx.astype(F32)
    sel_t = jnp.full((nsel, QB), NEG_INF, F32)
    for _ in range(min(NSA_N_SELECT, nsel)):
        best = jnp.max(score, axis=0, keepdims=True)
        first = jnp.min(jnp.where(score == best, nidx_f, float(nsel)), axis=0, keepdims=True)
        hit = nidx_f == first
        sel_t = jnp.where(hit, 0.0, sel_t)
        score = jnp.where(hit, -jnp.inf, score)
    sel_neg = sel_t.T.astype(BF16)

    for h in range(HPG):
        m_refs[h][...] = jnp.full((QB, LANES), -jnp.inf, F32)
        l_refs[h][...] = jnp.zeros((QB, LANES), F32)
        acc_refs[h][...] = jnp.zeros((QB, DH), F32)
    KT = SEL_TILE
    sub = KT // QB
    bpt = KT // SLC_LEN
    kk_i = lax.broadcasted_iota(jnp.int32, (QB, KT), 1)
    qq_i = lax.broadcasted_iota(jnp.int32, (QB, KT), 0)
    e_row = lax.broadcasted_iota(jnp.int32, (nsel, KT), 0)
    e_col = lax.broadcasted_iota(jnp.int32, (nsel, KT), 1) // SLC_LEN

    def lane_fold(x, op):
        parts = [x[:, u * LANES:(u + 1) * LANES] for u in range(KT // LANES)]
        while len(parts) > 1:
            parts = [op(parts[a], parts[a + 1]) for a in range(0, len(parts), 2)]
        return parts[0]

    def score_step(t, carry):
        k0 = pl.multiple_of(t * KT, KT)
        k_t = ks_ref[0, 0, pl.ds(k0, KT), :]
        expand = jnp.where(e_row == e_col + t * bpt, 1.0, 0.0).astype(BF16)
        madd = jnp.dot(sel_neg, expand, preferred_element_type=F32)
        madd = jnp.where(j * QB + qq_i - (k0 + kk_i) >= 0, madd, NEG_INF)
        s = lax.dot_general(q, k_t, nt, preferred_element_type=F32)
        for h in range(HPG):
            bias = jnp.concatenate([_dist_tile(bt_ref, h, j - t * sub - u) for u in range(sub)], axis=1)
            sh = s[h * QB:(h + 1) * QB, :] + bias + madd
            sc_ref[t, h * QB:(h + 1) * QB, :] = sh
            m_refs[h][...] = jnp.maximum(m_refs[h][...], lane_fold(sh, jnp.maximum))
        return carry

    def value_step(t, carry):
        k0 = pl.multiple_of(t * KT, KT)
        v_t = vs_ref[0, 0, pl.ds(k0, KT), :]
        for h in range(HPG):
            mb = m_refs[h][...]
            p = jnp.exp(sc_ref[t, h * QB:(h + 1) * QB, :] - jnp.concatenate([mb] * (KT // LANES), axis=1))
            l_refs[h][...] += lane_fold(p, jnp.add)
            acc_refs[h][...] += jnp.dot(p.astype(BF16), v_t, preferred_element_type=F32)
        return carry

    n_tiles = (j * QB) // KT + 1
    lax.fori_loop(0, n_tiles, score_step, 0)
    for h in range(HPG):
        m_refs[h][...] = jnp.broadcast_to(jnp.max(m_refs[h][...], axis=-1, keepdims=True), (QB, LANES))
    lax.fori_loop(0, n_tiles, value_step, 0)
    o_s = [acc_refs[h][...] / jnp.sum(l_refs[h][...], axis=-1, keepdims=True) for h in range(HPG)]

    WK = WINDOW + QB
    w0 = pl.multiple_of(j * QB, QB)
    k_w = kw_ref[0, 0, pl.ds(w0, WK), :]
    v_w = vw_ref[0, 0, pl.ds(w0, WK), :]
    wk_i = lax.broadcasted_iota(jnp.int32, (QB, WK), 1)
    wq_i = lax.broadcasted_iota(jnp.int32, (QB, WK), 0)
    dist_w = wq_i - wk_i + WINDOW
    mask_w = (dist_w >= 0) & (dist_w < WINDOW) & (j * QB - WINDOW + wk_i >= 0)
    madd_w = jnp.where(mask_w, 0.0, NEG_INF)
    s_w = lax.dot_general(q, k_w, nt, preferred_element_type=F32)
    nwt = WK // QB
    o_w = []
    for h in range(HPG):
        bias = jnp.concatenate([_dist_tile(bt_ref, h, nwt - 1 - u) for u in range(nwt)], axis=1)
        sh = s_w[h * QB:(h + 1) * QB, :] + bias + madd_w
        mxw = jnp.max(sh, axis=-1, keepdims=True)
        pw = jnp.exp(sh - mxw)
        lw = jnp.sum(pw, axis=-1, keepdims=True)
        o_w.append(jnp.dot(pw.astype(BF16), v_w, preferred_element_type=F32) / lw)

    gate = jax.nn.sigmoid(gt_ref[...])
    for h in range(HPG):
        gc = gate[:, 8 + 3 * h:9 + 3 * h]
        gs = gate[:, 9 + 3 * h:10 + 3 * h]
        gw = gate[:, 10 + 3 * h:11 + 3 * h]
        y = gc * o_c[h * QB:(h + 1) * QB, :] + gs * o_s[h] + gw * o_w[h]
        o_ref[:, h * DH:(h + 1) * DH] = y.astype(o_ref.dtype)


def nsa_tables(rel_bias, seq):
    def bias_of(dist):
        dist = jnp.maximum(dist, 0)
        max_exact = NUM_BUCKETS // 2
        log_ratio = jnp.log(jnp.maximum(dist, 1).astype(F32) / max_exact) / math.log(MAX_DISTANCE / max_exact)
        large = jnp.minimum(max_exact + (log_ratio * (NUM_BUCKETS - max_exact)).astype(jnp.int32), NUM_BUCKETS - 1)
        bucket = jnp.where(dist < max_exact, dist, large)
        onehot = (bucket[..., None] == jnp.arange(NUM_BUCKETS)).astype(F32)
        out = jnp.dot(onehot, rel_bias.astype(F32), precision=lax.Precision.HIGHEST)
        return jnp.moveaxis(out, -1, 0)

    assert seq // CMP_STRIDE >= LANES and Q_BLOCK * (N_BIAS_TILES - 1) - (Q_BLOCK - 1) >= MAX_DISTANCE
    i = jnp.arange(Q_BLOCK)
    dist = jnp.arange(N_BIAS_TILES)[:, None, None] * Q_BLOCK + i[None, :, None] - i[None, None, :]
    bt = bias_of(dist).reshape(NSA_G, NSA_HPG, N_BIAS_TILES, Q_BLOCK, Q_BLOCK)
    per_tile = Q_BLOCK // CMP_STRIDE
    assert (LANES - per_tile) * CMP_STRIDE - (CMP_LEN - 1) >= MAX_DISTANCE
    dist_c = i[:, None] - ((jnp.arange(LANES)[None, :] - (LANES - per_tile)) * CMP_STRIDE + CMP_LEN - 1)
    rel = bias_of(dist_c).reshape(NSA_G, NSA_HPG, Q_BLOCK, LANES)
    return bt, rel


def nsa_overlap(seq):
    ncp = seq // CMP_STRIDE
    ns = seq // SLC_LEN
    cstart = np.arange(ncp) * CMP_STRIDE
    sstart = np.arange(ns) * SLC_LEN
    ov = np.clip(np.minimum(cstart[:, None] + CMP_LEN, sstart[None, :] + SLC_LEN)
                 - np.maximum(cstart[:, None], sstart[None, :]), 0, None).astype(np.float32) / CMP_LEN
    ov[ncp - 1] = 0.0
    return jnp.asarray(ov.T, BF16)


def nsa_attend(z, pa, pw, cmp, bt, rel, ovt, bsz, seq):
    nqb = seq // Q_BLOCK
    ncp = seq // CMP_STRIDE
    nsel = seq // SLC_LEN
    R = NSA_HPG * Q_BLOCK
    return pl.pallas_call(
        _nsa_kernel,
        grid=(bsz, NSA_G, nqb),
        in_specs=[pl.BlockSpec((1, NSA_HPG, Q_BLOCK, NSA_DH), lambda b, g, j: (b, g, j, 0)),
                  pl.BlockSpec((1, 1, 1, ncp, NSA_DH), lambda b, g, j: (0, b, g, 0, 0)),
                  pl.BlockSpec((1, 1, 1, ncp, NSA_DH), lambda b, g, j: (1, b, g, 0, 0)),
                  pl.BlockSpec((1, NSA_HPG, Q_BLOCK, LANES), lambda b, g, j: (g, 0, 0, 0)),
                  pl.BlockSpec((nsel, ncp), lambda b, g, j: (0, 0)),
                  pl.BlockSpec((1, 1, seq, NSA_DH), lambda b, g, j: (b, 8 + g, 0, 0)),
                  pl.BlockSpec((1, 1, seq, NSA_DH), lambda b, g, j: (b, 10 + g, 0, 0)),
                  pl.BlockSpec((1, 1, seq + WINDOW, NSA_DH), lambda b, g, j: (b, g, 0, 0)),
                  pl.BlockSpec((1, 1, seq + WINDOW, NSA_DH), lambda b, g, j: (b, 2 + g, 0, 0)),
                  pl.BlockSpec((1, NSA_HPG, N_BIAS_TILES, Q_BLOCK, Q_BLOCK), lambda b, g, j: (g, 0, 0, 0, 0)),
                  pl.BlockSpec((Q_BLOCK, LANES), lambda b, g, j: (b * nqb + j, Z_SMALL // LANES + g))],
        out_specs=pl.BlockSpec((Q_BLOCK, NSA_HPG * NSA_DH), lambda b, g, j: (b * nqb + j, g)),
        out_shape=jax.ShapeDtypeStruct((bsz * seq, BRANCH_WIDTH), BF16),
        scratch_shapes=([pltpu.VMEM((seq // SEL_TILE, R, SEL_TILE), F32)]
                        + [pltpu.VMEM((Q_BLOCK, LANES), F32)] * (2 * NSA_HPG)
                        + [pltpu.VMEM((Q_BLOCK, NSA_DH), F32)] * NSA_HPG),
        compiler_params=_cp("arbitrary", "arbitrary", "arbitrary"),
        name="nsa_attend",
    )(pa, cmp, cmp, rel, ovt, pa, pa, pw, pw, bt, z)


def _merge_kernel(ya_ref, yb_ref, yc_ref, w_ref, ga_ref, gb_ref, gc_ref, o_ref):
    acc = jax.nn.sigmoid(ga_ref[...]) * jnp.dot(ya_ref[...], w_ref[0], preferred_element_type=F32)
    acc = acc + jax.nn.sigmoid(gb_ref[...]) * jnp.dot(yb_ref[...], w_ref[1], preferred_element_type=F32)
    acc = acc + jax.nn.sigmoid(gc_ref[...]) * jnp.dot(yc_ref[...], w_ref[2], preferred_element_type=F32)
    o_ref[...] = acc.astype(o_ref.dtype)


def merge(ya, yb, yc, w_branch, z, tm=1024, tn=512):
    n = ya.shape[0]
    d = w_branch.shape[2]
    bw = ya.shape[1]
    ymap = lambda i, j: (i, 0)
    gmap = lambda k: (lambda i, j: (i, (Z_GATE + k * d) // tn + j))
    return pl.pallas_call(
        _merge_kernel,
        grid=(n // tm, d // tn),
        in_specs=[pl.BlockSpec((tm, bw), ymap), pl.BlockSpec((tm, bw), ymap), pl.BlockSpec((tm, bw), ymap),
                  pl.BlockSpec((3, bw, tn), lambda i, j: (0, 0, j)),
                  pl.BlockSpec((tm, tn), gmap(0)), pl.BlockSpec((tm, tn), gmap(1)), pl.BlockSpec((tm, tn), gmap(2))],
        out_specs=pl.BlockSpec((tm, tn), lambda i, j: (i, j)),
        out_shape=jax.ShapeDtypeStruct((n, d), BF16),
        compiler_params=_cp("arbitrary", "arbitrary"),
        name="merge",
    )(ya, yb, yc, w_branch, z, z, z)


def _outproj_kernel(a_ref, w_ref, x_ref, g_ref, o_ref):
    o_ref[...] = x_ref[...] + g_ref[0] * jnp.dot(a_ref[...], w_ref[...], preferred_element_type=F32)


def out_proj(a, w, x2d, gate, seq, tm=1024, tn=512):
    n, k = a.shape
    d = w.shape[1]
    per_b = seq // tm
    return pl.pallas_call(
        _outproj_kernel,
        grid=(n // tm, d // tn),
        in_specs=[pl.BlockSpec((tm, k), lambda i, j: (i, 0)),
                  pl.BlockSpec((k, tn), lambda i, j: (0, j)),
                  pl.BlockSpec((tm, tn), lambda i, j: (i, j)),
                  pl.BlockSpec((1, 1, tn), lambda i, j: (i // per_b, 0, j))],
        out_specs=pl.BlockSpec((tm, tn), lambda i, j: (i, j)),
        out_shape=jax.ShapeDtypeStruct((n, d), F32),
        compiler_params=_cp("arbitrary", "arbitrary"),
        name="out_proj",
    )(a, w, x2d, gate[:, None, :])


def _pack_bf16_pairs(y):
    half = y.shape[1] // 2
    lo = lax.bitcast_convert_type(y[:, :half].astype(BF16).astype(F32), jnp.uint32)
    hi = lax.bitcast_convert_type(y[:, half:].astype(BF16).astype(F32), jnp.uint32)
    return lax.shift_right_logical(lo, jnp.uint32(16)) | (hi & jnp.uint32(0xFFFF0000))


def _unpack_bf16_pairs(w):
    lo = lax.bitcast_convert_type(lax.shift_left(w, jnp.uint32(16)), F32)
    hi = lax.bitcast_convert_type(w & jnp.uint32(0xFFFF0000), F32)
    return jnp.concatenate([lo, hi], axis=1)


def _router_kernel(x_ref, g_ref, sc_ref, sh_ref, rw_ref, rwl_ref, rb_ref, h_ref, idx_ref, p_ref):
    x = x_ref[...]
    y = x * lax.rsqrt(jnp.mean(x * x, axis=-1, keepdims=True) + EPS)
    hf = (y * g_ref[...]) * (1.0 + sc_ref[0]) + sh_ref[0]
    h = hf.astype(BF16)
    h_ref[...] = _pack_bf16_pairs(hf)
    h_lo = (hf - h.astype(F32)).astype(BF16)
    logits = (jnp.dot(h, rw_ref[...], preferred_element_type=F32)
              + jnp.dot(h_lo, rw_ref[...], preferred_element_type=F32)
              + jnp.dot(h, rwl_ref[...], preferred_element_type=F32)) + rb_ref[...]
    lane = lax.broadcasted_iota(jnp.int32, logits.shape, 1)
    lane_f = lane.astype(F32)
    idx_out = jnp.zeros(logits.shape, F32)
    val_out = jnp.full(logits.shape, -jnp.inf, F32)
    for k in range(TOP_K):
        best = jnp.max(logits, axis=-1, keepdims=True)
        first = jnp.min(jnp.where(logits == best, lane_f, float(LANES)), axis=-1, keepdims=True)
        idx_out = jnp.where(lane == k, first, idx_out)
        val_out = jnp.where(lane == k, best, val_out)
        logits = jnp.where(lane_f == first, -jnp.inf, logits)
    e = jnp.exp(val_out - jnp.max(val_out, axis=-1, keepdims=True))
    idx_ref[...] = idx_out.astype(jnp.int32)
    p_ref[...] = e / jnp.sum(e, axis=-1, keepdims=True)


def router(x2d, g, sc, sh, rw, rb, seq, tm=1024):
    n, d = x2d.shape
    per_b = seq // tm
    rw_hi = rw.astype(BF16)
    rw_lo = (rw - rw_hi.astype(F32)).astype(BF16)
    rwp = jnp.zeros((d, LANES), BF16).at[:, :N_EXPERTS].set(rw_hi)
    rwl = jnp.zeros((d, LANES), BF16).at[:, :N_EXPERTS].set(rw_lo)
    rbp = jnp.full((1, LANES), -jnp.inf, F32).at[0, :N_EXPERTS].set(rb)
    return pl.pallas_call(
        _router_kernel,
        grid=(n // tm,),
        in_specs=[pl.BlockSpec((tm, d), lambda i: (i, 0)),
                  pl.BlockSpec((1, d), lambda i: (0, 0)),
                  pl.BlockSpec((1, 1, d), lambda i: (i // per_b, 0, 0)),
                  pl.BlockSpec((1, 1, d), lambda i: (i // per_b, 0, 0)),
                  pl.BlockSpec((d, LANES), lambda i: (0, 0)),
                  pl.BlockSpec((d, LANES), lambda i: (0, 0)),
                  pl.BlockSpec((1, LANES), lambda i: (0, 0))],
        out_specs=[pl.BlockSpec((tm, d // 2), lambda i: (i, 0)),
                   pl.BlockSpec((tm, LANES), lambda i: (i, 0)),
                   pl.BlockSpec((tm, LANES), lambda i: (i, 0))],
        out_shape=[jax.ShapeDtypeStruct((n, d // 2), jnp.uint32),
                   jax.ShapeDtypeStruct((n, LANES), jnp.int32),
                   jax.ShapeDtypeStruct((n, LANES), F32)],
        compiler_params=_cp("arbitrary"),
        name="router",
    )(x2d, g.reshape(1, d), sc[:, None, :], sh[:, None, :], rwp, rwl, rbp)


def _dispatch_kernel(dest_ref, h_ref, init_ref, xb_ref, sem):
    del init_ref
    tm = h_ref.shape[0]

    def row_copy(k, r):
        dst = dest_ref[0, 0, k * tm + r]
        return pltpu.make_async_copy(h_ref.at[pl.ds(r, 1), :], xb_ref.at[pl.ds(dst, 1), :], sem)

    def start(r, carry):
        for k in range(TOP_K):
            row_copy(k, r).start()
        return carry

    def wait(r, carry):
        for k in range(TOP_K):
            row_copy(k, r).wait()
        return carry

    lax.fori_loop(0, tm, start, 0, unroll=8)
    lax.fori_loop(0, tm, wait, 0, unroll=8)


def dispatch(hp, dest_tiles, rows):
    n, w = hp.shape
    nb, _, per_tile = dest_tiles.shape
    tm = per_tile // TOP_K
    return pl.pallas_call(
        _dispatch_kernel,
        grid=(nb,),
        in_specs=[pl.BlockSpec((1, 1, per_tile), lambda i: (i, 0, 0), memory_space=pltpu.SMEM),
                  pl.BlockSpec((tm, w), lambda i: (i, 0)),
                  pl.BlockSpec(memory_space=pl.ANY)],
        out_specs=pl.BlockSpec(memory_space=pl.ANY),
        out_shape=jax.ShapeDtypeStruct((rows, w), jnp.uint32),
        scratch_shapes=[pltpu.SemaphoreType.DMA(())],
        input_output_aliases={2: 0},
        compiler_params=_cp("arbitrary"),
        name="dispatch",
    )(dest_tiles, hp, jnp.zeros((rows, w), jnp.uint32))


def _expert_kernel(be_ref, nv_ref, x_ref, wg_ref, wu_ref, bg_ref, bu_ref, w2_ref, b2_ref, o_ref, acc_ref, xs_ref):
    i = pl.program_id(0)
    f = pl.program_id(1)

    @pl.when(i < nv_ref[0])
    def _():
        @pl.when(f == 0)
        def _():
            acc_ref[...] = jnp.zeros(acc_ref.shape, F32)
            xs_ref[...] = _unpack_bf16_pairs(x_ref[...]).astype(BF16)

        x = xs_ref[...]
        gate = jnp.dot(x, wg_ref[0, 0].astype(BF16), preferred_element_type=F32) + bg_ref[0, 0]
        up = jnp.dot(x, wu_ref[0, 0].astype(BF16), preferred_element_type=F32) + bu_ref[0, 0]
        gate = jnp.minimum(gate, SWIGLU_LIMIT)
        up = jnp.clip(up, -SWIGLU_LIMIT, SWIGLU_LIMIT)
        act = (up + 1.0) * (gate * jax.nn.sigmoid(SWIGLU_ALPHA * gate))
        acc_ref[...] += jnp.dot(act.astype(BF16), w2_ref[0, 0].astype(BF16), preferred_element_type=F32)

        @pl.when(f == pl.num_programs(1) - 1)
        def _():
            o_ref[...] = _pack_bf16_pairs(acc_ref[...] + b2_ref[0, 0])

    @pl.when(i >= nv_ref[0])
    def _():
        o_ref[...] = jnp.zeros(o_ref.shape, o_ref.dtype)


def experts(xb, block_exp, n_valid, w1, b1, w2, b2, layer, tm, tf=256):
    rows = xb.shape[0]
    d = 2 * xb.shape[1]
    nf = D_FF // tf
    nblk = rows // tm
    n_exp = w1.shape[1]
    fe = lambda i, f, nv: jnp.where(i < nv[0], f, nf - 1)
    grid_spec = pltpu.PrefetchScalarGridSpec(
        num_scalar_prefetch=2,
        grid=(nblk, nf),
        in_specs=[pl.BlockSpec((tm, d // 2), lambda i, f, be, nv: (jnp.minimum(i, nv[0] - 1), 0)),
                  pl.BlockSpec((1, 1, d, tf), lambda i, f, be, nv: (layer, be[i], 0, fe(i, f, nv))),
                  pl.BlockSpec((1, 1, d, tf), lambda i, f, be, nv: (layer, be[i], 0, nf + fe(i, f, nv))),
                  pl.BlockSpec((1, 1, 1, tf), lambda i, f, be, nv: (layer * n_exp + be[i], 0, 0, fe(i, f, nv))),
                  pl.BlockSpec((1, 1, 1, tf), lambda i, f, be, nv: (layer * n_exp + be[i], 0, 0, nf + fe(i, f, nv))),
                  pl.BlockSpec((1, 1, tf, d), lambda i, f, be, nv: (layer, be[i], fe(i, f, nv), 0)),
                  pl.BlockSpec((1, 1, 1, d), lambda i, f, be, nv: (layer * n_exp + be[i], 0, 0, 0))],
        out_specs=pl.BlockSpec((tm, d // 2), lambda i, f, be, nv: (i, 0)),
        scratch_shapes=[pltpu.VMEM((tm, d), F32), pltpu.VMEM((tm, d), BF16)],
    )
    b1r = b1.reshape(-1, 1, 1, b1.shape[-1])
    b2r = b2.reshape(-1, 1, 1, d)
    return pl.pallas_call(
        _expert_kernel,
        grid_spec=grid_spec,
        out_shape=jax.ShapeDtypeStruct((rows, d // 2), jnp.uint32),
        compiler_params=_cp("arbitrary", "arbitrary"),
        name="experts",
    )(block_exp, n_valid, xb, w1, w1, b1r, b1r, w2, b2r)


def _combine_kernel(dest_ref, p_ref, x_ref, g_ref, yb_ref, o_ref, ybuf_ref, sem):
    tm = x_ref.shape[0]

    def row_copy(k, r):
        src = dest_ref[0, 0, k * tm + r]
        return pltpu.make_async_copy(yb_ref.at[pl.ds(src, 1), :], ybuf_ref.at[k, pl.ds(r, 1), :], sem)

    def start(r, carry):
        for k in range(TOP_K):
            row_copy(k, r).start()
        return carry

    def wait(r, carry):
        for k in range(TOP_K):
            row_copy(k, r).wait()
        return carry

    lax.fori_loop(0, tm, start, 0, unroll=8)
    lax.fori_loop(0, tm, wait, 0, unroll=8)
    p = p_ref[...]
    y = p[:, 0:1] * _unpack_bf16_pairs(ybuf_ref[0])
    for k in range(1, TOP_K):
        y = y + p[:, k:k + 1] * _unpack_bf16_pairs(ybuf_ref[k])
    o_ref[...] = x_ref[...] + g_ref[0] * y


def combine(yb, dest_tiles, prob, x2d, gate, seq):
    n, d = x2d.shape
    nb = dest_tiles.shape[0]
    tm = n // nb
    per_b = seq // tm
    return pl.pallas_call(
        _combine_kernel,
        grid=(nb,),
        in_specs=[pl.BlockSpec((1, 1, TOP_K * tm), lambda i: (i, 0, 0), memory_space=pltpu.SMEM),
                  pl.BlockSpec((tm, LANES), lambda i: (i, 0)),
                  pl.BlockSpec((tm, d), lambda i: (i, 0)),
                  pl.BlockSpec((1, 1, d), lambda i: (i // per_b, 0, 0)),
                  pl.BlockSpec(memory_space=pl.ANY)],
        out_specs=pl.BlockSpec((tm, d), lambda i: (i, 0)),
        out_shape=jax.ShapeDtypeStruct((n, d), F32),
        scratch_shapes=[pltpu.VMEM((TOP_K, tm, d // 2), jnp.uint32), pltpu.SemaphoreType.DMA(())],
        compiler_params=_cp("arbitrary"),
        name="combine",
    )(dest_tiles, prob, x2d, gate[:, None, :], yb)


def moe(x2d, g, sc, sh, gate, rw, rb, w1, b1, w2, b2, layer, seq, tm_e=1024, tm_t=512):
    n, d = x2d.shape
    h, idx, prob = router(x2d, g, sc, sh, rw, rb, seq)
    e_flat = idx[:, :TOP_K].T.reshape(-1)
    na = n * TOP_K
    onehot = (e_flat[:, None] == jnp.arange(N_EXPERTS)[None, :]).astype(jnp.int32)
    csum = jnp.cumsum(onehot, axis=0)
    counts = csum[-1]
    rank = jnp.take_along_axis(csum, e_flat[:, None], axis=1)[:, 0] - 1
    padded = (counts + tm_e - 1) // tm_e * tm_e
    pend = jnp.cumsum(padded)
    pstart = pend - padded
    dest = pstart[e_flat] + rank
    nblk = na // tm_e + N_EXPERTS
    rows = nblk * tm_e
    starts = jnp.arange(nblk, dtype=jnp.int32) * tm_e
    block_exp = jnp.minimum(jnp.sum(pend[None, :] <= starts[:, None], axis=1), N_EXPERTS - 1).astype(jnp.int32)
    n_valid = (pend[-1:] // tm_e).astype(jnp.int32)
    block_exp = jnp.where(starts < pend[-1], block_exp, block_exp[jnp.maximum(n_valid[0] - 1, 0)])
    nb = n // tm_t
    dest_tiles = dest.reshape(TOP_K, nb, tm_t).transpose(1, 0, 2).reshape(nb, 1, TOP_K * tm_t)
    xb = dispatch(h, dest_tiles, rows)
    yb = experts(xb, block_exp, n_valid, w1, b1, w2, b2, layer, tm_e)
    return combine(yb, dest_tiles, prob, x2d, gate, seq)


def fused_in_weight(w_in, w_gate):
    sizes = (512, 512, 1024, 4, 4, 1024, 1024, 256, 256, 256, 256, 256, 256, 24, 1024, 1024)
    pts = np.cumsum(sizes)[:-1].tolist()
    (qm, km, vm, ig, fg, og, qn, kc, vc, ks, vs, kw, vw, gn, gu, gv) = jnp.split(w_in, pts, axis=1)
    d = w_in.shape[0]
    pad = lambda k: jnp.zeros((d, k), w_in.dtype)
    per_group = NSA_HPG * 3
    small0 = jnp.concatenate([ig, fg, gn[:, :per_group], pad(LANES - 8 - per_group)], axis=1)
    small1 = jnp.concatenate([pad(8), gn[:, per_group:], pad(LANES - 8 - per_group)], axis=1)
    cols = [qm, km, vm, og, qn, gu, gv, kc, vc, ks, vs, kw, vw, w_gate[0], w_gate[1], w_gate[2], small0, small1]
    w = jnp.concatenate(cols, axis=1)
    w = jnp.concatenate([w, pad(Z_WIDTH - w.shape[1])], axis=1)
    return w.astype(BF16)


def kernel(x, c, ada_w, ada_b, norm1_g, norm2_g, w_in, conv_w, conv_b, mlstm_gate_b, mlstm_norm_g, cmp_pos,
           cmp_k_w1, cmp_k_w2, cmp_v_w1, cmp_v_w2, qnorm_g, knorm_g, rel_bias, gmlp_norm_g, gmlp_ws, gmlp_b,
           w_branch, w_gate, w_out, router_w, router_b, exp_w1, exp_b1, exp_w2, exp_b2):
    bsz, seq, d = x.shape
    depth = ada_w.shape[0]
    x2d = x.reshape(bsz * seq, d)
    bt, rel = nsa_tables(rel_bias, seq)
    ovt = nsa_overlap(seq)
    for l in range(depth):
        mod = ada_mod(c, ada_w, ada_b, l)
        sh1, sc1, g1, sh2, sc2, g2 = jnp.split(mod, 6, axis=-1)
        z = in_proj(x2d, norm1_g[l], sc1, sh1, fused_in_weight(w_in[l], w_gate[l]), seq)
        ya = mlstm(z, bsz, seq, conv_w[l], conv_b[l], mlstm_gate_b[l], mlstm_norm_g[l])
        pa, pw = nsa_prep(z, bsz, seq, qnorm_g[l], knorm_g[l])
        cmp = nsa_compress(z, bsz, seq, cmp_pos[l], cmp_k_w1[l], cmp_k_w2[l], cmp_v_w1[l], cmp_v_w2[l], knorm_g[l])
        yb = nsa_attend(z, pa, pw, cmp, bt, rel, ovt, bsz, seq)
        yc = gmlp(z, gmlp_norm_g[l], gmlp_ws[l], gmlp_b[l])
        merged = merge(ya, yb, yc, w_branch[l].astype(BF16), z)
        x2d = out_proj(merged, w_out[l].astype(BF16), x2d, g1, seq)
        x2d = moe(x2d, norm2_g[l], sc2, sh2, g2, router_w[l], router_b[l], exp_w1, exp_b1, exp_w2, exp_b2, l, seq)
    return x2d.reshape(bsz, seq, d)
```

```python
import functools
import math

import jax
import jax.numpy as jnp
import numpy as np
from jax import lax
from jax.experimental import pallas as pl
from jax.experimental.pallas import tpu as pltpu

F32 = jnp.float32
BF16 = jnp.bfloat16

LANES = 128
VMEM_LIMIT = 56 * 1024 * 1024

D_MODEL = 2048
BRANCH_WIDTH = 1024
MLSTM_HEADS = 4
MLSTM_DV = 256
MLSTM_DQK = 128
MLSTM_TILE = 256
CONV_WIDTH = 4
NSA_HEADS = 8
NSA_G = 2
NSA_HPG = 4
NSA_DH = 128
CMP_LEN = 32
CMP_STRIDE = 16
SLC_LEN = 64
NSA_N_SELECT = 16
WINDOW = 512
Q_BLOCK = 128
NUM_BUCKETS = 32
MAX_DISTANCE = 1024
GMLP_CHUNK = 128
GMLP_GROUPS = 8
N_EXPERTS = 32
TOP_K = 4
D_FF = 1536
SWIGLU_LIMIT = 7.0
SWIGLU_ALPHA = 1.702
NEG_INF = -1e30
FORCE_SELECT = 1e4
EPS = 1e-6

LOG2E = math.log2(math.e)
SEL_TILE = 512
N_DIST_TILES = 10
TAB_MASKED, TAB_WINDOW_FAR, N_TAB = 0, N_DIST_TILES + 1, N_DIST_TILES + 2

Z_QK, Z_V, Z_OG, Z_NQ, Z_GU, Z_GV = 0, 1024, 2048, 3072, 4096, 5120
Z_KV = 6144
Z_GATE = 7680
Z_SMALL = 13824
Z_WIDTH = 14336


def _cp(*sem):
    return pltpu.CompilerParams(dimension_semantics=sem, vmem_limit_bytes=VMEM_LIMIT)


def _ada_kernel(c_ref, w_ref, b_ref, o_ref):
    a = jax.nn.silu(c_ref[...]).astype(BF16)
    o_ref[...] = jnp.dot(a, w_ref[0].astype(BF16), preferred_element_type=F32) + b_ref[0]


def ada_mod(c, w, b, layer):
    bsz, d = c.shape
    n = w.shape[2]
    tn = 1024
    cp = jnp.zeros((8, d), F32).at[:bsz].set(c)
    out = pl.pallas_call(
        _ada_kernel,
        grid=(n // tn,),
        in_specs=[pl.BlockSpec((8, d), lambda j: (0, 0)),
                  pl.BlockSpec((1, d, tn), lambda j: (layer, 0, j)),
                  pl.BlockSpec((1, 1, tn), lambda j: (layer, 0, j))],
        out_specs=pl.BlockSpec((8, tn), lambda j: (0, j)),
        out_shape=jax.ShapeDtypeStruct((8, n), F32),
        compiler_params=_cp("arbitrary"),
        name="ada_mod",
    )(cp, w, b[:, None, :])
    return out[:bsz]


def _in_kernel(x_ref, g_ref, sc_ref, sh_ref, w_ref, o_ref, h_ref):
    @pl.when(pl.program_id(1) == 0)
    def _():
        x = x_ref[...]
        y = x * lax.rsqrt(jnp.mean(x * x, axis=-1, keepdims=True) + EPS)
        h = (y * g_ref[...]) * (1.0 + sc_ref[0]) + sh_ref[0]
        h_ref[...] = h.astype(BF16)

    o_ref[...] = jnp.dot(h_ref[...], w_ref[...], preferred_element_type=F32)


def in_proj(x2d, g, sc, sh, wcat, seq, tm=1024, tn=512):
    n, d = x2d.shape
    w = wcat.shape[1]
    per_b = seq // tm
    return pl.pallas_call(
        _in_kernel,
        grid=(n // tm, w // tn),
        in_specs=[pl.BlockSpec((tm, d), lambda i, j: (i, 0)),
                  pl.BlockSpec((1, d), lambda i, j: (0, 0)),
                  pl.BlockSpec((1, 1, d), lambda i, j: (i // per_b, 0, 0)),
                  pl.BlockSpec((1, 1, d), lambda i, j: (i // per_b, 0, 0)),
                  pl.BlockSpec((d, tn), lambda i, j: (0, j))],
        out_specs=pl.BlockSpec((tm, tn), lambda i, j: (i, j)),
        out_shape=jax.ShapeDtypeStruct((n, w), F32),
        scratch_shapes=[pltpu.VMEM((tm, d), BF16)],
        compiler_params=_cp("arbitrary", "arbitrary"),
        name="in_proj",
    )(x2d, g.reshape(1, d), sc[:, None, :], sh[:, None, :], wcat)


def _cumsum_rows(x, n):
    row = lax.broadcasted_iota(jnp.int32, x.shape, 0)
    s = 1
    while s < n:
        x = x + jnp.where(row >= s, pltpu.roll(x, s, 0), 0.0)
        s *= 2
    return x


def _mlstm_kernel(qk_ref, v_ref, og_ref, sm_ref, cw_ref, cb_ref, gb_ref, ng_ref, o_ref,
                  buf_ref, ct_ref, n_ref, m_ref):
    L = qk_ref.shape[0]
    H, DQK, DV = MLSTM_HEADS, MLSTM_DQK, MLSTM_DV

    @pl.when(pl.program_id(1) == 0)
    def _():
        buf_ref[0:8, :] = jnp.zeros((8, buf_ref.shape[1]), F32)
        ct_ref[...] = jnp.zeros(ct_ref.shape, F32)
        n_ref[...] = jnp.zeros(n_ref.shape, F32)
        m_ref[...] = jnp.full(m_ref.shape, NEG_INF, F32)

    buf_ref[8:8 + L, :] = qk_ref[...]
    conv = cb_ref[...] + cw_ref[3:4, :] * buf_ref[8:8 + L, :]
    for j in range(CONV_WIDTH - 1):
        conv = conv + cw_ref[j:j + 1, :] * buf_ref[5 + j:5 + j + L, :]
    qk = jax.nn.silu(conv)
    buf_ref[0:8, :] = buf_ref[L:L + 8, :]

    sm = sm_ref[...] + gb_ref[...]
    bcum = _cumsum_rows(jax.nn.log_sigmoid(sm), L)
    sm_t = sm.T
    bcum_t = bcum.T
    row = lax.broadcasted_iota(jnp.int32, (L, L), 0)
    col = lax.broadcasted_iota(jnp.int32, (L, L), 1)
    causal = row >= col

    for h in range(H):
        qf = qk[:, h * DQK:(h + 1) * DQK] * DQK ** -0.5
        q = qf.astype(BF16)
        kf = qk[:, (H + h) * DQK:(H + h + 1) * DQK]
        k = kf.astype(BF16)
        v = v_ref[:, h * DV:(h + 1) * DV].astype(BF16)
        b_col = bcum[:, H + h:H + h + 1]
        i_col = sm[:, h:h + 1]
        b_row = bcum_t[H + h:H + h + 1, :]
        i_row = sm_t[h:h + 1, :]
        m_prev = m_ref[h:h + 1, 0:1]
        a = b_col + m_prev
        dmat = jnp.where(causal, b_col - b_row + i_row, -jnp.inf)
        m_t = jnp.maximum(a, jnp.max(dmat, axis=1, keepdims=True))
        w_intra = jnp.exp(dmat - m_t)
        w_inter = jnp.exp(a - m_t)
        s = lax.dot_general(q, k, (((1,), (1,)), ((), ())), preferred_element_type=F32) * w_intra
        ct = ct_ref[h]
        nvec = n_ref[h:h + 1, :]
        num = jnp.dot(s.astype(BF16), v, preferred_element_type=F32) + w_inter * jnp.dot(
            q, ct.astype(BF16), preferred_element_type=F32)
        qn = jnp.sum(qf * nvec, axis=1, keepdims=True)
        den = jnp.sum(s, axis=1, keepdims=True) + w_inter * qn
        hm = num / jnp.maximum(jnp.abs(den), jnp.exp(-m_t))
        b_last = b_col[L - 1:L, :]
        gdec = b_last - b_col + i_col
        m_new = jnp.maximum(b_last + m_prev, jnp.max(gdec, axis=0, keepdims=True))
        ws = jnp.exp(gdec - m_new)
        wc = jnp.exp(b_last + m_prev - m_new)
        kw = ws * kf
        ct_ref[h] = wc * ct + jnp.dot(kw.T.astype(BF16), v, preferred_element_type=F32)
        n_ref[h:h + 1, :] = wc * nvec + jnp.sum(kw, axis=0, keepdims=True)
        m_ref[h:h + 1, :] = jnp.broadcast_to(m_new, (1, LANES))
        y = hm * lax.rsqrt(jnp.mean(hm * hm, axis=-1, keepdims=True) + EPS) * ng_ref[:, h * DV:(h + 1) * DV]
        og = og_ref[:, h * DV:(h + 1) * DV]
        o_ref[:, h * DV:(h + 1) * DV] = (jax.nn.sigmoid(og) * y).astype(o_ref.dtype)


def mlstm(z, bsz, seq, conv_w, conv_b, gate_b, norm_g):
    L = MLSTM_TILE
    nch = seq // L
    gb = jnp.zeros((1, LANES), F32).at[0, :2 * MLSTM_HEADS].set(gate_b)
    wide = BRANCH_WIDTH
    rowmap = lambda col: (lambda b, c: (b * nch + c, col))
    const = lambda b, c: (0, 0)
    return pl.pallas_call(
        _mlstm_kernel,
        grid=(bsz, nch),
        in_specs=[pl.BlockSpec((L, wide), rowmap(Z_QK // wide)),
                  pl.BlockSpec((L, wide), rowmap(Z_V // wide)),
                  pl.BlockSpec((L, wide), rowmap(Z_OG // wide)),
                  pl.BlockSpec((L, LANES), rowmap(Z_SMALL // LANES)),
                  pl.BlockSpec((CONV_WIDTH, wide), const),
                  pl.BlockSpec((1, wide), const),
                  pl.BlockSpec((1, LANES), const),
                  pl.BlockSpec((1, wide), const)],
        out_specs=pl.BlockSpec((L, wide), lambda b, c: (b * nch + c, 0)),
        out_shape=jax.ShapeDtypeStruct((bsz * seq, wide), BF16),
        scratch_shapes=[pltpu.VMEM((L + 8, wide), F32),
                        pltpu.VMEM((MLSTM_HEADS, MLSTM_DQK, MLSTM_DV), F32),
                        pltpu.VMEM((8, MLSTM_DQK), F32),
                        pltpu.VMEM((8, LANES), F32)],
        compiler_params=_cp("arbitrary", "arbitrary"),
        name="mlstm",
    )(z, z, z, z, conv_w, conv_b.reshape(1, wide), gb, norm_g.reshape(1, wide))


def _gmlp_kernel(u_ref, v_ref, ng_ref, ws_ref, b_ref, o_ref):
    T = u_ref.shape[0]
    C = GMLP_CHUNK
    u = jax.nn.gelu(u_ref[...])
    vg = jax.nn.gelu(v_ref[...])
    mu = jnp.mean(vg, axis=-1, keepdims=True)
    var = jnp.mean(jnp.square(vg - mu), axis=-1, keepdims=True)
    v = ((vg - mu) * lax.rsqrt(var + EPS) * ng_ref[...]).astype(BF16)
    row = lax.broadcasted_iota(jnp.int32, (C, C), 0)
    col = lax.broadcasted_iota(jnp.int32, (C, C), 1)
    for g in range(GMLP_GROUPS):
        wsc = jnp.where(row >= col, ws_ref[g], 0.0).astype(BF16)
        bcol = b_ref[:, g:g + 1]
        for n in range(T // C):
            vv = v[n * C:(n + 1) * C, g * C:(g + 1) * C]
            s = jnp.dot(wsc, vv, preferred_element_type=F32) + bcol
            o_ref[n * C:(n + 1) * C, g * C:(g + 1) * C] = (
                u[n * C:(n + 1) * C, g * C:(g + 1) * C] * s).astype(o_ref.dtype)


def gmlp(z, norm_g, ws, b, tg=512):
    n = z.shape[0]
    wide = BRANCH_WIDTH
    return pl.pallas_call(
        _gmlp_kernel,
        grid=(n // tg,),
        in_specs=[pl.BlockSpec((tg, wide), lambda i: (i, Z_GU // wide)),
                  pl.BlockSpec((tg, wide), lambda i: (i, Z_GV // wide)),
                  pl.BlockSpec((1, wide), lambda i: (0, 0)),
                  pl.BlockSpec((GMLP_GROUPS, GMLP_CHUNK, GMLP_CHUNK), lambda i: (0, 0, 0)),
                  pl.BlockSpec((GMLP_CHUNK, GMLP_GROUPS), lambda i: (0, 0))],
        out_specs=pl.BlockSpec((tg, wide), lambda i: (i, 0)),
        out_shape=jax.ShapeDtypeStruct((n, wide), BF16),
        compiler_params=_cp("arbitrary"),
        name="gmlp",
    )(z, z, norm_g.reshape(1, wide), ws, b.T)


def _prep_kernel(n_norm, npad, x_ref, g_ref, o_ref):
    nh = o_ref.shape[1]

    @pl.when(pl.program_id(1) < npad)
    def _():
        o_ref[...] = jnp.zeros(o_ref.shape, o_ref.dtype)

    @pl.when(pl.program_id(1) >= npad)
    def _():
        for h in range(nh):
            x = x_ref[:, h * LANES:(h + 1) * LANES]
            if h < n_norm:
                x = x * lax.rsqrt(jnp.mean(x * x, axis=-1, keepdims=True) + EPS) * g_ref[h]
            o_ref[0, h] = x.astype(o_ref.dtype)


def _prep_call(z, bsz, seq, col0, heads, n_norm, gains, pad, ts, name):
    width = heads * LANES
    nblk = seq // ts
    npad = pad // ts
    return pl.pallas_call(
        functools.partial(_prep_kernel, n_norm, npad),
        grid=(bsz, nblk + npad),
        in_specs=[pl.BlockSpec((ts, width), lambda b, s: (b * nblk + jnp.maximum(s - npad, 0), col0 // width)),
                  pl.BlockSpec((heads, 1, LANES), lambda b, s: (0, 0, 0))],
        out_specs=pl.BlockSpec((1, heads, ts, LANES), lambda b, s: (b, 0, s, 0)),
        out_shape=jax.ShapeDtypeStruct((bsz, heads, seq + pad, LANES), BF16),
        compiler_params=_cp("arbitrary", "arbitrary"),
        name=name,
    )(z, gains)


def nsa_prep(z, bsz, seq, qnorm_g, knorm_g):
    gq = jnp.tile((qnorm_g * (NSA_DH ** -0.5 * LOG2E)).reshape(1, 1, LANES), (NSA_HEADS, 1, 1))
    gk = jnp.tile(knorm_g.reshape(1, 1, LANES), (2 * NSA_G, 1, 1))
    pq = _prep_call(z, bsz, seq, Z_NQ, NSA_HEADS, NSA_HEADS, gq, 0, 1024, "nsa_prep_q")
    pkv = _prep_call(z, bsz, seq, Z_KV + 512, 2 * NSA_G, NSA_G, gk, 0, 1024, "nsa_prep_sel")
    pw = _prep_call(z, bsz, seq, Z_KV + 1024, 2 * NSA_G, NSA_G, gk, WINDOW, WINDOW, "nsa_prep_win")
    return pq, pkv, pw


def _compress_kernel(x_ref, pos_ref, w1_ref, w2_ref, g_ref, o_ref):
    half = x_ref.shape[-1]
    x = x_ref[0, 0, 0]
    lo = jnp.dot((x + pos_ref[0, :, 0:half]).astype(BF16), w1_ref[0, 0:half, :], preferred_element_type=F32)
    hi = jnp.dot((x + pos_ref[0, :, half:2 * half]).astype(BF16), w1_ref[0, half:2 * half, :],
                 preferred_element_type=F32)
    ncp = x.shape[0]
    pre = lo + pltpu.roll(hi, ncp - 1, 0)
    y = jnp.dot(jax.nn.gelu(pre).astype(BF16), w2_ref[0], preferred_element_type=F32)
    yn = y * lax.rsqrt(jnp.mean(y * y, axis=-1, keepdims=True) + EPS) * g_ref[...]
    o_ref[0, 0, 0] = jnp.where(pl.program_id(0) == 0, yn, y).astype(o_ref.dtype)


def nsa_compress(z, bsz, seq, cmp_pos, w1k, w2k, w1v, w2v, knorm_g):
    ncp = seq // CMP_STRIDE
    kv = z[:, Z_KV:Z_KV + 512].reshape(bsz, seq, 2, NSA_G, NSA_DH)
    kv = kv.transpose(2, 0, 3, 1, 4).reshape(2, bsz, NSA_G, ncp, CMP_STRIDE * NSA_DH)
    pos = cmp_pos.reshape(1, 1, CMP_LEN * NSA_DH)
    w1 = jnp.stack([w1k, w1v]).astype(BF16)
    w2 = jnp.stack([w2k, w2v]).astype(BF16)
    kdim = CMP_STRIDE * NSA_DH
    return pl.pallas_call(
        _compress_kernel,
        grid=(2, bsz, NSA_G),
        in_specs=[pl.BlockSpec((1, 1, 1, ncp, kdim), lambda t, b, g: (t, b, g, 0, 0)),
                  pl.BlockSpec((1, 1, 2 * kdim), lambda t, b, g: (0, 0, 0)),
                  pl.BlockSpec((1, 2 * kdim, NSA_DH), lambda t, b, g: (t, 0, 0)),
                  pl.BlockSpec((1, NSA_DH, NSA_DH), lambda t, b, g: (t, 0, 0)),
                  pl.BlockSpec((1, NSA_DH), lambda t, b, g: (0, 0))],
        out_specs=pl.BlockSpec((1, 1, 1, ncp, NSA_DH), lambda t, b, g: (t, b, g, 0, 0)),
        out_shape=jax.ShapeDtypeStruct((2, bsz, NSA_G, ncp, NSA_DH), BF16),
        compiler_params=_cp("arbitrary", "arbitrary", "arbitrary"),
        name="nsa_compress",
    )(kv, pos, w1, w2, knorm_g.reshape(1, NSA_DH))


def _dist_tile(bt_ref, h, rho):
    return bt_ref[0, h, jnp.clip(rho, -1, N_DIST_TILES - 1) + 1]


def _nsa_kernel(q_ref, kc_ref, vc_ref, rel_ref, ovt_ref, ks_ref, vs_ref, et_ref, kw_ref, vw_ref, bt_ref, gt_ref,
                o_ref, sc_ref, *state):
    m_refs, l_refs, acc_refs = state[0:NSA_HPG], state[NSA_HPG:2 * NSA_HPG], state[2 * NSA_HPG:3 * NSA_HPG]
    j = pl.program_id(2)
    QB, HPG, DH = Q_BLOCK, NSA_HPG, NSA_DH
    R = HPG * QB
    ncp = kc_ref.shape[3]
    nsel = ovt_ref.shape[0]
    q = q_ref[0].reshape(R, DH)
    nt = (((1,), (1,)), ((), ()))
    qi = lax.broadcasted_iota(jnp.int32, (QB, 1), 0)
    tpos = j * QB + qi

    cidx = lax.broadcasted_iota(jnp.int32, (QB, ncp), 1)
    mask_c = jnp.logical_and(cidx * CMP_STRIDE + (CMP_LEN - 1) <= tpos, cidx < ncp - 1)
    band0 = (j * (QB // CMP_STRIDE) + ncp - (LANES - QB // CMP_STRIDE)) % ncp
    s_c = lax.dot_general(q, kc_ref[0, 0, 0], nt, preferred_element_type=F32)
    p_c = []
    for h in range(HPG):
        sat = bt_ref[0, h, N_DIST_TILES]
        bias = jnp.concatenate([rel_ref[0, h]] + [sat] * (ncp // LANES - 1), axis=1)
        bias = pltpu.roll(bias, band0, 1)
        sh = jnp.where(mask_c, s_c[h * QB:(h + 1) * QB, :] + bias, NEG_INF)
        mx = jnp.max(sh, axis=-1, keepdims=True)
        e = jnp.where(mask_c, jnp.exp2(sh - mx), 0.0)
        den = jnp.sum(e, axis=-1, keepdims=True)
        p_c.append(e * jnp.where(den > 0.0, 1.0 / den, 0.0))
    o_c = jnp.dot(jnp.concatenate(p_c, axis=0).astype(BF16), vc_ref[0, 0, 0], preferred_element_type=F32)

    psum = p_c[0] + p_c[1] + p_c[2] + p_c[3]
    p_hi = psum.astype(BF16)
    p_lo = (psum - p_hi.astype(F32)).astype(BF16)
    imp = (lax.dot_general(ovt_ref[...], p_hi, nt, preferred_element_type=F32)
           + lax.dot_general(ovt_ref[...], p_lo, nt, preferred_element_type=F32))
    nidx = lax.broadcasted_iota(jnp.int32, (nsel, QB), 0)
    tpos_l = j * QB + lax.broadcasted_iota(jnp.int32, (1, QB), 1)
    blk_t = tpos_l // SLC_LEN
    valid = nidx * SLC_LEN <= tpos_l
    forced = (nidx == 0) | (nidx == blk_t) | (nidx == blk_t - 1)
    score = jnp.where(valid, imp + jnp.where(forced, FORCE_SELECT, 0.0), NEG_INF)
    nidx_f = nidx.astype(F32)
    sel_t = jnp.full((nsel, QB), NEG_INF, F32)
    for _ in range(min(NSA_N_SELECT, nsel)):
        best = jnp.max(score, axis=0, keepdims=True)
        first = jnp.min(jnp.where(score == best, nidx_f, float(nsel)), axis=0, keepdims=True)
        hit = nidx_f == first
        sel_t = jnp.where(hit, 0.0, sel_t)
        score = jnp.where(hit, -jnp.inf, score)
    sel_neg = sel_t.T.astype(BF16)

    for h in range(HPG):
        m_refs[h][...] = jnp.full((QB, LANES), -jnp.inf, F32)
        l_refs[h][...] = jnp.zeros((QB, LANES), F32)
        acc_refs[h][...] = jnp.zeros((QB, DH), F32)
    KT = SEL_TILE
    sub = KT // QB
    q_sel = jnp.concatenate([q, jnp.concatenate([sel_neg] * HPG, axis=0)], axis=1)

    def lane_fold(x, op):
        parts = [x[:, u * LANES:(u + 1) * LANES] for u in range(KT // LANES)]
        while len(parts) > 1:
            parts = [op(parts[a], parts[a + 1]) for a in range(0, len(parts), 2)]
        return parts[0]

    def score_step(t, carry):
        k0 = pl.multiple_of(t * KT, KT)
        k_sel = jnp.concatenate([ks_ref[0, 0, pl.ds(k0, KT), :], et_ref[pl.ds(k0, KT), :]], axis=1)
        s = lax.dot_general(q_sel, k_sel, nt, preferred_element_type=F32)
        for h in range(HPG):
            bias = jnp.concatenate([_dist_tile(bt_ref, h, j - t * sub - u) for u in range(sub)], axis=1)
            sh = s[h * QB:(h + 1) * QB, :] + bias
            sc_ref[t, h * QB:(h + 1) * QB, :] = sh
            m_refs[h][...] = jnp.maximum(m_refs[h][...], lane_fold(sh, jnp.maximum))
        return carry

    def value_step(t, carry):
        k0 = pl.multiple_of(t * KT, KT)
        v_t = vs_ref[0, 0, pl.ds(k0, KT), :]
        for h in range(HPG):
            mb = m_refs[h][...]
            p = jnp.exp2(sc_ref[t, h * QB:(h + 1) * QB, :] - jnp.concatenate([mb] * (KT // LANES), axis=1))
            l_refs[h][...] += lane_fold(p, jnp.add)
            acc_refs[h][...] += jnp.dot(p.astype(BF16), v_t, preferred_element_type=F32)
        return carry

    n_pairs = ((j * QB) // KT + 2) // 2

    def pair(step):
        def body(p, carry):
            return step(2 * p + 1, step(2 * p, carry))
        return body

    lax.fori_loop(0, n_pairs, pair(score_step), 0)
    for h in range(HPG):
        m_refs[h][...] = jnp.broadcast_to(jnp.max(m_refs[h][...], axis=-1, keepdims=True), (QB, LANES))
    lax.fori_loop(0, n_pairs, pair(value_step), 0)
    o_s = [acc_refs[h][...] / jnp.sum(l_refs[h][...], axis=-1, keepdims=True) for h in range(HPG)]

    WK = WINDOW + QB
    w0 = pl.multiple_of(j * QB, QB)
    k_w = kw_ref[0, 0, pl.ds(w0, WK), :]
    v_w = vw_ref[0, 0, pl.ds(w0, WK), :]
    s_w = lax.dot_general(q, k_w, nt, preferred_element_type=F32)
    nwt = WK // QB
    tab_w = [jnp.where(j + u >= nwt - 1, TAB_WINDOW_FAR if u == 0 else nwt - u, TAB_MASKED) for u in range(nwt)]
    o_w = []
    for h in range(HPG):
        bias = jnp.concatenate([bt_ref[0, h, tab_w[u]] for u in range(nwt)], axis=1)
        sh = s_w[h * QB:(h + 1) * QB, :] + bias
        mxw = jnp.max(sh, axis=-1, keepdims=True)
        pw = jnp.exp2(sh - mxw)
        lw = jnp.sum(pw, axis=-1, keepdims=True)
        o_w.append(jnp.dot(pw.astype(BF16), v_w, preferred_element_type=F32) / lw)

    gate = jax.nn.sigmoid(gt_ref[...])
    for h in range(HPG):
        gc = gate[:, 8 + 3 * h:9 + 3 * h]
        gs = gate[:, 9 + 3 * h:10 + 3 * h]
        gw = gate[:, 10 + 3 * h:11 + 3 * h]
        y = gc * o_c[h * QB:(h + 1) * QB, :] + gs * o_s[h] + gw * o_w[h]
        o_ref[:, h * DH:(h + 1) * DH] = y.astype(o_ref.dtype)


def nsa_tables(rel_bias, seq):
    def bias_of(dist):
        dist = jnp.maximum(dist, 0)
        max_exact = NUM_BUCKETS // 2
        log_ratio = jnp.log(jnp.maximum(dist, 1).astype(F32) / max_exact) / math.log(MAX_DISTANCE / max_exact)
        large = jnp.minimum(max_exact + (log_ratio * (NUM_BUCKETS - max_exact)).astype(jnp.int32), NUM_BUCKETS - 1)
        bucket = jnp.where(dist < max_exact, dist, large)
        onehot = (bucket[..., None] == jnp.arange(NUM_BUCKETS)).astype(F32)
        out = jnp.dot(onehot, rel_bias.astype(F32), precision=lax.Precision.HIGHEST)
        return jnp.moveaxis(out, -1, 0)

    assert seq // CMP_STRIDE >= LANES and Q_BLOCK * (N_DIST_TILES - 1) - (Q_BLOCK - 1) >= MAX_DISTANCE
    i = jnp.arange(Q_BLOCK)
    delta = i[None, :, None] - i[None, None, :]
    dist = jnp.arange(N_DIST_TILES)[:, None, None] * Q_BLOCK + delta
    tiles = bias_of(dist) * LOG2E
    causal = jnp.where(dist >= 0, tiles, NEG_INF)
    far = jnp.where(delta < 0, tiles[:, WINDOW // Q_BLOCK], NEG_INF)[:, None]
    masked = jnp.full_like(far, NEG_INF)
    bt = jnp.concatenate([masked, causal, far], axis=1).reshape(NSA_G, NSA_HPG, N_TAB, Q_BLOCK, Q_BLOCK)
    per_tile = Q_BLOCK // CMP_STRIDE
    assert (LANES - per_tile) * CMP_STRIDE - (CMP_LEN - 1) >= MAX_DISTANCE
    dist_c = i[:, None] - ((jnp.arange(LANES)[None, :] - (LANES - per_tile)) * CMP_STRIDE + CMP_LEN - 1)
    rel = (bias_of(dist_c) * LOG2E).reshape(NSA_G, NSA_HPG, Q_BLOCK, LANES)
    return bt, rel


def nsa_block_indicator(seq):
    nsel = seq // SLC_LEN
    return jnp.asarray(np.arange(seq)[:, None] // SLC_LEN == np.arange(nsel)[None, :], BF16)


def nsa_overlap(seq):
    ncp = seq // CMP_STRIDE
    ns = seq // SLC_LEN
    cstart = np.arange(ncp) * CMP_STRIDE
    sstart = np.arange(ns) * SLC_LEN
    ov = np.clip(np.minimum(cstart[:, None] + CMP_LEN, sstart[None, :] + SLC_LEN)
                 - np.maximum(cstart[:, None], sstart[None, :]), 0, None).astype(np.float32) / CMP_LEN
    ov[ncp - 1] = 0.0
    return jnp.asarray(ov.T, BF16)


def nsa_attend(z, pq, pkv, pw, cmp, bt, rel, ovt, et, bsz, seq):
    nqb = seq // Q_BLOCK
    ncp = seq // CMP_STRIDE
    nsel = seq // SLC_LEN
    R = NSA_HPG * Q_BLOCK
    assert (seq // SEL_TILE) % 2 == 0
    return pl.pallas_call(
        _nsa_kernel,
        grid=(bsz, NSA_G, nqb),
        in_specs=[pl.BlockSpec((1, NSA_HPG, Q_BLOCK, NSA_DH), lambda b, g, j: (b, g, j, 0)),
                  pl.BlockSpec((1, 1, 1, ncp, NSA_DH), lambda b, g, j: (0, b, g, 0, 0)),
                  pl.BlockSpec((1, 1, 1, ncp, NSA_DH), lambda b, g, j: (1, b, g, 0, 0)),
                  pl.BlockSpec((1, NSA_HPG, Q_BLOCK, LANES), lambda b, g, j: (g, 0, 0, 0)),
                  pl.BlockSpec((nsel, ncp), lambda b, g, j: (0, 0)),
                  pl.BlockSpec((1, 1, seq, NSA_DH), lambda b, g, j: (b, g, 0, 0)),
                  pl.BlockSpec((1, 1, seq, NSA_DH), lambda b, g, j: (b, NSA_G + g, 0, 0)),
                  pl.BlockSpec((seq, nsel), lambda b, g, j: (0, 0)),
                  pl.BlockSpec((1, 1, seq + WINDOW, NSA_DH), lambda b, g, j: (b, g, 0, 0)),
                  pl.BlockSpec((1, 1, seq + WINDOW, NSA_DH), lambda b, g, j: (b, NSA_G + g, 0, 0)),
                  pl.BlockSpec((1, NSA_HPG, N_TAB, Q_BLOCK, Q_BLOCK), lambda b, g, j: (g, 0, 0, 0, 0)),
                  pl.BlockSpec((Q_BLOCK, LANES), lambda b, g, j: (b * nqb + j, Z_SMALL // LANES + g))],
        out_specs=pl.BlockSpec((Q_BLOCK, NSA_HPG * NSA_DH), lambda b, g, j: (b * nqb + j, g)),
        out_shape=jax.ShapeDtypeStruct((bsz * seq, BRANCH_WIDTH), BF16),
        scratch_shapes=([pltpu.VMEM((seq // SEL_TILE, R, SEL_TILE), F32)]
                        + [pltpu.VMEM((Q_BLOCK, LANES), F32)] * (2 * NSA_HPG)
                        + [pltpu.VMEM((Q_BLOCK, NSA_DH), F32)] * NSA_HPG),
        compiler_params=_cp("arbitrary", "arbitrary", "arbitrary"),
        name="nsa_attend",
    )(pq, cmp, cmp, rel, ovt, pkv, pkv, et, pw, pw, bt, z)


def _merge_kernel(ya_ref, yb_ref, yc_ref, w_ref, ga_ref, gb_ref, gc_ref, o_ref):
    acc = jax.nn.sigmoid(ga_ref[...]) * jnp.dot(ya_ref[...], w_ref[0], preferred_element_type=F32)
    acc = acc + jax.nn.sigmoid(gb_ref[...]) * jnp.dot(yb_ref[...], w_ref[1], preferred_element_type=F32)
    acc = acc + jax.nn.sigmoid(gc_ref[...]) * jnp.dot(yc_ref[...], w_ref[2], preferred_element_type=F32)
    o_ref[...] = acc.astype(o_ref.dtype)


def merge(ya, yb, yc, w_branch, z, tm=1024, tn=512):
    n = ya.shape[0]
    d = w_branch.shape[2]
    bw = ya.shape[1]
    ymap = lambda i, j: (i, 0)
    gmap = lambda k: (lambda i, j: (i, (Z_GATE + k * d) // tn + j))
    return pl.pallas_call(
        _merge_kernel,
        grid=(n // tm, d // tn),
        in_specs=[pl.BlockSpec((tm, bw), ymap), pl.BlockSpec((tm, bw), ymap), pl.BlockSpec((tm, bw), ymap),
                  pl.BlockSpec((3, bw, tn), lambda i, j: (0, 0, j)),
                  pl.BlockSpec((tm, tn), gmap(0)), pl.BlockSpec((tm, tn), gmap(1)), pl.BlockSpec((tm, tn), gmap(2))],
        out_specs=pl.BlockSpec((tm, tn), lambda i, j: (i, j)),
        out_shape=jax.ShapeDtypeStruct((n, d), BF16),
        compiler_params=_cp("arbitrary", "arbitrary"),
        name="merge",
    )(ya, yb, yc, w_branch, z, z, z)


def _outproj_kernel(a_ref, w_ref, x_ref, g_ref, o_ref):
    o_ref[...] = x_ref[...] + g_ref[0] * jnp.dot(a_ref[...], w_ref[...], preferred_element_type=F32)


def out_proj(a, w, x2d, gate, seq, tm=1024, tn=512):
    n, k = a.shape
    d = w.shape[1]
    per_b = seq // tm
    return pl.pallas_call(
        _outproj_kernel,
        grid=(n // tm, d // tn),
        in_specs=[pl.BlockSpec((tm, k), lambda i, j: (i, 0)),
                  pl.BlockSpec((k, tn), lambda i, j: (0, j)),
                  pl.BlockSpec((tm, tn), lambda i, j: (i, j)),
                  pl.BlockSpec((1, 1, tn), lambda i, j: (i // per_b, 0, j))],
        out_specs=pl.BlockSpec((tm, tn), lambda i, j: (i, j)),
        out_shape=jax.ShapeDtypeStruct((n, d), F32),
        compiler_params=_cp("arbitrary", "arbitrary"),
        name="out_proj",
    )(a, w, x2d, gate[:, None, :])


def _pack_bf16_pairs(y):
    half = y.shape[1] // 2
    lo = lax.bitcast_convert_type(y[:, :half].astype(BF16).astype(F32), jnp.uint32)
    hi = lax.bitcast_convert_type(y[:, half:].astype(BF16).astype(F32), jnp.uint32)
    return lax.shift_right_logical(lo, jnp.uint32(16)) | (hi & jnp.uint32(0xFFFF0000))


def _unpack_bf16_pairs(w):
    lo = lax.bitcast_convert_type(lax.shift_left(w, jnp.uint32(16)), F32)
    hi = lax.bitcast_convert_type(w & jnp.uint32(0xFFFF0000), F32)
    return jnp.concatenate([lo, hi], axis=1)


def _router_kernel(x_ref, g_ref, sc_ref, sh_ref, rw_ref, rwl_ref, rb_ref, h_ref, idx_ref, p_ref):
    x = x_ref[...]
    y = x * lax.rsqrt(jnp.mean(x * x, axis=-1, keepdims=True) + EPS)
    hf = (y * g_ref[...]) * (1.0 + sc_ref[0]) + sh_ref[0]
    h = hf.astype(BF16)
    h_ref[...] = _pack_bf16_pairs(hf)
    h_lo = (hf - h.astype(F32)).astype(BF16)
    logits = (jnp.dot(h, rw_ref[...], preferred_element_type=F32)
              + jnp.dot(h_lo, rw_ref[...], preferred_element_type=F32)
              + jnp.dot(h, rwl_ref[...], preferred_element_type=F32)) + rb_ref[...]
    lane = lax.broadcasted_iota(jnp.int32, logits.shape, 1)
    lane_f = lane.astype(F32)
    idx_out = jnp.zeros(logits.shape, F32)
    val_out = jnp.full(logits.shape, -jnp.inf, F32)
    for k in range(TOP_K):
        best = jnp.max(logits, axis=-1, keepdims=True)
        first = jnp.min(jnp.where(logits == best, lane_f, float(LANES)), axis=-1, keepdims=True)
        idx_out = jnp.where(lane == k, first, idx_out)
        val_out = jnp.where(lane == k, best, val_out)
        logits = jnp.where(lane_f == first, -jnp.inf, logits)
    e = jnp.exp(val_out - jnp.max(val_out, axis=-1, keepdims=True))
    idx_ref[...] = idx_out.astype(jnp.int32)
    p_ref[...] = e / jnp.sum(e, axis=-1, keepdims=True)


def router(x2d, g, sc, sh, rw, rb, seq, tm=1024):
    n, d = x2d.shape
    per_b = seq // tm
    rw_hi = rw.astype(BF16)
    rw_lo = (rw - rw_hi.astype(F32)).astype(BF16)
    rwp = jnp.zeros((d, LANES), BF16).at[:, :N_EXPERTS].set(rw_hi)
    rwl = jnp.zeros((d, LANES), BF16).at[:, :N_EXPERTS].set(rw_lo)
    rbp = jnp.full((1, LANES), -jnp.inf, F32).at[0, :N_EXPERTS].set(rb)
    return pl.pallas_call(
        _router_kernel,
        grid=(n // tm,),
        in_specs=[pl.BlockSpec((tm, d), lambda i: (i, 0)),
                  pl.BlockSpec((1, d), lambda i: (0, 0)),
                  pl.BlockSpec((1, 1, d), lambda i: (i // per_b, 0, 0)),
                  pl.BlockSpec((1, 1, d), lambda i: (i // per_b, 0, 0)),
                  pl.BlockSpec((d, LANES), lambda i: (0, 0)),
                  pl.BlockSpec((d, LANES), lambda i: (0, 0)),
                  pl.BlockSpec((1, LANES), lambda i: (0, 0))],
        out_specs=[pl.BlockSpec((tm, d // 2), lambda i: (i, 0)),
                   pl.BlockSpec((tm, LANES), lambda i: (i, 0)),
                   pl.BlockSpec((tm, LANES), lambda i: (i, 0))],
        out_shape=[jax.ShapeDtypeStruct((n, d // 2), jnp.uint32),
                   jax.ShapeDtypeStruct((n, LANES), jnp.int32),
                   jax.ShapeDtypeStruct((n, LANES), F32)],
        compiler_params=_cp("arbitrary"),
        name="router",
    )(x2d, g.reshape(1, d), sc[:, None, :], sh[:, None, :], rwp, rwl, rbp)


def _dispatch_kernel(dest_ref, h_ref, init_ref, xb_ref, sem):
    del init_ref
    tm = h_ref.shape[0]

    def row_copy(k, r):
        dst = dest_ref[0, 0, k * tm + r]
        return pltpu.make_async_copy(h_ref.at[pl.ds(r, 1), :], xb_ref.at[pl.ds(dst, 1), :], sem)

    def start(r, carry):
        for k in range(TOP_K):
            row_copy(k, r).start()
        return carry

    def wait(r, carry):
        for k in range(TOP_K):
            row_copy(k, r).wait()
        return carry

    lax.fori_loop(0, tm, start, 0, unroll=8)
    lax.fori_loop(0, tm, wait, 0, unroll=8)


def dispatch(hp, dest_tiles, rows):
    n, w = hp.shape
    nb, _, per_tile = dest_tiles.shape
    tm = per_tile // TOP_K
    return pl.pallas_call(
        _dispatch_kernel,
        grid=(nb,),
        in_specs=[pl.BlockSpec((1, 1, per_tile), lambda i: (i, 0, 0), memory_space=pltpu.SMEM),
                  pl.BlockSpec((tm, w), lambda i: (i, 0)),
                  pl.BlockSpec(memory_space=pl.ANY)],
        out_specs=pl.BlockSpec(memory_space=pl.ANY),
        out_shape=jax.ShapeDtypeStruct((rows, w), jnp.uint32),
        scratch_shapes=[pltpu.SemaphoreType.DMA(())],
        input_output_aliases={2: 0},
        compiler_params=_cp("arbitrary"),
        name="dispatch",
    )(dest_tiles, hp, jnp.zeros((rows, w), jnp.uint32))


def _expert_kernel(be_ref, nv_ref, x_ref, wg_ref, wu_ref, bg_ref, bu_ref, w2_ref, b2_ref, o_ref, acc_ref, xs_ref):
    i = pl.program_id(0)
    f = pl.program_id(1)

    @pl.when(i < nv_ref[0])
    def _():
        @pl.when(f == 0)
        def _():
            acc_ref[...] = jnp.zeros(acc_ref.shape, F32)
            xs_ref[...] = _unpack_bf16_pairs(x_ref[...]).astype(BF16)

        x = xs_ref[...]
        gate = jnp.dot(x, wg_ref[0, 0].astype(BF16), preferred_element_type=F32) + bg_ref[0, 0]
        up = jnp.dot(x, wu_ref[0, 0].astype(BF16), preferred_element_type=F32) + bu_ref[0, 0]
        gate = jnp.minimum(gate, SWIGLU_LIMIT)
        up = jnp.clip(up, -SWIGLU_LIMIT, SWIGLU_LIMIT)
        act = (up + 1.0) * (gate * jax.nn.sigmoid(SWIGLU_ALPHA * gate))
        acc_ref[...] += jnp.dot(act.astype(BF16), w2_ref[0, 0].astype(BF16), preferred_element_type=F32)

        @pl.when(f == pl.num_programs(1) - 1)
        def _():
            o_ref[...] = _pack_bf16_pairs(acc_ref[...] + b2_ref[0, 0])

    @pl.when(i >= nv_ref[0])
    def _():
        o_ref[...] = jnp.zeros(o_ref.shape, o_ref.dtype)


def experts(xb, block_exp, n_valid, w1, b1, w2, b2, layer, tm, tf=256):
    rows = xb.shape[0]
    d = 2 * xb.shape[1]
    nf = D_FF // tf
    nblk = rows // tm
    n_exp = w1.shape[1]
    fe = lambda i, f, nv: jnp.where(i < nv[0], f, nf - 1)
    grid_spec = pltpu.PrefetchScalarGridSpec(
        num_scalar_prefetch=2,
        grid=(nblk, nf),
        in_specs=[pl.BlockSpec((tm, d // 2), lambda i, f, be, nv: (jnp.minimum(i, nv[0] - 1), 0)),
                  pl.BlockSpec((1, 1, d, tf), lambda i, f, be, nv: (layer, be[i], 0, fe(i, f, nv))),
                  pl.BlockSpec((1, 1, d, tf), lambda i, f, be, nv: (layer, be[i], 0, nf + fe(i, f, nv))),
                  pl.BlockSpec((1, 1, 1, tf), lambda i, f, be, nv: (layer * n_exp + be[i], 0, 0, fe(i, f, nv))),
                  pl.BlockSpec((1, 1, 1, tf), lambda i, f, be, nv: (layer * n_exp + be[i], 0, 0, nf + fe(i, f, nv))),
                  pl.BlockSpec((1, 1, tf, d), lambda i, f, be, nv: (layer, be[i], fe(i, f, nv), 0)),
                  pl.BlockSpec((1, 1, 1, d), lambda i, f, be, nv: (layer * n_exp + be[i], 0, 0, 0))],
        out_specs=pl.BlockSpec((tm, d // 2), lambda i, f, be, nv: (i, 0)),
        scratch_shapes=[pltpu.VMEM((tm, d), F32), pltpu.VMEM((tm, d), BF16)],
    )
    b1r = b1.reshape(-1, 1, 1, b1.shape[-1])
    b2r = b2.reshape(-1, 1, 1, d)
    return pl.pallas_call(
        _expert_kernel,
        grid_spec=grid_spec,
        out_shape=jax.ShapeDtypeStruct((rows, d // 2), jnp.uint32),
        compiler_params=_cp("arbitrary", "arbitrary"),
        name="experts",
    )(block_exp, n_valid, xb, w1, w1, b1r, b1r, w2, b2r)


def _combine_kernel(dest_ref, p_ref, x_ref, g_ref, yb_ref, o_ref, ybuf_ref, sem):
    tm = x_ref.shape[0]

    def row_copy(k, r):
        src = dest_ref[0, 0, k * tm + r]
        return pltpu.make_async_copy(yb_ref.at[pl.ds(src, 1), :], ybuf_ref.at[k, pl.ds(r, 1), :], sem)

    def start(r, carry):
        for k in range(TOP_K):
            row_copy(k, r).start()
        return carry

    def wait(r, carry):
        for k in range(TOP_K):
            row_copy(k, r).wait()
        return carry

    lax.fori_loop(0, tm, start, 0, unroll=8)
    lax.fori_loop(0, tm, wait, 0, unroll=8)
    p = p_ref[...]
    y = p[:, 0:1] * _unpack_bf16_pairs(ybuf_ref[0])
    for k in range(1, TOP_K):
        y = y + p[:, k:k + 1] * _unpack_bf16_pairs(ybuf_ref[k])
    o_ref[...] = x_ref[...] + g_ref[0] * y


def combine(yb, dest_tiles, prob, x2d, gate, seq):
    n, d = x2d.shape
    nb = dest_tiles.shape[0]
    tm = n // nb
    per_b = seq // tm
    return pl.pallas_call(
        _combine_kernel,
        grid=(nb,),
        in_specs=[pl.BlockSpec((1, 1, TOP_K * tm), lambda i: (i, 0, 0), memory_space=pltpu.SMEM),
                  pl.BlockSpec((tm, LANES), lambda i: (i, 0)),
                  pl.BlockSpec((tm, d), lambda i: (i, 0)),
                  pl.BlockSpec((1, 1, d), lambda i: (i // per_b, 0, 0)),
                  pl.BlockSpec(memory_space=pl.ANY)],
        out_specs=pl.BlockSpec((tm, d), lambda i: (i, 0)),
        out_shape=jax.ShapeDtypeStruct((n, d), F32),
        scratch_shapes=[pltpu.VMEM((TOP_K, tm, d // 2), jnp.uint32), pltpu.SemaphoreType.DMA(())],
        compiler_params=_cp("arbitrary"),
        name="combine",
    )(dest_tiles, prob, x2d, gate[:, None, :], yb)


def moe(x2d, g, sc, sh, gate, rw, rb, w1, b1, w2, b2, layer, seq, tm_e=1024, tm_t=512):
    n, d = x2d.shape
    h, idx, prob = router(x2d, g, sc, sh, rw, rb, seq)
    e_flat = idx[:, :TOP_K].T.reshape(-1)
    na = n * TOP_K
    onehot = (e_flat[:, None] == jnp.arange(N_EXPERTS)[None, :]).astype(jnp.int32)
    csum = jnp.cumsum(onehot, axis=0)
    counts = csum[-1]
    rank = jnp.take_along_axis(csum, e_flat[:, None], axis=1)[:, 0] - 1
    padded = (counts + tm_e - 1) // tm_e * tm_e
    pend = jnp.cumsum(padded)
    pstart = pend - padded
    dest = pstart[e_flat] + rank
    nblk = na // tm_e + N_EXPERTS
    rows = nblk * tm_e
    starts = jnp.arange(nblk, dtype=jnp.int32) * tm_e
    block_exp = jnp.minimum(jnp.sum(pend[None, :] <= starts[:, None], axis=1), N_EXPERTS - 1).astype(jnp.int32)
    n_valid = (pend[-1:] // tm_e).astype(jnp.int32)
    block_exp = jnp.where(starts < pend[-1], block_exp, block_exp[jnp.maximum(n_valid[0] - 1, 0)])
    nb = n // tm_t
    dest_tiles = dest.reshape(TOP_K, nb, tm_t).transpose(1, 0, 2).reshape(nb, 1, TOP_K * tm_t)
    xb = dispatch(h, dest_tiles, rows)
    yb = experts(xb, block_exp, n_valid, w1, b1, w2, b2, layer, tm_e)
    return combine(yb, dest_tiles, prob, x2d, gate, seq)


def fused_in_weight(w_in, w_gate):
    sizes = (512, 512, 1024, 4, 4, 1024, 1024, 256, 256, 256, 256, 256, 256, 24, 1024, 1024)
    pts = np.cumsum(sizes)[:-1].tolist()
    (qm, km, vm, ig, fg, og, qn, kc, vc, ks, vs, kw, vw, gn, gu, gv) = jnp.split(w_in, pts, axis=1)
    d = w_in.shape[0]
    pad = lambda k: jnp.zeros((d, k), w_in.dtype)
    per_group = NSA_HPG * 3
    small0 = jnp.concatenate([ig, fg, gn[:, :per_group], pad(LANES - 8 - per_group)], axis=1)
    small1 = jnp.concatenate([pad(8), gn[:, per_group:], pad(LANES - 8 - per_group)], axis=1)
    cols = [qm, km, vm, og, qn, gu, gv, kc, vc, ks, vs, kw, vw, w_gate[0], w_gate[1], w_gate[2], small0, small1]
    w = jnp.concatenate(cols, axis=1)
    w = jnp.concatenate([w, pad(Z_WIDTH - w.shape[1])], axis=1)
    return w.astype(BF16)


def kernel(x, c, ada_w, ada_b, norm1_g, norm2_g, w_in, conv_w, conv_b, mlstm_gate_b, mlstm_norm_g, cmp_pos,
           cmp_k_w1, cmp_k_w2, cmp_v_w1, cmp_v_w2, qnorm_g, knorm_g, rel_bias, gmlp_norm_g, gmlp_ws, gmlp_b,
           w_branch, w_gate, w_out, router_w, router_b, exp_w1, exp_b1, exp_w2, exp_b2):
    bsz, seq, d = x.shape
    depth = ada_w.shape[0]
    x2d = x.reshape(bsz * seq, d)
    bt, rel = nsa_tables(rel_bias, seq)
    ovt = nsa_overlap(seq)
    et = nsa_block_indicator(seq)
    for l in range(depth):
        mod = ada_mod(c, ada_w, ada_b, l)
        sh1, sc1, g1, sh2, sc2, g2 = jnp.split(mod, 6, axis=-1)
        z = in_proj(x2d, norm1_g[l], sc1, sh1, fused_in_weight(w_in[l], w_gate[l]), seq)
        ya = mlstm(z, bsz, seq, conv_w[l], conv_b[l], mlstm_gate_b[l], mlstm_norm_g[l])
        pq, pkv, pw = nsa_prep(z, bsz, seq, qnorm_g[l], knorm_g[l])
        cmp = nsa_compress(z, bsz, seq, cmp_pos[l], cmp_k_w1[l], cmp_k_w2[l], cmp_v_w1[l], cmp_v_w2[l], knorm_g[l])
        yb = nsa_attend(z, pq, pkv, pw, cmp, bt, rel, ovt, et, bsz, seq)
        yc = gmlp(z, gmlp_norm_g[l], gmlp_ws[l], gmlp_b[l])
        merged = merge(ya, yb, yc, w_branch[l].astype(BF16), z)
        x2d = out_proj(merged, w_out[l].astype(BF16), x2d, g1, seq)
        x2d = moe(x2d, norm2_g[l], sc2, sh2, g2, router_w[l], router_b[l], exp_w1, exp_b1, exp_w2, exp_b2, l, seq)
    return x2d.reshape(bsz, seq, d)
```

```python
import functools
import math

import jax
import jax.numpy as jnp
import numpy as np
from jax import lax
from jax.experimental import pallas as pl
from jax.experimental.pallas import tpu as pltpu

F32 = jnp.float32
BF16 = jnp.bfloat16

LANES = 128
VMEM_LIMIT = 56 * 1024 * 1024

D_MODEL = 2048
BRANCH_WIDTH = 1024
MLSTM_HEADS = 4
MLSTM_DV = 256
MLSTM_DQK = 128
MLSTM_TILE = 256
CONV_WIDTH = 4
NSA_HEADS = 8
NSA_G = 2
NSA_HPG = 4
NSA_DH = 128
CMP_LEN = 32
CMP_STRIDE = 16
SLC_LEN = 64
NSA_N_SELECT = 16
WINDOW = 512
Q_BLOCK = 128
NUM_BUCKETS = 32
MAX_DISTANCE = 1024
GMLP_CHUNK = 128
GMLP_GROUPS = 8
N_EXPERTS = 32
TOP_K = 4
D_FF = 1536
SWIGLU_LIMIT = 7.0
SWIGLU_ALPHA = 1.702
NEG_INF = -1e30
FORCE_SELECT = 1e4
EPS = 1e-6

LOG2E = math.log2(math.e)
SEL_TILE = 512
N_DIST_TILES = 10
TAB_MASKED, TAB_WINDOW_FAR, N_TAB = 0, N_DIST_TILES + 1, N_DIST_TILES + 2

Z_QK, Z_V, Z_OG, Z_NQ, Z_GU, Z_GV = 0, 1024, 2048, 3072, 4096, 5120
Z_KV = 6144
Z_GATE = 7680
Z_SMALL = 13824
Z_WIDTH = 14336


def _cp(*sem):
    return pltpu.CompilerParams(dimension_semantics=sem, vmem_limit_bytes=VMEM_LIMIT)


def _ada_kernel(c_ref, w_ref, b_ref, o_ref):
    a = jax.nn.silu(c_ref[...]).astype(BF16)
    o_ref[...] = jnp.dot(a, w_ref[0].astype(BF16), preferred_element_type=F32) + b_ref[0]


def ada_mod(c, w, b, layer):
    bsz, d = c.shape
    n = w.shape[2]
    tn = 1024
    cp = jnp.zeros((8, d), F32).at[:bsz].set(c)
    out = pl.pallas_call(
        _ada_kernel,
        grid=(n // tn,),
        in_specs=[pl.BlockSpec((8, d), lambda j: (0, 0)),
                  pl.BlockSpec((1, d, tn), lambda j: (layer, 0, j)),
                  pl.BlockSpec((1, 1, tn), lambda j: (layer, 0, j))],
        out_specs=pl.BlockSpec((8, tn), lambda j: (0, j)),
        out_shape=jax.ShapeDtypeStruct((8, n), F32),
        compiler_params=_cp("arbitrary"),
        name="ada_mod",
    )(cp, w, b[:, None, :])
    return out[:bsz]


def _in_kernel(x_ref, g_ref, sc_ref, sh_ref, w_ref, o_ref, h_ref):
    @pl.when(pl.program_id(1) == 0)
    def _():
        x = x_ref[...]
        y = x * lax.rsqrt(jnp.mean(x * x, axis=-1, keepdims=True) + EPS)
        h = (y * g_ref[...]) * (1.0 + sc_ref[0]) + sh_ref[0]
        h_ref[...] = h.astype(BF16)

    o_ref[...] = jnp.dot(h_ref[...], w_ref[...], preferred_element_type=F32)


def in_proj(x2d, g, sc, sh, wcat, seq, tm=1024, tn=1024):
    n, d = x2d.shape
    w = wcat.shape[1]
    per_b = seq // tm
    return pl.pallas_call(
        _in_kernel,
        grid=(n // tm, w // tn),
        in_specs=[pl.BlockSpec((tm, d), lambda i, j: (i, 0)),
                  pl.BlockSpec((1, d), lambda i, j: (0, 0)),
                  pl.BlockSpec((1, 1, d), lambda i, j: (i // per_b, 0, 0)),
                  pl.BlockSpec((1, 1, d), lambda i, j: (i // per_b, 0, 0)),
                  pl.BlockSpec((d, tn), lambda i, j: (0, j))],
        out_specs=pl.BlockSpec((tm, tn), lambda i, j: (i, j)),
        out_shape=jax.ShapeDtypeStruct((n, w), F32),
        scratch_shapes=[pltpu.VMEM((tm, d), BF16)],
        compiler_params=_cp("arbitrary", "arbitrary"),
        name="in_proj",
    )(x2d, g.reshape(1, d), sc[:, None, :], sh[:, None, :], wcat)


def _cumsum_rows(x, n):
    row = lax.broadcasted_iota(jnp.int32, x.shape, 0)
    s = 1
    while s < n:
        x = x + jnp.where(row >= s, pltpu.roll(x, s, 0), 0.0)
        s *= 2
    return x


def _mlstm_kernel(qk_ref, v_ref, og_ref, sm_ref, cw_ref, cb_ref, gb_ref, ng_ref, o_ref,
                  buf_ref, ct_ref, n_ref, m_ref):
    L = qk_ref.shape[0]
    H, DQK, DV = MLSTM_HEADS, MLSTM_DQK, MLSTM_DV

    @pl.when(pl.program_id(1) == 0)
    def _():
        buf_ref[0:8, :] = jnp.zeros((8, buf_ref.shape[1]), F32)
        ct_ref[...] = jnp.zeros(ct_ref.shape, F32)
        n_ref[...] = jnp.zeros(n_ref.shape, F32)
        m_ref[...] = jnp.full(m_ref.shape, NEG_INF, F32)

    buf_ref[8:8 + L, :] = qk_ref[...]
    conv = cb_ref[...] + cw_ref[3:4, :] * buf_ref[8:8 + L, :]
    for j in range(CONV_WIDTH - 1):
        conv = conv + cw_ref[j:j + 1, :] * buf_ref[5 + j:5 + j + L, :]
    qk = jax.nn.silu(conv)
    buf_ref[0:8, :] = buf_ref[L:L + 8, :]

    sm = sm_ref[...] + gb_ref[...]
    bcum = _cumsum_rows(jax.nn.log_sigmoid(sm), L)
    sm_t = sm.T
    bcum_t = bcum.T
    row = lax.broadcasted_iota(jnp.int32, (L, L), 0)
    col = lax.broadcasted_iota(jnp.int32, (L, L), 1)
    causal = row >= col

    for h in range(H):
        qf = qk[:, h * DQK:(h + 1) * DQK] * DQK ** -0.5
        q = qf.astype(BF16)
        kf = qk[:, (H + h) * DQK:(H + h + 1) * DQK]
        k = kf.astype(BF16)
        v = v_ref[:, h * DV:(h + 1) * DV].astype(BF16)
        b_col = bcum[:, H + h:H + h + 1]
        i_col = sm[:, h:h + 1]
        b_row = bcum_t[H + h:H + h + 1, :]
        i_row = sm_t[h:h + 1, :]
        m_prev = m_ref[h:h + 1, 0:1]
        a = b_col + m_prev
        dmat = jnp.where(causal, b_col - b_row + i_row, -jnp.inf)
        m_t = jnp.maximum(a, jnp.max(dmat, axis=1, keepdims=True))
        w_intra = jnp.exp(dmat - m_t)
        w_inter = jnp.exp(a - m_t)
        s = lax.dot_general(q, k, (((1,), (1,)), ((), ())), preferred_element_type=F32) * w_intra
        ct = ct_ref[h]
        nvec = n_ref[h:h + 1, :]
        num = jnp.dot(s.astype(BF16), v, preferred_element_type=F32) + w_inter * jnp.dot(
            q, ct.astype(BF16), preferred_element_type=F32)
        qn = jnp.sum(qf * nvec, axis=1, keepdims=True)
        den = jnp.sum(s, axis=1, keepdims=True) + w_inter * qn
        hm = num / jnp.maximum(jnp.abs(den), jnp.exp(-m_t))
        b_last = b_col[L - 1:L, :]
        gdec = b_last - b_col + i_col
        m_new = jnp.maximum(b_last + m_prev, jnp.max(gdec, axis=0, keepdims=True))
        ws = jnp.exp(gdec - m_new)
        wc = jnp.exp(b_last + m_prev - m_new)
        kw = ws * kf
        ct_ref[h] = wc * ct + jnp.dot(kw.T.astype(BF16), v, preferred_element_type=F32)
        n_ref[h:h + 1, :] = wc * nvec + jnp.sum(kw, axis=0, keepdims=True)
        m_ref[h:h + 1, :] = jnp.broadcast_to(m_new, (1, LANES))
        y = hm * lax.rsqrt(jnp.mean(hm * hm, axis=-1, keepdims=True) + EPS) * ng_ref[:, h * DV:(h + 1) * DV]
        og = og_ref[:, h * DV:(h + 1) * DV]
        o_ref[:, h * DV:(h + 1) * DV] = (jax.nn.sigmoid(og) * y).astype(o_ref.dtype)


def mlstm(z, bsz, seq, conv_w, conv_b, gate_b, norm_g):
    L = MLSTM_TILE
    nch = seq // L
    gb = jnp.zeros((1, LANES), F32).at[0, :2 * MLSTM_HEADS].set(gate_b)
    wide = BRANCH_WIDTH
    rowmap = lambda col: (lambda b, c: (b * nch + c, col))
    const = lambda b, c: (0, 0)
    return pl.pallas_call(
        _mlstm_kernel,
        grid=(bsz, nch),
        in_specs=[pl.BlockSpec((L, wide), rowmap(Z_QK // wide)),
                  pl.BlockSpec((L, wide), rowmap(Z_V // wide)),
                  pl.BlockSpec((L, wide), rowmap(Z_OG // wide)),
                  pl.BlockSpec((L, LANES), rowmap(Z_SMALL // LANES)),
                  pl.BlockSpec((CONV_WIDTH, wide), const),
                  pl.BlockSpec((1, wide), const),
                  pl.BlockSpec((1, LANES), const),
                  pl.BlockSpec((1, wide), const)],
        out_specs=pl.BlockSpec((L, wide), lambda b, c: (b * nch + c, 0)),
        out_shape=jax.ShapeDtypeStruct((bsz * seq, wide), BF16),
        scratch_shapes=[pltpu.VMEM((L + 8, wide), F32),
                        pltpu.VMEM((MLSTM_HEADS, MLSTM_DQK, MLSTM_DV), F32),
                        pltpu.VMEM((8, MLSTM_DQK), F32),
                        pltpu.VMEM((8, LANES), F32)],
        compiler_params=_cp("arbitrary", "arbitrary"),
        name="mlstm",
    )(z, z, z, z, conv_w, conv_b.reshape(1, wide), gb, norm_g.reshape(1, wide))


def _gmlp_kernel(u_ref, v_ref, ng_ref, ws_ref, b_ref, o_ref):
    T = u_ref.shape[0]
    C = GMLP_CHUNK
    u = jax.nn.gelu(u_ref[...])
    vg = jax.nn.gelu(v_ref[...])
    mu = jnp.mean(vg, axis=-1, keepdims=True)
    var = jnp.mean(jnp.square(vg - mu), axis=-1, keepdims=True)
    v = ((vg - mu) * lax.rsqrt(var + EPS) * ng_ref[...]).astype(BF16)
    row = lax.broadcasted_iota(jnp.int32, (C, C), 0)
    col = lax.broadcasted_iota(jnp.int32, (C, C), 1)
    for g in range(GMLP_GROUPS):
        wsc = jnp.where(row >= col, ws_ref[g], 0.0).astype(BF16)
        bcol = b_ref[:, g:g + 1]
        for n in range(T // C):
            vv = v[n * C:(n + 1) * C, g * C:(g + 1) * C]
            s = jnp.dot(wsc, vv, preferred_element_type=F32) + bcol
            o_ref[n * C:(n + 1) * C, g * C:(g + 1) * C] = (
                u[n * C:(n + 1) * C, g * C:(g + 1) * C] * s).astype(o_ref.dtype)


def gmlp(z, norm_g, ws, b, tg=512):
    n = z.shape[0]
    wide = BRANCH_WIDTH
    return pl.pallas_call(
        _gmlp_kernel,
        grid=(n // tg,),
        in_specs=[pl.BlockSpec((tg, wide), lambda i: (i, Z_GU // wide)),
                  pl.BlockSpec((tg, wide), lambda i: (i, Z_GV // wide)),
                  pl.BlockSpec((1, wide), lambda i: (0, 0)),
                  pl.BlockSpec((GMLP_GROUPS, GMLP_CHUNK, GMLP_CHUNK), lambda i: (0, 0, 0)),
                  pl.BlockSpec((GMLP_CHUNK, GMLP_GROUPS), lambda i: (0, 0))],
        out_specs=pl.BlockSpec((tg, wide), lambda i: (i, 0)),
        out_shape=jax.ShapeDtypeStruct((n, wide), BF16),
        compiler_params=_cp("arbitrary"),
        name="gmlp",
    )(z, z, norm_g.reshape(1, wide), ws, b.T)


def _prep_kernel(n_norm, npad, x_ref, g_ref, o_ref):
    nh = o_ref.shape[1]

    @pl.when(pl.program_id(1) < npad)
    def _():
        o_ref[...] = jnp.zeros(o_ref.shape, o_ref.dtype)

    @pl.when(pl.program_id(1) >= npad)
    def _():
        for h in range(nh):
            x = x_ref[:, h * LANES:(h + 1) * LANES]
            if h < n_norm:
                x = x * lax.rsqrt(jnp.mean(x * x, axis=-1, keepdims=True) + EPS) * g_ref[h]
            o_ref[0, h] = x.astype(o_ref.dtype)


def _prep_call(z, bsz, seq, col0, heads, n_norm, gains, pad, ts, name):
    width = heads * LANES
    nblk = seq // ts
    npad = pad // ts
    return pl.pallas_call(
        functools.partial(_prep_kernel, n_norm, npad),
        grid=(bsz, nblk + npad),
        in_specs=[pl.BlockSpec((ts, width), lambda b, s: (b * nblk + jnp.maximum(s - npad, 0), col0 // width)),
                  pl.BlockSpec((heads, 1, LANES), lambda b, s: (0, 0, 0))],
        out_specs=pl.BlockSpec((1, heads, ts, LANES), lambda b, s: (b, 0, s, 0)),
        out_shape=jax.ShapeDtypeStruct((bsz, heads, seq + pad, LANES), BF16),
        compiler_params=_cp("arbitrary", "arbitrary"),
        name=name,
    )(z, gains)


def nsa_prep(z, bsz, seq, qnorm_g, knorm_g):
    gq = jnp.tile((qnorm_g * (NSA_DH ** -0.5 * LOG2E)).reshape(1, 1, LANES), (NSA_HEADS, 1, 1))
    gk = jnp.tile(knorm_g.reshape(1, 1, LANES), (2 * NSA_G, 1, 1))
    pq = _prep_call(z, bsz, seq, Z_NQ, NSA_HEADS, NSA_HEADS, gq, 0, 1024, "nsa_prep_q")
    pkv = _prep_call(z, bsz, seq, Z_KV + 512, 2 * NSA_G, NSA_G, gk, 0, 1024, "nsa_prep_sel")
    pw = _prep_call(z, bsz, seq, Z_KV + 1024, 2 * NSA_G, NSA_G, gk, WINDOW, WINDOW, "nsa_prep_win")
    return pq, pkv, pw


def _compress_kernel(x_ref, pos_ref, w1_ref, w2_ref, g_ref, o_ref):
    half = x_ref.shape[-1]
    x = x_ref[0, 0, 0]
    lo = jnp.dot((x + pos_ref[0, :, 0:half]).astype(BF16), w1_ref[0, 0:half, :], preferred_element_type=F32)
    hi = jnp.dot((x + pos_ref[0, :, half:2 * half]).astype(BF16), w1_ref[0, half:2 * half, :],
                 preferred_element_type=F32)
    ncp = x.shape[0]
    pre = lo + pltpu.roll(hi, ncp - 1, 0)
    y = jnp.dot(jax.nn.gelu(pre).astype(BF16), w2_ref[0], preferred_element_type=F32)
    yn = y * lax.rsqrt(jnp.mean(y * y, axis=-1, keepdims=True) + EPS) * g_ref[...]
    o_ref[0, 0, 0] = jnp.where(pl.program_id(0) == 0, yn, y).astype(o_ref.dtype)


def nsa_compress(z, bsz, seq, cmp_pos, w1k, w2k, w1v, w2v, knorm_g):
    ncp = seq // CMP_STRIDE
    kv = z[:, Z_KV:Z_KV + 512].reshape(bsz, seq, 2, NSA_G, NSA_DH)
    kv = kv.transpose(2, 0, 3, 1, 4).reshape(2, bsz, NSA_G, ncp, CMP_STRIDE * NSA_DH)
    pos = cmp_pos.reshape(1, 1, CMP_LEN * NSA_DH)
    w1 = jnp.stack([w1k, w1v]).astype(BF16)
    w2 = jnp.stack([w2k, w2v]).astype(BF16)
    kdim = CMP_STRIDE * NSA_DH
    return pl.pallas_call(
        _compress_kernel,
        grid=(2, bsz, NSA_G),
        in_specs=[pl.BlockSpec((1, 1, 1, ncp, kdim), lambda t, b, g: (t, b, g, 0, 0)),
                  pl.BlockSpec((1, 1, 2 * kdim), lambda t, b, g: (0, 0, 0)),
                  pl.BlockSpec((1, 2 * kdim, NSA_DH), lambda t, b, g: (t, 0, 0)),
                  pl.BlockSpec((1, NSA_DH, NSA_DH), lambda t, b, g: (t, 0, 0)),
                  pl.BlockSpec((1, NSA_DH), lambda t, b, g: (0, 0))],
        out_specs=pl.BlockSpec((1, 1, 1, ncp, NSA_DH), lambda t, b, g: (t, b, g, 0, 0)),
        out_shape=jax.ShapeDtypeStruct((2, bsz, NSA_G, ncp, NSA_DH), BF16),
        compiler_params=_cp("arbitrary", "arbitrary", "arbitrary"),
        name="nsa_compress",
    )(kv, pos, w1, w2, knorm_g.reshape(1, NSA_DH))


def _dist_tile(bt_ref, h, rho):
    return bt_ref[0, h, jnp.clip(rho, -1, N_DIST_TILES - 1) + 1]


def _nsa_kernel(q_ref, kc_ref, vc_ref, rel_ref, ovt_ref, ks_ref, vs_ref, et_ref, kw_ref, vw_ref, bt_ref, gt_ref,
                o_ref, sc_ref, *state):
    m_refs, l_refs, acc_refs = state[0:NSA_HPG], state[NSA_HPG:2 * NSA_HPG], state[2 * NSA_HPG:3 * NSA_HPG]
    j = pl.program_id(2)
    QB, HPG, DH = Q_BLOCK, NSA_HPG, NSA_DH
    R = HPG * QB
    ncp = kc_ref.shape[3]
    nsel = ovt_ref.shape[0]
    q = q_ref[0].reshape(R, DH)
    nt = (((1,), (1,)), ((), ()))
    qi = lax.broadcasted_iota(jnp.int32, (QB, 1), 0)
    tpos = j * QB + qi

    cidx = lax.broadcasted_iota(jnp.int32, (QB, ncp), 1)
    mask_c = jnp.logical_and(cidx * CMP_STRIDE + (CMP_LEN - 1) <= tpos, cidx < ncp - 1)
    band0 = (j * (QB // CMP_STRIDE) + ncp - (LANES - QB // CMP_STRIDE)) % ncp
    s_c = lax.dot_general(q, kc_ref[0, 0, 0], nt, preferred_element_type=F32)
    p_c = []
    for h in range(HPG):
        sat = bt_ref[0, h, N_DIST_TILES]
        bias = jnp.concatenate([rel_ref[0, h]] + [sat] * (ncp // LANES - 1), axis=1)
        bias = pltpu.roll(bias, band0, 1)
        sh = jnp.where(mask_c, s_c[h * QB:(h + 1) * QB, :] + bias, NEG_INF)
        mx = jnp.max(sh, axis=-1, keepdims=True)
        e = jnp.where(mask_c, jnp.exp2(sh - mx), 0.0)
        den = jnp.sum(e, axis=-1, keepdims=True)
        p_c.append(e * jnp.where(den > 0.0, 1.0 / den, 0.0))
    o_c = jnp.dot(jnp.concatenate(p_c, axis=0).astype(BF16), vc_ref[0, 0, 0], preferred_element_type=F32)

    psum = p_c[0] + p_c[1] + p_c[2] + p_c[3]
    p_hi = psum.astype(BF16)
    p_lo = (psum - p_hi.astype(F32)).astype(BF16)
    imp = (lax.dot_general(ovt_ref[...], p_hi, nt, preferred_element_type=F32)
           + lax.dot_general(ovt_ref[...], p_lo, nt, preferred_element_type=F32))
    nidx = lax.broadcasted_iota(jnp.int32, (nsel, QB), 0)
    tpos_l = j * QB + lax.broadcasted_iota(jnp.int32, (1, QB), 1)
    blk_t = tpos_l // SLC_LEN
    valid = nidx * SLC_LEN <= tpos_l
    forced = (nidx == 0) | (nidx == blk_t) | (nidx == blk_t - 1)
    score = jnp.where(valid, imp + jnp.where(forced, FORCE_SELECT, 0.0), NEG_INF)
    nidx_f = nidx.astype(F32)
    sel_t = jnp.full((nsel, QB), NEG_INF, F32)
    for _ in range(min(NSA_N_SELECT, nsel)):
        best = jnp.max(score, axis=0, keepdims=True)
        first = jnp.min(jnp.where(score == best, nidx_f, float(nsel)), axis=0, keepdims=True)
        hit = nidx_f == first
        sel_t = jnp.where(hit, 0.0, sel_t)
        score = jnp.where(hit, -jnp.inf, score)
    sel_neg = sel_t.T.astype(BF16)

    for h in range(HPG):
        m_refs[h][...] = jnp.full((QB, LANES), -jnp.inf, F32)
        l_refs[h][...] = jnp.zeros((QB, LANES), F32)
        acc_refs[h][...] = jnp.zeros((QB, DH), F32)
    KT = SEL_TILE
    sub = KT // QB
    q_sel = jnp.concatenate([q, jnp.concatenate([sel_neg] * HPG, axis=0)], axis=1)

    def lane_fold(x, op):
        parts = [x[:, u * LANES:(u + 1) * LANES] for u in range(KT // LANES)]
        while len(parts) > 1:
            parts = [op(parts[a], parts[a + 1]) for a in range(0, len(parts), 2)]
        return parts[0]

    def score_step(t, carry):
        k0 = pl.multiple_of(t * KT, KT)
        k_sel = jnp.concatenate([ks_ref[0, 0, pl.ds(k0, KT), :], et_ref[pl.ds(k0, KT), :]], axis=1)
        s = lax.dot_general(q_sel, k_sel, nt, preferred_element_type=F32)
        for h in range(HPG):
            bias = jnp.concatenate([_dist_tile(bt_ref, h, j - t * sub - u) for u in range(sub)], axis=1)
            sh = s[h * QB:(h + 1) * QB, :] + bias
            sc_ref[t, h * QB:(h + 1) * QB, :] = sh
            m_refs[h][...] = jnp.maximum(m_refs[h][...], lane_fold(sh, jnp.maximum))
        return carry

    def value_step(t, carry):
        k0 = pl.multiple_of(t * KT, KT)
        v_t = vs_ref[0, 0, pl.ds(k0, KT), :]
        for h in range(HPG):
            mb = m_refs[h][...]
            p = jnp.exp2(sc_ref[t, h * QB:(h + 1) * QB, :] - jnp.concatenate([mb] * (KT // LANES), axis=1))
            l_refs[h][...] += lane_fold(p, jnp.add)
            acc_refs[h][...] += jnp.dot(p.astype(BF16), v_t, preferred_element_type=F32)
        return carry

    n_pairs = ((j * QB) // KT + 2) // 2

    def pair(step):
        def body(p, carry):
            return step(2 * p + 1, step(2 * p, carry))
        return body

    lax.fori_loop(0, n_pairs, pair(score_step), 0)
    for h in range(HPG):
        m_refs[h][...] = jnp.broadcast_to(jnp.max(m_refs[h][...], axis=-1, keepdims=True), (QB, LANES))
    lax.fori_loop(0, n_pairs, pair(value_step), 0)
    o_s = [acc_refs[h][...] / jnp.sum(l_refs[h][...], axis=-1, keepdims=True) for h in range(HPG)]

    WK = WINDOW + QB
    w0 = pl.multiple_of(j * QB, QB)
    k_w = kw_ref[0, 0, pl.ds(w0, WK), :]
    v_w = vw_ref[0, 0, pl.ds(w0, WK), :]
    s_w = lax.dot_general(q, k_w, nt, preferred_element_type=F32)
    nwt = WK // QB
    tab_w = [jnp.where(j + u >= nwt - 1, TAB_WINDOW_FAR if u == 0 else nwt - u, TAB_MASKED) for u in range(nwt)]
    o_w = []
    for h in range(HPG):
        bias = jnp.concatenate([bt_ref[0, h, tab_w[u]] for u in range(nwt)], axis=1)
        sh = s_w[h * QB:(h + 1) * QB, :] + bias
        mxw = jnp.max(sh, axis=-1, keepdims=True)
        pw = jnp.exp2(sh - mxw)
        lw = jnp.sum(pw, axis=-1, keepdims=True)
        o_w.append(jnp.dot(pw.astype(BF16), v_w, preferred_element_type=F32) / lw)

    gate = jax.nn.sigmoid(gt_ref[...])
    for h in range(HPG):
        gc = gate[:, 8 + 3 * h:9 + 3 * h]
        gs = gate[:, 9 + 3 * h:10 + 3 * h]
        gw = gate[:, 10 + 3 * h:11 + 3 * h]
        y = gc * o_c[h * QB:(h + 1) * QB, :] + gs * o_s[h] + gw * o_w[h]
        o_ref[:, h * DH:(h + 1) * DH] = y.astype(o_ref.dtype)


def nsa_tables(rel_bias, seq):
    def bias_of(dist):
        dist = jnp.maximum(dist, 0)
        max_exact = NUM_BUCKETS // 2
        log_ratio = jnp.log(jnp.maximum(dist, 1).astype(F32) / max_exact) / math.log(MAX_DISTANCE / max_exact)
        large = jnp.minimum(max_exact + (log_ratio * (NUM_BUCKETS - max_exact)).astype(jnp.int32), NUM_BUCKETS - 1)
        bucket = jnp.where(dist < max_exact, dist, large)
        onehot = (bucket[..., None] == jnp.arange(NUM_BUCKETS)).astype(F32)
        out = jnp.dot(onehot, rel_bias.astype(F32), precision=lax.Precision.HIGHEST)
        return jnp.moveaxis(out, -1, 0)

    assert seq // CMP_STRIDE >= LANES and Q_BLOCK * (N_DIST_TILES - 1) - (Q_BLOCK - 1) >= MAX_DISTANCE
    i = jnp.arange(Q_BLOCK)
    delta = i[None, :, None] - i[None, None, :]
    dist = jnp.arange(N_DIST_TILES)[:, None, None] * Q_BLOCK + delta
    tiles = bias_of(dist) * LOG2E
    causal = jnp.where(dist >= 0, tiles, NEG_INF)
    far = jnp.where(delta < 0, tiles[:, WINDOW // Q_BLOCK], NEG_INF)[:, None]
    masked = jnp.full_like(far, NEG_INF)
    bt = jnp.concatenate([masked, causal, far], axis=1).reshape(NSA_G, NSA_HPG, N_TAB, Q_BLOCK, Q_BLOCK)
    per_tile = Q_BLOCK // CMP_STRIDE
    assert (LANES - per_tile) * CMP_STRIDE - (CMP_LEN - 1) >= MAX_DISTANCE
    dist_c = i[:, None] - ((jnp.arange(LANES)[None, :] - (LANES - per_tile)) * CMP_STRIDE + CMP_LEN - 1)
    rel = (bias_of(dist_c) * LOG2E).reshape(NSA_G, NSA_HPG, Q_BLOCK, LANES)
    return bt, rel


def nsa_block_indicator(seq):
    nsel = seq // SLC_LEN
    return jnp.asarray(np.arange(seq)[:, None] // SLC_LEN == np.arange(nsel)[None, :], BF16)


def nsa_overlap(seq):
    ncp = seq // CMP_STRIDE
    ns = seq // SLC_LEN
    cstart = np.arange(ncp) * CMP_STRIDE
    sstart = np.arange(ns) * SLC_LEN
    ov = np.clip(np.minimum(cstart[:, None] + CMP_LEN, sstart[None, :] + SLC_LEN)
                 - np.maximum(cstart[:, None], sstart[None, :]), 0, None).astype(np.float32) / CMP_LEN
    ov[ncp - 1] = 0.0
    return jnp.asarray(ov.T, BF16)


def nsa_attend(z, pq, pkv, pw, cmp, bt, rel, ovt, et, bsz, seq):
    nqb = seq // Q_BLOCK
    ncp = seq // CMP_STRIDE
    nsel = seq // SLC_LEN
    R = NSA_HPG * Q_BLOCK
    assert (seq // SEL_TILE) % 2 == 0
    return pl.pallas_call(
        _nsa_kernel,
        grid=(bsz, NSA_G, nqb),
        in_specs=[pl.BlockSpec((1, NSA_HPG, Q_BLOCK, NSA_DH), lambda b, g, j: (b, g, j, 0)),
                  pl.BlockSpec((1, 1, 1, ncp, NSA_DH), lambda b, g, j: (0, b, g, 0, 0)),
                  pl.BlockSpec((1, 1, 1, ncp, NSA_DH), lambda b, g, j: (1, b, g, 0, 0)),
                  pl.BlockSpec((1, NSA_HPG, Q_BLOCK, LANES), lambda b, g, j: (g, 0, 0, 0)),
                  pl.BlockSpec((nsel, ncp), lambda b, g, j: (0, 0)),
                  pl.BlockSpec((1, 1, seq, NSA_DH), lambda b, g, j: (b, g, 0, 0)),
                  pl.BlockSpec((1, 1, seq, NSA_DH), lambda b, g, j: (b, NSA_G + g, 0, 0)),
                  pl.BlockSpec((seq, nsel), lambda b, g, j: (0, 0)),
                  pl.BlockSpec((1, 1, seq + WINDOW, NSA_DH), lambda b, g, j: (b, g, 0, 0)),
                  pl.BlockSpec((1, 1, seq + WINDOW, NSA_DH), lambda b, g, j: (b, NSA_G + g, 0, 0)),
                  pl.BlockSpec((1, NSA_HPG, N_TAB, Q_BLOCK, Q_BLOCK), lambda b, g, j: (g, 0, 0, 0, 0)),
                  pl.BlockSpec((Q_BLOCK, LANES), lambda b, g, j: (b * nqb + j, Z_SMALL // LANES + g))],
        out_specs=pl.BlockSpec((Q_BLOCK, NSA_HPG * NSA_DH), lambda b, g, j: (b * nqb + j, g)),
        out_shape=jax.ShapeDtypeStruct((bsz * seq, BRANCH_WIDTH), BF16),
        scratch_shapes=([pltpu.VMEM((seq // SEL_TILE, R, SEL_TILE), F32)]
                        + [pltpu.VMEM((Q_BLOCK, LANES), F32)] * (2 * NSA_HPG)
                        + [pltpu.VMEM((Q_BLOCK, NSA_DH), F32)] * NSA_HPG),
        compiler_params=_cp("arbitrary", "arbitrary", "arbitrary"),
        name="nsa_attend",
    )(pq, cmp, cmp, rel, ovt, pkv, pkv, et, pw, pw, bt, z)


def _merge_kernel(ya_ref, yb_ref, yc_ref, w_ref, ga_ref, gb_ref, gc_ref, o_ref):
    acc = jax.nn.sigmoid(ga_ref[...]) * jnp.dot(ya_ref[...], w_ref[0], preferred_element_type=F32)
    acc = acc + jax.nn.sigmoid(gb_ref[...]) * jnp.dot(yb_ref[...], w_ref[1], preferred_element_type=F32)
    acc = acc + jax.nn.sigmoid(gc_ref[...]) * jnp.dot(yc_ref[...], w_ref[2], preferred_element_type=F32)
    o_ref[...] = acc.astype(o_ref.dtype)


def merge(ya, yb, yc, w_branch, z, tm=1024, tn=512):
    n = ya.shape[0]
    d = w_branch.shape[2]
    bw = ya.shape[1]
    ymap = lambda i, j: (i, 0)
    gmap = lambda k: (lambda i, j: (i, (Z_GATE + k * d) // tn + j))
    return pl.pallas_call(
        _merge_kernel,
        grid=(n // tm, d // tn),
        in_specs=[pl.BlockSpec((tm, bw), ymap), pl.BlockSpec((tm, bw), ymap), pl.BlockSpec((tm, bw), ymap),
                  pl.BlockSpec((3, bw, tn), lambda i, j: (0, 0, j)),
                  pl.BlockSpec((tm, tn), gmap(0)), pl.BlockSpec((tm, tn), gmap(1)), pl.BlockSpec((tm, tn), gmap(2))],
        out_specs=pl.BlockSpec((tm, tn), lambda i, j: (i, j)),
        out_shape=jax.ShapeDtypeStruct((n, d), BF16),
        compiler_params=_cp("arbitrary", "arbitrary"),
        name="merge",
    )(ya, yb, yc, w_branch, z, z, z)


def _outproj_kernel(a_ref, w_ref, x_ref, g_ref, o_ref):
    o_ref[...] = x_ref[...] + g_ref[0] * jnp.dot(a_ref[...], w_ref[...], preferred_element_type=F32)


def out_proj(a, w, x2d, gate, seq, tm=1024, tn=512):
    n, k = a.shape
    d = w.shape[1]
    per_b = seq // tm
    return pl.pallas_call(
        _outproj_kernel,
        grid=(n // tm, d // tn),
        in_specs=[pl.BlockSpec((tm, k), lambda i, j: (i, 0)),
                  pl.BlockSpec((k, tn), lambda i, j: (0, j)),
                  pl.BlockSpec((tm, tn), lambda i, j: (i, j)),
                  pl.BlockSpec((1, 1, tn), lambda i, j: (i // per_b, 0, j))],
        out_specs=pl.BlockSpec((tm, tn), lambda i, j: (i, j)),
        out_shape=jax.ShapeDtypeStruct((n, d), F32),
        compiler_params=_cp("arbitrary", "arbitrary"),
        name="out_proj",
    )(a, w, x2d, gate[:, None, :])


def _pack_bf16_pairs(y):
    half = y.shape[1] // 2
    lo = lax.bitcast_convert_type(y[:, :half].astype(BF16).astype(F32), jnp.uint32)
    hi = lax.bitcast_convert_type(y[:, half:].astype(BF16).astype(F32), jnp.uint32)
    return lax.shift_right_logical(lo, jnp.uint32(16)) | (hi & jnp.uint32(0xFFFF0000))


def _unpack_bf16_pairs(w):
    lo = lax.bitcast_convert_type(lax.shift_left(w, jnp.uint32(16)), F32)
    hi = lax.bitcast_convert_type(w & jnp.uint32(0xFFFF0000), F32)
    return jnp.concatenate([lo, hi], axis=1)


def _router_kernel(x_ref, g_ref, sc_ref, sh_ref, rw_ref, rwl_ref, rb_ref, h_ref, idx_ref, p_ref):
    x = x_ref[...]
    y = x * lax.rsqrt(jnp.mean(x * x, axis=-1, keepdims=True) + EPS)
    hf = (y * g_ref[...]) * (1.0 + sc_ref[0]) + sh_ref[0]
    h = hf.astype(BF16)
    h_ref[...] = _pack_bf16_pairs(hf)
    h_lo = (hf - h.astype(F32)).astype(BF16)
    logits = (jnp.dot(h, rw_ref[...], preferred_element_type=F32)
              + jnp.dot(h_lo, rw_ref[...], preferred_element_type=F32)
              + jnp.dot(h, rwl_ref[...], preferred_element_type=F32)) + rb_ref[...]
    lane = lax.broadcasted_iota(jnp.int32, logits.shape, 1)
    lane_f = lane.astype(F32)
    idx_out = jnp.zeros(logits.shape, F32)
    val_out = jnp.full(logits.shape, -jnp.inf, F32)
    for k in range(TOP_K):
        best = jnp.max(logits, axis=-1, keepdims=True)
        first = jnp.min(jnp.where(logits == best, lane_f, float(LANES)), axis=-1, keepdims=True)
        idx_out = jnp.where(lane == k, first, idx_out)
        val_out = jnp.where(lane == k, best, val_out)
        logits = jnp.where(lane_f == first, -jnp.inf, logits)
    e = jnp.exp(val_out - jnp.max(val_out, axis=-1, keepdims=True))
    idx_ref[...] = idx_out.astype(jnp.int32)
    p_ref[...] = e / jnp.sum(e, axis=-1, keepdims=True)


def router(x2d, g, sc, sh, rw, rb, seq, tm=1024):
    n, d = x2d.shape
    per_b = seq // tm
    rw_hi = rw.astype(BF16)
    rw_lo = (rw - rw_hi.astype(F32)).astype(BF16)
    rwp = jnp.zeros((d, LANES), BF16).at[:, :N_EXPERTS].set(rw_hi)
    rwl = jnp.zeros((d, LANES), BF16).at[:, :N_EXPERTS].set(rw_lo)
    rbp = jnp.full((1, LANES), -jnp.inf, F32).at[0, :N_EXPERTS].set(rb)
    return pl.pallas_call(
        _router_kernel,
        grid=(n // tm,),
        in_specs=[pl.BlockSpec((tm, d), lambda i: (i, 0)),
                  pl.BlockSpec((1, d), lambda i: (0, 0)),
                  pl.BlockSpec((1, 1, d), lambda i: (i // per_b, 0, 0)),
                  pl.BlockSpec((1, 1, d), lambda i: (i // per_b, 0, 0)),
                  pl.BlockSpec((d, LANES), lambda i: (0, 0)),
                  pl.BlockSpec((d, LANES), lambda i: (0, 0)),
                  pl.BlockSpec((1, LANES), lambda i: (0, 0))],
        out_specs=[pl.BlockSpec((tm, d // 2), lambda i: (i, 0)),
                   pl.BlockSpec((tm, LANES), lambda i: (i, 0)),
                   pl.BlockSpec((tm, LANES), lambda i: (i, 0))],
        out_shape=[jax.ShapeDtypeStruct((n, d // 2), jnp.uint32),
                   jax.ShapeDtypeStruct((n, LANES), jnp.int32),
                   jax.ShapeDtypeStruct((n, LANES), F32)],
        compiler_params=_cp("arbitrary"),
        name="router",
    )(x2d, g.reshape(1, d), sc[:, None, :], sh[:, None, :], rwp, rwl, rbp)


def _dispatch_kernel(dest_ref, h_ref, init_ref, xb_ref, sem):
    del init_ref
    tm = h_ref.shape[0]

    def row_copy(k, r):
        dst = dest_ref[0, 0, k * tm + r]
        return pltpu.make_async_copy(h_ref.at[pl.ds(r, 1), :], xb_ref.at[pl.ds(dst, 1), :], sem)

    def start(r, carry):
        for k in range(TOP_K):
            row_copy(k, r).start()
        return carry

    def wait(r, carry):
        for k in range(TOP_K):
            row_copy(k, r).wait()
        return carry

    lax.fori_loop(0, tm, start, 0, unroll=8)
    lax.fori_loop(0, tm, wait, 0, unroll=8)


def dispatch(hp, dest_tiles, rows):
    n, w = hp.shape
    nb, _, per_tile = dest_tiles.shape
    tm = per_tile // TOP_K
    return pl.pallas_call(
        _dispatch_kernel,
        grid=(nb,),
        in_specs=[pl.BlockSpec((1, 1, per_tile), lambda i: (i, 0, 0), memory_space=pltpu.SMEM),
                  pl.BlockSpec((tm, w), lambda i: (i, 0)),
                  pl.BlockSpec(memory_space=pl.ANY)],
        out_specs=pl.BlockSpec(memory_space=pl.ANY),
        out_shape=jax.ShapeDtypeStruct((rows, w), jnp.uint32),
        scratch_shapes=[pltpu.SemaphoreType.DMA(())],
        input_output_aliases={2: 0},
        compiler_params=_cp("arbitrary"),
        name="dispatch",
    )(dest_tiles, hp, jnp.zeros((rows, w), jnp.uint32))


def _expert_kernel(nf, be_ref, nv_ref, x_ref, wg_ref, wu_ref, bg_ref, bu_ref, w2a_ref, w2b_ref, b2a_ref, b2b_ref,
                   o_ref, xs_ref, act_ref):
    i = pl.program_id(0)
    s = pl.program_id(1)
    live = i < nv_ref[0]

    @pl.when(jnp.logical_and(live, s == 0))
    def _():
        xs_ref[...] = _unpack_bf16_pairs(x_ref[...]).astype(BF16)

    @pl.when(jnp.logical_and(live, s < nf))
    def _():
        x = xs_ref[...]
        gate = jnp.dot(x, wg_ref[0, 0].astype(BF16), preferred_element_type=F32) + bg_ref[0, 0]
        up = jnp.dot(x, wu_ref[0, 0].astype(BF16), preferred_element_type=F32) + bu_ref[0, 0]
        gate = jnp.minimum(gate, SWIGLU_LIMIT)
        up = jnp.clip(up, -SWIGLU_LIMIT, SWIGLU_LIMIT)
        act = (up + 1.0) * (gate * jax.nn.sigmoid(SWIGLU_ALPHA * gate))
        act_ref[jnp.minimum(s, nf - 1)] = act.astype(BF16)

    @pl.when(jnp.logical_and(live, s >= nf))
    def _():
        act = jnp.concatenate([act_ref[f] for f in range(nf)], axis=1)
        ya = jnp.dot(act, w2a_ref[0, 0].astype(BF16), preferred_element_type=F32) + b2a_ref[0, 0]
        yb = jnp.dot(act, w2b_ref[0, 0].astype(BF16), preferred_element_type=F32) + b2b_ref[0, 0]
        o_ref[...] = _pack_bf16_pairs(jnp.concatenate([ya, yb], axis=1))

    @pl.when(jnp.logical_and(jnp.logical_not(live), s >= nf))
    def _():
        o_ref[...] = jnp.zeros(o_ref.shape, o_ref.dtype)


def experts(xb, block_exp, n_valid, w1, b1, w2, b2, layer, tm, tf=256):
    rows = xb.shape[0]
    d = 2 * xb.shape[1]
    nf = D_FF // tf
    half = d // 2
    nout = half // tf
    nblk = rows // tm
    n_exp = w1.shape[1]
    fe = lambda i, s, nv: jnp.where(i < nv[0], jnp.minimum(s, nf - 1), nf - 1)
    ne = lambda i, s, nv: jnp.where(i < nv[0], jnp.maximum(s - nf, 0), nout - 1)
    grid_spec = pltpu.PrefetchScalarGridSpec(
        num_scalar_prefetch=2,
        grid=(nblk, nf + nout),
        in_specs=[pl.BlockSpec((tm, half), lambda i, s, be, nv: (jnp.minimum(i, nv[0] - 1), 0)),
                  pl.BlockSpec((1, 1, d, tf), lambda i, s, be, nv: (layer, be[i], 0, fe(i, s, nv))),
                  pl.BlockSpec((1, 1, d, tf), lambda i, s, be, nv: (layer, be[i], 0, nf + fe(i, s, nv))),
                  pl.BlockSpec((1, 1, 1, tf), lambda i, s, be, nv: (layer * n_exp + be[i], 0, 0, fe(i, s, nv))),
                  pl.BlockSpec((1, 1, 1, tf), lambda i, s, be, nv: (layer * n_exp + be[i], 0, 0, nf + fe(i, s, nv))),
                  pl.BlockSpec((1, 1, D_FF, tf), lambda i, s, be, nv: (layer, be[i], 0, ne(i, s, nv))),
                  pl.BlockSpec((1, 1, D_FF, tf), lambda i, s, be, nv: (layer, be[i], 0, nout + ne(i, s, nv))),
                  pl.BlockSpec((1, 1, 1, tf), lambda i, s, be, nv: (layer * n_exp + be[i], 0, 0, ne(i, s, nv))),
                  pl.BlockSpec((1, 1, 1, tf), lambda i, s, be, nv: (layer * n_exp + be[i], 0, 0, nout + ne(i, s, nv)))],
        out_specs=pl.BlockSpec((tm, tf), lambda i, s, be, nv: (i, jnp.maximum(s - nf, 0))),
        scratch_shapes=[pltpu.VMEM((tm, d), BF16), pltpu.VMEM((nf, tm, tf), BF16)],
    )
    b1r = b1.reshape(-1, 1, 1, b1.shape[-1])
    b2r = b2.reshape(-1, 1, 1, d)
    return pl.pallas_call(
        functools.partial(_expert_kernel, nf),
        grid_spec=grid_spec,
        out_shape=jax.ShapeDtypeStruct((rows, half), jnp.uint32),
        compiler_params=_cp("arbitrary", "arbitrary"),
        name="experts",
    )(block_exp, n_valid, xb, w1, w1, b1r, b1r, w2, w2, b2r, b2r)


def _combine_kernel(dest_ref, p_ref, x_ref, g_ref, yb_ref, o_ref, ybuf_ref, sem):
    tm = x_ref.shape[0]

    def row_copy(k, r):
        src = dest_ref[0, 0, k * tm + r]
        return pltpu.make_async_copy(yb_ref.at[pl.ds(src, 1), :], ybuf_ref.at[k, pl.ds(r, 1), :], sem)

    def start(r, carry):
        for k in range(TOP_K):
            row_copy(k, r).start()
        return carry

    def wait(r, carry):
        for k in range(TOP_K):
            row_copy(k, r).wait()
        return carry

    lax.fori_loop(0, tm, start, 0, unroll=8)
    lax.fori_loop(0, tm, wait, 0, unroll=8)
    p = p_ref[...]
    y = p[:, 0:1] * _unpack_bf16_pairs(ybuf_ref[0])
    for k in range(1, TOP_K):
        y = y + p[:, k:k + 1] * _unpack_bf16_pairs(ybuf_ref[k])
    o_ref[...] = x_ref[...] + g_ref[0] * y


def combine(yb, dest_tiles, prob, x2d, gate, seq):
    n, d = x2d.shape
    nb = dest_tiles.shape[0]
    tm = n // nb
    per_b = seq // tm
    return pl.pallas_call(
        _combine_kernel,
        grid=(nb,),
        in_specs=[pl.BlockSpec((1, 1, TOP_K * tm), lambda i: (i, 0, 0), memory_space=pltpu.SMEM),
                  pl.BlockSpec((tm, LANES), lambda i: (i, 0)),
                  pl.BlockSpec((tm, d), lambda i: (i, 0)),
                  pl.BlockSpec((1, 1, d), lambda i: (i // per_b, 0, 0)),
                  pl.BlockSpec(memory_space=pl.ANY)],
        out_specs=pl.BlockSpec((tm, d), lambda i: (i, 0)),
        out_shape=jax.ShapeDtypeStruct((n, d), F32),
        scratch_shapes=[pltpu.VMEM((TOP_K, tm, d // 2), jnp.uint32), pltpu.SemaphoreType.DMA(())],
        compiler_params=_cp("arbitrary"),
        name="combine",
    )(dest_tiles, prob, x2d, gate[:, None, :], yb)


def moe(x2d, g, sc, sh, gate, rw, rb, w1, b1, w2, b2, layer, seq, tm_e=1024, tm_t=512):
    n, d = x2d.shape
    h, idx, prob = router(x2d, g, sc, sh, rw, rb, seq)
    e_flat = idx[:, :TOP_K].T.reshape(-1)
    na = n * TOP_K
    onehot = (e_flat[:, None] == jnp.arange(N_EXPERTS)[None, :]).astype(jnp.int32)
    csum = jnp.cumsum(onehot, axis=0)
    counts = csum[-1]
    rank = jnp.take_along_axis(csum, e_flat[:, None], axis=1)[:, 0] - 1
    padded = (counts + tm_e - 1) // tm_e * tm_e
    pend = jnp.cumsum(padded)
    pstart = pend - padded
    dest = pstart[e_flat] + rank
    nblk = na // tm_e + N_EXPERTS
    rows = nblk * tm_e
    starts = jnp.arange(nblk, dtype=jnp.int32) * tm_e
    block_exp = jnp.minimum(jnp.sum(pend[None, :] <= starts[:, None], axis=1), N_EXPERTS - 1).astype(jnp.int32)
    n_valid = (pend[-1:] // tm_e).astype(jnp.int32)
    block_exp = jnp.where(starts < pend[-1], block_exp, block_exp[jnp.maximum(n_valid[0] - 1, 0)])
    nb = n // tm_t
    dest_tiles = dest.reshape(TOP_K, nb, tm_t).transpose(1, 0, 2).reshape(nb, 1, TOP_K * tm_t)
    xb = dispatch(h, dest_tiles, rows)
    yb = experts(xb, block_exp, n_valid, w1, b1, w2, b2, layer, tm_e)
    return combine(yb, dest_tiles, prob, x2d, gate, seq)


def fused_in_weight(w_in, w_gate):
    sizes = (512, 512, 1024, 4, 4, 1024, 1024, 256, 256, 256, 256, 256, 256, 24, 1024, 1024)
    pts = np.cumsum(sizes)[:-1].tolist()
    (qm, km, vm, ig, fg, og, qn, kc, vc, ks, vs, kw, vw, gn, gu, gv) = jnp.split(w_in, pts, axis=1)
    d = w_in.shape[0]
    pad = lambda k: jnp.zeros((d, k), w_in.dtype)
    per_group = NSA_HPG * 3
    small0 = jnp.concatenate([ig, fg, gn[:, :per_group], pad(LANES - 8 - per_group)], axis=1)
    small1 = jnp.concatenate([pad(8), gn[:, per_group:], pad(LANES - 8 - per_group)], axis=1)
    cols = [qm, km, vm, og, qn, gu, gv, kc, vc, ks, vs, kw, vw, w_gate[0], w_gate[1], w_gate[2], small0, small1]
    w = jnp.concatenate(cols, axis=1)
    w = jnp.concatenate([w, pad(Z_WIDTH - w.shape[1])], axis=1)
    return w.astype(BF16)


def kernel(x, c, ada_w, ada_b, norm1_g, norm2_g, w_in, conv_w, conv_b, mlstm_gate_b, mlstm_norm_g, cmp_pos,
           cmp_k_w1, cmp_k_w2, cmp_v_w1, cmp_v_w2, qnorm_g, knorm_g, rel_bias, gmlp_norm_g, gmlp_ws, gmlp_b,
           w_branch, w_gate, w_out, router_w, router_b, exp_w1, exp_b1, exp_w2, exp_b2):
    bsz, seq, d = x.shape
    depth = ada_w.shape[0]
    x2d = x.reshape(bsz * seq, d)
    bt, rel = nsa_tables(rel_bias, seq)
    ovt = nsa_overlap(seq)
    et = nsa_block_indicator(seq)
    for l in range(depth):
        mod = ada_mod(c, ada_w, ada_b, l)
        sh1, sc1, g1, sh2, sc2, g2 = jnp.split(mod, 6, axis=-1)
        z = in_proj(x2d, norm1_g[l], sc1, sh1, fused_in_weight(w_in[l], w_gate[l]), seq)
        ya = mlstm(z, bsz, seq, conv_w[l], conv_b[l], mlstm_gate_b[l], mlstm_norm_g[l])
        pq, pkv, pw = nsa_prep(z, bsz, seq, qnorm_g[l], knorm_g[l])
        cmp = nsa_compress(z, bsz, seq, cmp_pos[l], cmp_k_w1[l], cmp_k_w2[l], cmp_v_w1[l], cmp_v_w2[l], knorm_g[l])
        yb = nsa_attend(z, pq, pkv, pw, cmp, bt, rel, ovt, et, bsz, seq)
        yc = gmlp(z, gmlp_norm_g[l], gmlp_ws[l], gmlp_b[l])
        merged = merge(ya, yb, yc, w_branch[l].astype(BF16), z)
        x2d = out_proj(merged, w_out[l].astype(BF16), x2d, g1, seq)
        x2d = moe(x2d, norm2_g[l], sc2, sh2, g2, router_w[l], router_b[l], exp_w1, exp_b1, exp_w2, exp_b2, l, seq)
    return x2d.reshape(bsz, seq, d)
```

```python
import functools
import math

import jax
import jax.numpy as jnp
import numpy as np
from jax import lax
from jax.experimental import pallas as pl
from jax.experimental.pallas import tpu as pltpu

F32 = jnp.float32
BF16 = jnp.bfloat16

LANES = 128
VMEM_LIMIT = 56 * 1024 * 1024

D_MODEL = 2048
BRANCH_WIDTH = 1024
MLSTM_HEADS = 4
MLSTM_DV = 256
MLSTM_DQK = 128
MLSTM_TILE = 256
CONV_WIDTH = 4
NSA_HEADS = 8
NSA_G = 2
NSA_HPG = 4
NSA_DH = 128
CMP_LEN = 32
CMP_STRIDE = 16
SLC_LEN = 64
NSA_N_SELECT = 16
WINDOW = 512
Q_BLOCK = 128
NUM_BUCKETS = 32
MAX_DISTANCE = 1024
GMLP_CHUNK = 128
GMLP_GROUPS = 8
N_EXPERTS = 32
TOP_K = 4
D_FF = 1536
SWIGLU_LIMIT = 7.0
SWIGLU_ALPHA = 1.702
NEG_INF = -1e30
FORCE_SELECT = 1e4
EPS = 1e-6

LOG2E = math.log2(math.e)
SEL_TILE = 512
N_DIST_TILES = 10
TAB_MASKED, TAB_WINDOW_FAR, N_TAB = 0, N_DIST_TILES + 1, N_DIST_TILES + 2

Z_QK, Z_V, Z_OG, Z_NQ, Z_GU, Z_GV = 0, 1024, 2048, 3072, 4096, 5120
Z_KV = 6144
Z_GATE = 7680
Z_SMALL = 13824
Z_WIDTH = 14336


def _cp(*sem):
    return pltpu.CompilerParams(dimension_semantics=sem, vmem_limit_bytes=VMEM_LIMIT)


def _ada_kernel(c_ref, w_ref, b_ref, o_ref):
    a = jax.nn.silu(c_ref[...]).astype(BF16)
    o_ref[...] = jnp.dot(a, w_ref[0].astype(BF16), preferred_element_type=F32) + b_ref[0]


def ada_mod(c, w, b, layer):
    bsz, d = c.shape
    n = w.shape[2]
    tn = 1024
    cp = jnp.zeros((8, d), F32).at[:bsz].set(c)
    out = pl.pallas_call(
        _ada_kernel,
        grid=(n // tn,),
        in_specs=[pl.BlockSpec((8, d), lambda j: (0, 0)),
                  pl.BlockSpec((1, d, tn), lambda j: (layer, 0, j)),
                  pl.BlockSpec((1, 1, tn), lambda j: (layer, 0, j))],
        out_specs=pl.BlockSpec((8, tn), lambda j: (0, j)),
        out_shape=jax.ShapeDtypeStruct((8, n), F32),
        compiler_params=_cp("arbitrary"),
        name="ada_mod",
    )(cp, w, b[:, None, :])
    return out[:bsz]


def _in_kernel(x_ref, g_ref, sc_ref, sh_ref, w_ref, o_ref, h_ref):
    @pl.when(pl.program_id(1) == 0)
    def _():
        x = x_ref[...]
        y = x * lax.rsqrt(jnp.mean(x * x, axis=-1, keepdims=True) + EPS)
        h = (y * g_ref[...]) * (1.0 + sc_ref[0]) + sh_ref[0]
        h_ref[...] = h.astype(BF16)

    o_ref[...] = jnp.dot(h_ref[...], w_ref[...], preferred_element_type=F32)


def in_proj(x2d, g, sc, sh, wcat, seq, tm=1024, tn=1024):
    n, d = x2d.shape
    w = wcat.shape[1]
    per_b = seq // tm
    return pl.pallas_call(
        _in_kernel,
        grid=(n // tm, w // tn),
        in_specs=[pl.BlockSpec((tm, d), lambda i, j: (i, 0)),
                  pl.BlockSpec((1, d), lambda i, j: (0, 0)),
                  pl.BlockSpec((1, 1, d), lambda i, j: (i // per_b, 0, 0)),
                  pl.BlockSpec((1, 1, d), lambda i, j: (i // per_b, 0, 0)),
                  pl.BlockSpec((d, tn), lambda i, j: (0, j))],
        out_specs=pl.BlockSpec((tm, tn), lambda i, j: (i, j)),
        out_shape=jax.ShapeDtypeStruct((n, w), F32),
        scratch_shapes=[pltpu.VMEM((tm, d), BF16)],
        compiler_params=_cp("arbitrary", "arbitrary"),
        name="in_proj",
    )(x2d, g.reshape(1, d), sc[:, None, :], sh[:, None, :], wcat)


def _cumsum_rows(x, n):
    row = lax.broadcasted_iota(jnp.int32, x.shape, 0)
    s = 1
    while s < n:
        x = x + jnp.where(row >= s, pltpu.roll(x, s, 0), 0.0)
        s *= 2
    return x


def _mlstm_kernel(qk_ref, v_ref, og_ref, sm_ref, cw_ref, cb_ref, gb_ref, ng_ref, o_ref,
                  buf_ref, ct_ref, n_ref, m_ref):
    L = qk_ref.shape[0]
    H, DQK, DV = MLSTM_HEADS, MLSTM_DQK, MLSTM_DV

    @pl.when(pl.program_id(1) == 0)
    def _():
        buf_ref[0:8, :] = jnp.zeros((8, buf_ref.shape[1]), F32)
        ct_ref[...] = jnp.zeros(ct_ref.shape, F32)
        n_ref[...] = jnp.zeros(n_ref.shape, F32)
        m_ref[...] = jnp.full(m_ref.shape, NEG_INF, F32)

    buf_ref[8:8 + L, :] = qk_ref[...]
    conv = cb_ref[...] + cw_ref[3:4, :] * buf_ref[8:8 + L, :]
    for j in range(CONV_WIDTH - 1):
        conv = conv + cw_ref[j:j + 1, :] * buf_ref[5 + j:5 + j + L, :]
    qk = jax.nn.silu(conv)
    buf_ref[0:8, :] = buf_ref[L:L + 8, :]

    sm = sm_ref[...] + gb_ref[...]
    bcum = _cumsum_rows(jax.nn.log_sigmoid(sm), L)
    sm_t = sm.T
    bcum_t = bcum.T
    row = lax.broadcasted_iota(jnp.int32, (L, L), 0)
    col = lax.broadcasted_iota(jnp.int32, (L, L), 1)
    causal = row >= col

    for h in range(H):
        qf = qk[:, h * DQK:(h + 1) * DQK] * DQK ** -0.5
        q = qf.astype(BF16)
        kf = qk[:, (H + h) * DQK:(H + h + 1) * DQK]
        k = kf.astype(BF16)
        v = v_ref[:, h * DV:(h + 1) * DV].astype(BF16)
        b_col = bcum[:, H + h:H + h + 1]
        i_col = sm[:, h:h + 1]
        b_row = bcum_t[H + h:H + h + 1, :]
        i_row = sm_t[h:h + 1, :]
        m_prev = m_ref[h:h + 1, 0:1]
        a = b_col + m_prev
        dmat = jnp.where(causal, b_col - b_row + i_row, -jnp.inf)
        m_t = jnp.maximum(a, jnp.max(dmat, axis=1, keepdims=True))
        w_intra = jnp.exp(dmat - m_t)
        w_inter = jnp.exp(a - m_t)
        s = lax.dot_general(q, k, (((1,), (1,)), ((), ())), preferred_element_type=F32) * w_intra
        ct = ct_ref[h]
        nvec = n_ref[h:h + 1, :]
        num = jnp.dot(s.astype(BF16), v, preferred_element_type=F32) + w_inter * jnp.dot(
            q, ct.astype(BF16), preferred_element_type=F32)
        qn = jnp.sum(qf * nvec, axis=1, keepdims=True)
        den = jnp.sum(s, axis=1, keepdims=True) + w_inter * qn
        hm = num / jnp.maximum(jnp.abs(den), jnp.exp(-m_t))
        b_last = b_col[L - 1:L, :]
        gdec = b_last - b_col + i_col
        m_new = jnp.maximum(b_last + m_prev, jnp.max(gdec, axis=0, keepdims=True))
        ws = jnp.exp(gdec - m_new)
        wc = jnp.exp(b_last + m_prev - m_new)
        kw = ws * kf
        ct_ref[h] = wc * ct + jnp.dot(kw.T.astype(BF16), v, preferred_element_type=F32)
        n_ref[h:h + 1, :] = wc * nvec + jnp.sum(kw, axis=0, keepdims=True)
        m_ref[h:h + 1, :] = jnp.broadcast_to(m_new, (1, LANES))
        y = hm * lax.rsqrt(jnp.mean(hm * hm, axis=-1, keepdims=True) + EPS) * ng_ref[:, h * DV:(h + 1) * DV]
        og = og_ref[:, h * DV:(h + 1) * DV]
        o_ref[:, h * DV:(h + 1) * DV] = (jax.nn.sigmoid(og) * y).astype(o_ref.dtype)


def mlstm(z, bsz, seq, conv_w, conv_b, gate_b, norm_g):
    L = MLSTM_TILE
    nch = seq // L
    gb = jnp.zeros((1, LANES), F32).at[0, :2 * MLSTM_HEADS].set(gate_b)
    wide = BRANCH_WIDTH
    rowmap = lambda col: (lambda b, c: (b * nch + c, col))
    const = lambda b, c: (0, 0)
    return pl.pallas_call(
        _mlstm_kernel,
        grid=(bsz, nch),
        in_specs=[pl.BlockSpec((L, wide), rowmap(Z_QK // wide)),
                  pl.BlockSpec((L, wide), rowmap(Z_V // wide)),
                  pl.BlockSpec((L, wide), rowmap(Z_OG // wide)),
                  pl.BlockSpec((L, LANES), rowmap(Z_SMALL // LANES)),
                  pl.BlockSpec((CONV_WIDTH, wide), const),
                  pl.BlockSpec((1, wide), const),
                  pl.BlockSpec((1, LANES), const),
                  pl.BlockSpec((1, wide), const)],
        out_specs=pl.BlockSpec((L, wide), lambda b, c: (b * nch + c, 0)),
        out_shape=jax.ShapeDtypeStruct((bsz * seq, wide), BF16),
        scratch_shapes=[pltpu.VMEM((L + 8, wide), F32),
                        pltpu.VMEM((MLSTM_HEADS, MLSTM_DQK, MLSTM_DV), F32),
                        pltpu.VMEM((8, MLSTM_DQK), F32),
                        pltpu.VMEM((8, LANES), F32)],
        compiler_params=_cp("arbitrary", "arbitrary"),
        name="mlstm",
    )(z, z, z, z, conv_w, conv_b.reshape(1, wide), gb, norm_g.reshape(1, wide))


def _gmlp_kernel(u_ref, v_ref, ng_ref, ws_ref, b_ref, o_ref):
    T = u_ref.shape[0]
    C = GMLP_CHUNK
    u = jax.nn.gelu(u_ref[...])
    vg = jax.nn.gelu(v_ref[...])
    mu = jnp.mean(vg, axis=-1, keepdims=True)
    var = jnp.mean(jnp.square(vg - mu), axis=-1, keepdims=True)
    v = ((vg - mu) * lax.rsqrt(var + EPS) * ng_ref[...]).astype(BF16)
    row = lax.broadcasted_iota(jnp.int32, (C, C), 0)
    col = lax.broadcasted_iota(jnp.int32, (C, C), 1)
    for g in range(GMLP_GROUPS):
        wsc = jnp.where(row >= col, ws_ref[g], 0.0).astype(BF16)
        bcol = b_ref[:, g:g + 1]
        for n in range(T // C):
            vv = v[n * C:(n + 1) * C, g * C:(g + 1) * C]
            s = jnp.dot(wsc, vv, preferred_element_type=F32) + bcol
            o_ref[n * C:(n + 1) * C, g * C:(g + 1) * C] = (
                u[n * C:(n + 1) * C, g * C:(g + 1) * C] * s).astype(o_ref.dtype)


def gmlp(z, norm_g, ws, b, tg=512):
    n = z.shape[0]
    wide = BRANCH_WIDTH
    return pl.pallas_call(
        _gmlp_kernel,
        grid=(n // tg,),
        in_specs=[pl.BlockSpec((tg, wide), lambda i: (i, Z_GU // wide)),
                  pl.BlockSpec((tg, wide), lambda i: (i, Z_GV // wide)),
                  pl.BlockSpec((1, wide), lambda i: (0, 0)),
                  pl.BlockSpec((GMLP_GROUPS, GMLP_CHUNK, GMLP_CHUNK), lambda i: (0, 0, 0)),
                  pl.BlockSpec((GMLP_CHUNK, GMLP_GROUPS), lambda i: (0, 0))],
        out_specs=pl.BlockSpec((tg, wide), lambda i: (i, 0)),
        out_shape=jax.ShapeDtypeStruct((n, wide), BF16),
        compiler_params=_cp("arbitrary"),
        name="gmlp",
    )(z, z, norm_g.reshape(1, wide), ws, b.T)


def _prep_kernel(n_norm, npad, x_ref, g_ref, o_ref):
    nh = o_ref.shape[1]

    @pl.when(pl.program_id(1) < npad)
    def _():
        o_ref[...] = jnp.zeros(o_ref.shape, o_ref.dtype)

    @pl.when(pl.program_id(1) >= npad)
    def _():
        for h in range(nh):
            x = x_ref[:, h * LANES:(h + 1) * LANES]
            if h < n_norm:
                x = x * lax.rsqrt(jnp.mean(x * x, axis=-1, keepdims=True) + EPS) * g_ref[h]
            o_ref[0, h] = x.astype(o_ref.dtype)


def _prep_call(z, bsz, seq, col0, heads, n_norm, gains, pad, ts, name):
    width = heads * LANES
    nblk = seq // ts
    npad = pad // ts
    return pl.pallas_call(
        functools.partial(_prep_kernel, n_norm, npad),
        grid=(bsz, nblk + npad),
        in_specs=[pl.BlockSpec((ts, width), lambda b, s: (b * nblk + jnp.maximum(s - npad, 0), col0 // width)),
                  pl.BlockSpec((heads, 1, LANES), lambda b, s: (0, 0, 0))],
        out_specs=pl.BlockSpec((1, heads, ts, LANES), lambda b, s: (b, 0, s, 0)),
        out_shape=jax.ShapeDtypeStruct((bsz, heads, seq + pad, LANES), BF16),
        compiler_params=_cp("arbitrary", "arbitrary"),
        name=name,
    )(z, gains)


def nsa_prep(z, bsz, seq, qnorm_g, knorm_g):
    gq = jnp.tile((qnorm_g * (NSA_DH ** -0.5 * LOG2E)).reshape(1, 1, LANES), (NSA_HEADS, 1, 1))
    gk = jnp.tile(knorm_g.reshape(1, 1, LANES), (2 * NSA_G, 1, 1))
    pq = _prep_call(z, bsz, seq, Z_NQ, NSA_HEADS, NSA_HEADS, gq, 0, 1024, "nsa_prep_q")
    pkv = _prep_call(z, bsz, seq, Z_KV + 512, 2 * NSA_G, NSA_G, gk, 0, 1024, "nsa_prep_sel")
    pw = _prep_call(z, bsz, seq, Z_KV + 1024, 2 * NSA_G, NSA_G, gk, WINDOW, WINDOW, "nsa_prep_win")
    return pq, pkv, pw


def _compress_kernel(x_ref, pos_ref, w1_ref, w2_ref, g_ref, o_ref):
    half = x_ref.shape[-1]
    x = x_ref[0, 0, 0]
    lo = jnp.dot((x + pos_ref[0, :, 0:half]).astype(BF16), w1_ref[0, 0:half, :], preferred_element_type=F32)
    hi = jnp.dot((x + pos_ref[0, :, half:2 * half]).astype(BF16), w1_ref[0, half:2 * half, :],
                 preferred_element_type=F32)
    ncp = x.shape[0]
    pre = lo + pltpu.roll(hi, ncp - 1, 0)
    y = jnp.dot(jax.nn.gelu(pre).astype(BF16), w2_ref[0], preferred_element_type=F32)
    yn = y * lax.rsqrt(jnp.mean(y * y, axis=-1, keepdims=True) + EPS) * g_ref[...]
    o_ref[0, 0, 0] = jnp.where(pl.program_id(0) == 0, yn, y).astype(o_ref.dtype)


def nsa_compress(z, bsz, seq, cmp_pos, w1k, w2k, w1v, w2v, knorm_g):
    ncp = seq // CMP_STRIDE
    kv = z[:, Z_KV:Z_KV + 512].reshape(bsz, seq, 2, NSA_G, NSA_DH)
    kv = kv.transpose(2, 0, 3, 1, 4).reshape(2, bsz, NSA_G, ncp, CMP_STRIDE * NSA_DH)
    pos = cmp_pos.reshape(1, 1, CMP_LEN * NSA_DH)
    w1 = jnp.stack([w1k, w1v]).astype(BF16)
    w2 = jnp.stack([w2k, w2v]).astype(BF16)
    kdim = CMP_STRIDE * NSA_DH
    return pl.pallas_call(
        _compress_kernel,
        grid=(2, bsz, NSA_G),
        in_specs=[pl.BlockSpec((1, 1, 1, ncp, kdim), lambda t, b, g: (t, b, g, 0, 0)),
                  pl.BlockSpec((1, 1, 2 * kdim), lambda t, b, g: (0, 0, 0)),
                  pl.BlockSpec((1, 2 * kdim, NSA_DH), lambda t, b, g: (t, 0, 0)),
                  pl.BlockSpec((1, NSA_DH, NSA_DH), lambda t, b, g: (t, 0, 0)),
                  pl.BlockSpec((1, NSA_DH), lambda t, b, g: (0, 0))],
        out_specs=pl.BlockSpec((1, 1, 1, ncp, NSA_DH), lambda t, b, g: (t, b, g, 0, 0)),
        out_shape=jax.ShapeDtypeStruct((2, bsz, NSA_G, ncp, NSA_DH), BF16),
        compiler_params=_cp("arbitrary", "arbitrary", "arbitrary"),
        name="nsa_compress",
    )(kv, pos, w1, w2, knorm_g.reshape(1, NSA_DH))


def _dist_tile(bt_ref, h, rho):
    return bt_ref[0, h, jnp.clip(rho, -1, N_DIST_TILES - 1) + 1]


def _nsa_kernel(q_ref, kc_ref, vc_ref, rel_ref, ovt_ref, ks_ref, vs_ref, et_ref, kw_ref, vw_ref, bt_ref, gt_ref,
                o_ref, sc_ref, *state):
    m_refs, l_refs, acc_refs = state[0:NSA_HPG], state[NSA_HPG:2 * NSA_HPG], state[2 * NSA_HPG:3 * NSA_HPG]
    j = pl.program_id(2)
    QB, HPG, DH = Q_BLOCK, NSA_HPG, NSA_DH
    R = HPG * QB
    ncp = kc_ref.shape[3]
    nsel = ovt_ref.shape[0]
    q = q_ref[0].reshape(R, DH)
    nt = (((1,), (1,)), ((), ()))
    qi = lax.broadcasted_iota(jnp.int32, (QB, 1), 0)
    tpos = j * QB + qi

    cidx = lax.broadcasted_iota(jnp.int32, (QB, ncp), 1)
    mask_c = jnp.logical_and(cidx * CMP_STRIDE + (CMP_LEN - 1) <= tpos, cidx < ncp - 1)
    band0 = (j * (QB // CMP_STRIDE) + ncp - (LANES - QB // CMP_STRIDE)) % ncp
    s_c = lax.dot_general(q, kc_ref[0, 0, 0], nt, preferred_element_type=F32)
    p_c = []
    for h in range(HPG):
        sat = bt_ref[0, h, N_DIST_TILES]
        bias = jnp.concatenate([rel_ref[0, h]] + [sat] * (ncp // LANES - 1), axis=1)
        bias = pltpu.roll(bias, band0, 1)
        sh = jnp.where(mask_c, s_c[h * QB:(h + 1) * QB, :] + bias, NEG_INF)
        mx = jnp.max(sh, axis=-1, keepdims=True)
        e = jnp.where(mask_c, jnp.exp2(sh - mx), 0.0)
        den = jnp.sum(e, axis=-1, keepdims=True)
        p_c.append(e * jnp.where(den > 0.0, 1.0 / den, 0.0))
    o_c = jnp.dot(jnp.concatenate(p_c, axis=0).astype(BF16), vc_ref[0, 0, 0], preferred_element_type=F32)

    psum = p_c[0] + p_c[1] + p_c[2] + p_c[3]
    p_hi = psum.astype(BF16)
    p_lo = (psum - p_hi.astype(F32)).astype(BF16)
    imp = (lax.dot_general(ovt_ref[...], p_hi, nt, preferred_element_type=F32)
           + lax.dot_general(ovt_ref[...], p_lo, nt, preferred_element_type=F32))
    nidx = lax.broadcasted_iota(jnp.int32, (nsel, QB), 0)
    tpos_l = j * QB + lax.broadcasted_iota(jnp.int32, (1, QB), 1)
    blk_t = tpos_l // SLC_LEN
    valid = nidx * SLC_LEN <= tpos_l
    forced = (nidx == 0) | (nidx == blk_t) | (nidx == blk_t - 1)
    score = jnp.where(valid, imp + jnp.where(forced, FORCE_SELECT, 0.0), NEG_INF)
    nidx_f = nidx.astype(F32)
    sel_t = jnp.full((nsel, QB), NEG_INF, F32)
    for _ in range(min(NSA_N_SELECT, nsel)):
        best = jnp.max(score, axis=0, keepdims=True)
        first = jnp.min(jnp.where(score == best, nidx_f, float(nsel)), axis=0, keepdims=True)
        hit = nidx_f == first
        sel_t = jnp.where(hit, 0.0, sel_t)
        score = jnp.where(hit, -jnp.inf, score)
    sel_neg = sel_t.T.astype(BF16)

    for h in range(HPG):
        m_refs[h][...] = jnp.full((QB, LANES), -jnp.inf, F32)
        l_refs[h][...] = jnp.zeros((QB, LANES), F32)
        acc_refs[h][...] = jnp.zeros((QB, DH), F32)
    KT = SEL_TILE
    sub = KT // QB
    q_sel = jnp.concatenate([q, jnp.concatenate([sel_neg] * HPG, axis=0)], axis=1)

    def lane_fold(x, op):
        parts = [x[:, u * LANES:(u + 1) * LANES] for u in range(KT // LANES)]
        while len(parts) > 1:
            parts = [op(parts[a], parts[a + 1]) for a in range(0, len(parts), 2)]
        return parts[0]

    def score_step(t, carry):
        k0 = pl.multiple_of(t * KT, KT)
        k_sel = jnp.concatenate([ks_ref[0, 0, pl.ds(k0, KT), :], et_ref[pl.ds(k0, KT), :]], axis=1)
        s = lax.dot_general(q_sel, k_sel, nt, preferred_element_type=F32)
        for h in range(HPG):
            bias = jnp.concatenate([_dist_tile(bt_ref, h, j - t * sub - u) for u in range(sub)], axis=1)
            sh = s[h * QB:(h + 1) * QB, :] + bias
            sc_ref[t, h * QB:(h + 1) * QB, :] = sh
            m_refs[h][...] = jnp.maximum(m_refs[h][...], lane_fold(sh, jnp.maximum))
        return carry

    def value_step(t, carry):
        k0 = pl.multiple_of(t * KT, KT)
        v_t = vs_ref[0, 0, pl.ds(k0, KT), :]
        for h in range(HPG):
            mb = m_refs[h][...]
            p = jnp.exp2(sc_ref[t, h * QB:(h + 1) * QB, :] - jnp.concatenate([mb] * (KT // LANES), axis=1))
            l_refs[h][...] += lane_fold(p, jnp.add)
            acc_refs[h][...] += jnp.dot(p.astype(BF16), v_t, preferred_element_type=F32)
        return carry

    n_pairs = ((j * QB) // KT + 2) // 2

    def pair(step):
        def body(p, carry):
            return step(2 * p + 1, step(2 * p, carry))
        return body

    lax.fori_loop(0, n_pairs, pair(score_step), 0)
    for h in range(HPG):
        m_refs[h][...] = jnp.broadcast_to(jnp.max(m_refs[h][...], axis=-1, keepdims=True), (QB, LANES))
    lax.fori_loop(0, n_pairs, pair(value_step), 0)
    o_s = [acc_refs[h][...] / jnp.sum(l_refs[h][...], axis=-1, keepdims=True) for h in range(HPG)]

    WK = WINDOW + QB
    w0 = pl.multiple_of(j * QB, QB)
    k_w = kw_ref[0, 0, pl.ds(w0, WK), :]
    v_w = vw_ref[0, 0, pl.ds(w0, WK), :]
    s_w = lax.dot_general(q, k_w, nt, preferred_element_type=F32)
    nwt = WK // QB
    tab_w = [jnp.where(j + u >= nwt - 1, TAB_WINDOW_FAR if u == 0 else nwt - u, TAB_MASKED) for u in range(nwt)]
    o_w = []
    for h in range(HPG):
        bias = jnp.concatenate([bt_ref[0, h, tab_w[u]] for u in range(nwt)], axis=1)
        sh = s_w[h * QB:(h + 1) * QB, :] + bias
        mxw = jnp.max(sh, axis=-1, keepdims=True)
        pw = jnp.exp2(sh - mxw)
        lw = jnp.sum(pw, axis=-1, keepdims=True)
        o_w.append(jnp.dot(pw.astype(BF16), v_w, preferred_element_type=F32) / lw)

    gate = jax.nn.sigmoid(gt_ref[...])
    for h in range(HPG):
        gc = gate[:, 8 + 3 * h:9 + 3 * h]
        gs = gate[:, 9 + 3 * h:10 + 3 * h]
        gw = gate[:, 10 + 3 * h:11 + 3 * h]
        y = gc * o_c[h * QB:(h + 1) * QB, :] + gs * o_s[h] + gw * o_w[h]
        o_ref[:, h * DH:(h + 1) * DH] = y.astype(o_ref.dtype)


def nsa_tables(rel_bias, seq):
    def bias_of(dist):
        dist = jnp.maximum(dist, 0)
        max_exact = NUM_BUCKETS // 2
        log_ratio = jnp.log(jnp.maximum(dist, 1).astype(F32) / max_exact) / math.log(MAX_DISTANCE / max_exact)
        large = jnp.minimum(max_exact + (log_ratio * (NUM_BUCKETS - max_exact)).astype(jnp.int32), NUM_BUCKETS - 1)
        bucket = jnp.where(dist < max_exact, dist, large)
        onehot = (bucket[..., None] == jnp.arange(NUM_BUCKETS)).astype(F32)
        out = jnp.dot(onehot, rel_bias.astype(F32), precision=lax.Precision.HIGHEST)
        return jnp.moveaxis(out, -1, 0)

    assert seq // CMP_STRIDE >= LANES and Q_BLOCK * (N_DIST_TILES - 1) - (Q_BLOCK - 1) >= MAX_DISTANCE
    i = jnp.arange(Q_BLOCK)
    delta = i[None, :, None] - i[None, None, :]
    dist = jnp.arange(N_DIST_TILES)[:, None, None] * Q_BLOCK + delta
    tiles = bias_of(dist) * LOG2E
    causal = jnp.where(dist >= 0, tiles, NEG_INF)
    far = jnp.where(delta < 0, tiles[:, WINDOW // Q_BLOCK], NEG_INF)[:, None]
    masked = jnp.full_like(far, NEG_INF)
    bt = jnp.concatenate([masked, causal, far], axis=1).reshape(NSA_G, NSA_HPG, N_TAB, Q_BLOCK, Q_BLOCK)
    per_tile = Q_BLOCK // CMP_STRIDE
    assert (LANES - per_tile) * CMP_STRIDE - (CMP_LEN - 1) >= MAX_DISTANCE
    dist_c = i[:, None] - ((jnp.arange(LANES)[None, :] - (LANES - per_tile)) * CMP_STRIDE + CMP_LEN - 1)
    rel = (bias_of(dist_c) * LOG2E).reshape(NSA_G, NSA_HPG, Q_BLOCK, LANES)
    return bt, rel


def nsa_block_indicator(seq):
    nsel = seq // SLC_LEN
    return jnp.asarray(np.arange(seq)[:, None] // SLC_LEN == np.arange(nsel)[None, :], BF16)


def nsa_overlap(seq):
    ncp = seq // CMP_STRIDE
    ns = seq // SLC_LEN
    cstart = np.arange(ncp) * CMP_STRIDE
    sstart = np.arange(ns) * SLC_LEN
    ov = np.clip(np.minimum(cstart[:, None] + CMP_LEN, sstart[None, :] + SLC_LEN)
                 - np.maximum(cstart[:, None], sstart[None, :]), 0, None).astype(np.float32) / CMP_LEN
    ov[ncp - 1] = 0.0
    return jnp.asarray(ov.T, BF16)


def nsa_attend(z, pq, pkv, pw, cmp, bt, rel, ovt, et, bsz, seq):
    nqb = seq // Q_BLOCK
    ncp = seq // CMP_STRIDE
    nsel = seq // SLC_LEN
    R = NSA_HPG * Q_BLOCK
    assert (seq // SEL_TILE) % 2 == 0
    return pl.pallas_call(
        _nsa_kernel,
        grid=(bsz, NSA_G, nqb),
        in_specs=[pl.BlockSpec((1, NSA_HPG, Q_BLOCK, NSA_DH), lambda b, g, j: (b, g, j, 0)),
                  pl.BlockSpec((1, 1, 1, ncp, NSA_DH), lambda b, g, j: (0, b, g, 0, 0)),
                  pl.BlockSpec((1, 1, 1, ncp, NSA_DH), lambda b, g, j: (1, b, g, 0, 0)),
                  pl.BlockSpec((1, NSA_HPG, Q_BLOCK, LANES), lambda b, g, j: (g, 0, 0, 0)),
                  pl.BlockSpec((nsel, ncp), lambda b, g, j: (0, 0)),
                  pl.BlockSpec((1, 1, seq, NSA_DH), lambda b, g, j: (b, g, 0, 0)),
                  pl.BlockSpec((1, 1, seq, NSA_DH), lambda b, g, j: (b, NSA_G + g, 0, 0)),
                  pl.BlockSpec((seq, nsel), lambda b, g, j: (0, 0)),
                  pl.BlockSpec((1, 1, seq + WINDOW, NSA_DH), lambda b, g, j: (b, g, 0, 0)),
                  pl.BlockSpec((1, 1, seq + WINDOW, NSA_DH), lambda b, g, j: (b, NSA_G + g, 0, 0)),
                  pl.BlockSpec((1, NSA_HPG, N_TAB, Q_BLOCK, Q_BLOCK), lambda b, g, j: (g, 0, 0, 0, 0)),
                  pl.BlockSpec((Q_BLOCK, LANES), lambda b, g, j: (b * nqb + j, Z_SMALL // LANES + g))],
        out_specs=pl.BlockSpec((Q_BLOCK, NSA_HPG * NSA_DH), lambda b, g, j: (b * nqb + j, g)),
        out_shape=jax.ShapeDtypeStruct((bsz * seq, BRANCH_WIDTH), BF16),
        scratch_shapes=([pltpu.VMEM((seq // SEL_TILE, R, SEL_TILE), F32)]
                        + [pltpu.VMEM((Q_BLOCK, LANES), F32)] * (2 * NSA_HPG)
                        + [pltpu.VMEM((Q_BLOCK, NSA_DH), F32)] * NSA_HPG),
        compiler_params=_cp("arbitrary", "arbitrary", "arbitrary"),
        name="nsa_attend",
    )(pq, cmp, cmp, rel, ovt, pkv, pkv, et, pw, pw, bt, z)


def _merge_kernel(ya_ref, yb_ref, yc_ref, w_ref, ga_ref, gb_ref, gc_ref, o_ref):
    acc = jax.nn.sigmoid(ga_ref[...]) * jnp.dot(ya_ref[...], w_ref[0], preferred_element_type=F32)
    acc = acc + jax.nn.sigmoid(gb_ref[...]) * jnp.dot(yb_ref[...], w_ref[1], preferred_element_type=F32)
    acc = acc + jax.nn.sigmoid(gc_ref[...]) * jnp.dot(yc_ref[...], w_ref[2], preferred_element_type=F32)
    o_ref[...] = acc.astype(o_ref.dtype)


def merge(ya, yb, yc, w_branch, z, tm=1024, tn=512):
    n = ya.shape[0]
    d = w_branch.shape[2]
    bw = ya.shape[1]
    ymap = lambda i, j: (i, 0)
    gmap = lambda k: (lambda i, j: (i, (Z_GATE + k * d) // tn + j))
    return pl.pallas_call(
        _merge_kernel,
        grid=(n // tm, d // tn),
        in_specs=[pl.BlockSpec((tm, bw), ymap), pl.BlockSpec((tm, bw), ymap), pl.BlockSpec((tm, bw), ymap),
                  pl.BlockSpec((3, bw, tn), lambda i, j: (0, 0, j)),
                  pl.BlockSpec((tm, tn), gmap(0)), pl.BlockSpec((tm, tn), gmap(1)), pl.BlockSpec((tm, tn), gmap(2))],
        out_specs=pl.BlockSpec((tm, tn), lambda i, j: (i, j)),
        out_shape=jax.ShapeDtypeStruct((n, d), BF16),
        compiler_params=_cp("arbitrary", "arbitrary"),
        name="merge",
    )(ya, yb, yc, w_branch, z, z, z)


def _outproj_kernel(a_ref, w_ref, x_ref, g_ref, o_ref):
    o_ref[...] = x_ref[...] + g_ref[0] * jnp.dot(a_ref[...], w_ref[...], preferred_element_type=F32)


def out_proj(a, w, x2d, gate, seq, tm=1024, tn=512):
    n, k = a.shape
    d = w.shape[1]
    per_b = seq // tm
    return pl.pallas_call(
        _outproj_kernel,
        grid=(n // tm, d // tn),
        in_specs=[pl.BlockSpec((tm, k), lambda i, j: (i, 0)),
                  pl.BlockSpec((k, tn), lambda i, j: (0, j)),
                  pl.BlockSpec((tm, tn), lambda i, j: (i, j)),
                  pl.BlockSpec((1, 1, tn), lambda i, j: (i // per_b, 0, j))],
        out_specs=pl.BlockSpec((tm, tn), lambda i, j: (i, j)),
        out_shape=jax.ShapeDtypeStruct((n, d), F32),
        compiler_params=_cp("arbitrary", "arbitrary"),
        name="out_proj",
    )(a, w, x2d, gate[:, None, :])


def _pack_bf16_pairs(y):
    half = y.shape[1] // 2
    lo = lax.bitcast_convert_type(y[:, :half].astype(BF16).astype(F32), jnp.uint32)
    hi = lax.bitcast_convert_type(y[:, half:].astype(BF16).astype(F32), jnp.uint32)
    return lax.shift_right_logical(lo, jnp.uint32(16)) | (hi & jnp.uint32(0xFFFF0000))


def _unpack_bf16_pairs(w):
    lo = lax.bitcast_convert_type(lax.shift_left(w, jnp.uint32(16)), F32)
    hi = lax.bitcast_convert_type(w & jnp.uint32(0xFFFF0000), F32)
    return jnp.concatenate([lo, hi], axis=1)


def _router_kernel(x_ref, g_ref, sc_ref, sh_ref, rw_ref, rwl_ref, rb_ref, h_ref, idx_ref, p_ref):
    x = x_ref[...]
    y = x * lax.rsqrt(jnp.mean(x * x, axis=-1, keepdims=True) + EPS)
    hf = (y * g_ref[...]) * (1.0 + sc_ref[0]) + sh_ref[0]
    h = hf.astype(BF16)
    h_ref[...] = _pack_bf16_pairs(hf)
    h_lo = (hf - h.astype(F32)).astype(BF16)
    logits = (jnp.dot(h, rw_ref[...], preferred_element_type=F32)
              + jnp.dot(h_lo, rw_ref[...], preferred_element_type=F32)
              + jnp.dot(h, rwl_ref[...], preferred_element_type=F32)) + rb_ref[...]
    lane = lax.broadcasted_iota(jnp.int32, logits.shape, 1)
    lane_f = lane.astype(F32)
    idx_out = jnp.zeros(logits.shape, F32)
    val_out = jnp.full(logits.shape, -jnp.inf, F32)
    for k in range(TOP_K):
        best = jnp.max(logits, axis=-1, keepdims=True)
        first = jnp.min(jnp.where(logits == best, lane_f, float(LANES)), axis=-1, keepdims=True)
        idx_out = jnp.where(lane == k, first, idx_out)
        val_out = jnp.where(lane == k, best, val_out)
        logits = jnp.where(lane_f == first, -jnp.inf, logits)
    e = jnp.exp(val_out - jnp.max(val_out, axis=-1, keepdims=True))
    idx_ref[...] = idx_out.astype(jnp.int32)
    p_ref[...] = e / jnp.sum(e, axis=-1, keepdims=True)


def router(x2d, g, sc, sh, rw, rb, seq, tm=1024):
    n, d = x2d.shape
    per_b = seq // tm
    rw_hi = rw.astype(BF16)
    rw_lo = (rw - rw_hi.astype(F32)).astype(BF16)
    rwp = jnp.zeros((d, LANES), BF16).at[:, :N_EXPERTS].set(rw_hi)
    rwl = jnp.zeros((d, LANES), BF16).at[:, :N_EXPERTS].set(rw_lo)
    rbp = jnp.full((1, LANES), -jnp.inf, F32).at[0, :N_EXPERTS].set(rb)
    return pl.pallas_call(
        _router_kernel,
        grid=(n // tm,),
        in_specs=[pl.BlockSpec((tm, d), lambda i: (i, 0)),
                  pl.BlockSpec((1, d), lambda i: (0, 0)),
                  pl.BlockSpec((1, 1, d), lambda i: (i // per_b, 0, 0)),
                  pl.BlockSpec((1, 1, d), lambda i: (i // per_b, 0, 0)),
                  pl.BlockSpec((d, LANES), lambda i: (0, 0)),
                  pl.BlockSpec((d, LANES), lambda i: (0, 0)),
                  pl.BlockSpec((1, LANES), lambda i: (0, 0))],
        out_specs=[pl.BlockSpec((tm, d // 2), lambda i: (i, 0)),
                   pl.BlockSpec((tm, LANES), lambda i: (i, 0)),
                   pl.BlockSpec((tm, LANES), lambda i: (i, 0))],
        out_shape=[jax.ShapeDtypeStruct((n, d // 2), jnp.uint32),
                   jax.ShapeDtypeStruct((n, LANES), jnp.int32),
                   jax.ShapeDtypeStruct((n, LANES), F32)],
        compiler_params=_cp("arbitrary"),
        name="router",
    )(x2d, g.reshape(1, d), sc[:, None, :], sh[:, None, :], rwp, rwl, rbp)


def _dispatch_kernel(dest_ref, h_ref, init_ref, xb_ref, sem):
    del init_ref
    tm = h_ref.shape[0]

    def row_copy(k, r):
        dst = dest_ref[0, 0, k * tm + r]
        return pltpu.make_async_copy(h_ref.at[pl.ds(r, 1), :], xb_ref.at[pl.ds(dst, 1), :], sem)

    def start(r, carry):
        for k in range(TOP_K):
            row_copy(k, r).start()
        return carry

    def wait(r, carry):
        for k in range(TOP_K):
            row_copy(k, r).wait()
        return carry

    lax.fori_loop(0, tm, start, 0, unroll=8)
    lax.fori_loop(0, tm, wait, 0, unroll=8)


def dispatch(hp, dest_tiles, init):
    n, w = hp.shape
    rows = init.shape[0]
    nb, _, per_tile = dest_tiles.shape
    tm = per_tile // TOP_K
    return pl.pallas_call(
        _dispatch_kernel,
        grid=(nb,),
        in_specs=[pl.BlockSpec((1, 1, per_tile), lambda i: (i, 0, 0), memory_space=pltpu.SMEM),
                  pl.BlockSpec((tm, w), lambda i: (i, 0)),
                  pl.BlockSpec(memory_space=pl.ANY)],
        out_specs=pl.BlockSpec(memory_space=pl.ANY),
        out_shape=jax.ShapeDtypeStruct((rows, w), jnp.uint32),
        scratch_shapes=[pltpu.SemaphoreType.DMA(())],
        input_output_aliases={2: 0},
        compiler_params=_cp("arbitrary"),
        name="dispatch",
    )(dest_tiles, hp, init)


def _expert_kernel(be_ref, nv_ref, x_ref, wg_ref, wu_ref, bg_ref, bu_ref, w2_ref, b2_ref, o_ref, acc_ref, xs_ref):
    i = pl.program_id(0)
    f = pl.program_id(1)

    @pl.when(i < nv_ref[0])
    def _():
        @pl.when(f == 0)
        def _():
            acc_ref[...] = jnp.zeros(acc_ref.shape, F32)
            xs_ref[...] = _unpack_bf16_pairs(x_ref[...]).astype(BF16)

        x = xs_ref[...]
        gate = jnp.dot(x, wg_ref[0, 0].astype(BF16), preferred_element_type=F32) + bg_ref[0, 0]
        up = jnp.dot(x, wu_ref[0, 0].astype(BF16), preferred_element_type=F32) + bu_ref[0, 0]
        gate = jnp.minimum(gate, SWIGLU_LIMIT)
        up = jnp.clip(up, -SWIGLU_LIMIT, SWIGLU_LIMIT)
        act = (up + 1.0) * (gate * jax.nn.sigmoid(SWIGLU_ALPHA * gate))
        acc_ref[...] += jnp.dot(act.astype(BF16), w2_ref[0, 0].astype(BF16), preferred_element_type=F32)

        @pl.when(f == pl.num_programs(1) - 1)
        def _():
            o_ref[...] = _pack_bf16_pairs(acc_ref[...] + b2_ref[0, 0])

    @pl.when(i >= nv_ref[0])
    def _():
        o_ref[...] = jnp.zeros(o_ref.shape, o_ref.dtype)


def experts(xb, block_exp, n_valid, w1, b1, w2, b2, layer, tm, tf=256):
    rows = xb.shape[0]
    d = 2 * xb.shape[1]
    nf = D_FF // tf
    nblk = rows // tm
    n_exp = w1.shape[1]
    fe = lambda i, f, nv: jnp.where(i < nv[0], f, nf - 1)
    grid_spec = pltpu.PrefetchScalarGridSpec(
        num_scalar_prefetch=2,
        grid=(nblk, nf),
        in_specs=[pl.BlockSpec((tm, d // 2), lambda i, f, be, nv: (jnp.minimum(i, nv[0] - 1), 0)),
                  pl.BlockSpec((1, 1, d, tf), lambda i, f, be, nv: (layer, be[i], 0, fe(i, f, nv))),
                  pl.BlockSpec((1, 1, d, tf), lambda i, f, be, nv: (layer, be[i], 0, nf + fe(i, f, nv))),
                  pl.BlockSpec((1, 1, 1, tf), lambda i, f, be, nv: (layer * n_exp + be[i], 0, 0, fe(i, f, nv))),
                  pl.BlockSpec((1, 1, 1, tf), lambda i, f, be, nv: (layer * n_exp + be[i], 0, 0, nf + fe(i, f, nv))),
                  pl.BlockSpec((1, 1, tf, d), lambda i, f, be, nv: (layer, be[i], fe(i, f, nv), 0)),
                  pl.BlockSpec((1, 1, 1, d), lambda i, f, be, nv: (layer * n_exp + be[i], 0, 0, 0))],
        out_specs=pl.BlockSpec((tm, d // 2), lambda i, f, be, nv: (i, 0)),
        scratch_shapes=[pltpu.VMEM((tm, d), F32), pltpu.VMEM((tm, d), BF16)],
    )
    b1r = b1.reshape(-1, 1, 1, b1.shape[-1])
    b2r = b2.reshape(-1, 1, 1, d)
    return pl.pallas_call(
        _expert_kernel,
        grid_spec=grid_spec,
        out_shape=jax.ShapeDtypeStruct((rows, d // 2), jnp.uint32),
        compiler_params=_cp("arbitrary", "arbitrary"),
        name="experts",
    )(block_exp, n_valid, xb, w1, w1, b1r, b1r, w2, b2r)


def _combine_kernel(dest_ref, p_ref, x_ref, g_ref, yb_ref, o_ref, ybuf_ref, sem):
    tm = x_ref.shape[0]

    def row_copy(k, r):
        src = dest_ref[0, 0, k * tm + r]
        return pltpu.make_async_copy(yb_ref.at[pl.ds(src, 1), :], ybuf_ref.at[k, pl.ds(r, 1), :], sem)

    def start(r, carry):
        for k in range(TOP_K):
            row_copy(k, r).start()
        return carry

    def wait(r, carry):
        for k in range(TOP_K):
            row_copy(k, r).wait()
        return carry

    lax.fori_loop(0, tm, start, 0, unroll=8)
    lax.fori_loop(0, tm, wait, 0, unroll=8)
    p = p_ref[...]
    y = p[:, 0:1] * _unpack_bf16_pairs(ybuf_ref[0])
    for k in range(1, TOP_K):
        y = y + p[:, k:k + 1] * _unpack_bf16_pairs(ybuf_ref[k])
    o_ref[...] = x_ref[...] + g_ref[0] * y


def combine(yb, dest_tiles, prob, x2d, gate, seq):
    n, d = x2d.shape
    nb = dest_tiles.shape[0]
    tm = n // nb
    per_b = seq // tm
    return pl.pallas_call(
        _combine_kernel,
        grid=(nb,),
        in_specs=[pl.BlockSpec((1, 1, TOP_K * tm), lambda i: (i, 0, 0), memory_space=pltpu.SMEM),
                  pl.BlockSpec((tm, LANES), lambda i: (i, 0)),
                  pl.BlockSpec((tm, d), lambda i: (i, 0)),
                  pl.BlockSpec((1, 1, d), lambda i: (i // per_b, 0, 0)),
                  pl.BlockSpec(memory_space=pl.ANY)],
        out_specs=pl.BlockSpec((tm, d), lambda i: (i, 0)),
        out_shape=jax.ShapeDtypeStruct((n, d), F32),
        scratch_shapes=[pltpu.VMEM((TOP_K, tm, d // 2), jnp.uint32), pltpu.SemaphoreType.DMA(())],
        compiler_params=_cp("arbitrary"),
        name="combine",
    )(dest_tiles, prob, x2d, gate[:, None, :], yb)


def moe(x2d, g, sc, sh, gate, rw, rb, w1, b1, w2, b2, layer, seq, xb_init=None, tm_e=1024, tm_t=512):
    n, d = x2d.shape
    h, idx, prob = router(x2d, g, sc, sh, rw, rb, seq)
    e_flat = idx[:, :TOP_K].T.reshape(-1)
    na = n * TOP_K
    onehot = (e_flat[:, None] == jnp.arange(N_EXPERTS)[None, :]).astype(jnp.int32)
    csum = jnp.cumsum(onehot, axis=0)
    counts = csum[-1]
    rank = jnp.take_along_axis(csum, e_flat[:, None], axis=1)[:, 0] - 1
    padded = (counts + tm_e - 1) // tm_e * tm_e
    pend = jnp.cumsum(padded)
    pstart = pend - padded
    dest = pstart[e_flat] + rank
    nblk = na // tm_e + N_EXPERTS
    rows = nblk * tm_e
    starts = jnp.arange(nblk, dtype=jnp.int32) * tm_e
    block_exp = jnp.minimum(jnp.sum(pend[None, :] <= starts[:, None], axis=1), N_EXPERTS - 1).astype(jnp.int32)
    n_valid = (pend[-1:] // tm_e).astype(jnp.int32)
    block_exp = jnp.where(starts < pend[-1], block_exp, block_exp[jnp.maximum(n_valid[0] - 1, 0)])
    nb = n // tm_t
    dest_tiles = dest.reshape(TOP_K, nb, tm_t).transpose(1, 0, 2).reshape(nb, 1, TOP_K * tm_t)
    if xb_init is None:
        xb_init = jnp.zeros((rows, h.shape[1]), jnp.uint32)
    xb = dispatch(h, dest_tiles, xb_init)
    yb = experts(xb, block_exp, n_valid, w1, b1, w2, b2, layer, tm_e)
    return combine(yb, dest_tiles, prob, x2d, gate, seq), xb


def fused_in_weight(w_in, w_gate):
    sizes = (512, 512, 1024, 4, 4, 1024, 1024, 256, 256, 256, 256, 256, 256, 24, 1024, 1024)
    pts = np.cumsum(sizes)[:-1].tolist()
    (qm, km, vm, ig, fg, og, qn, kc, vc, ks, vs, kw, vw, gn, gu, gv) = jnp.split(w_in, pts, axis=1)
    d = w_in.shape[0]
    pad = lambda k: jnp.zeros((d, k), w_in.dtype)
    per_group = NSA_HPG * 3
    small0 = jnp.concatenate([ig, fg, gn[:, :per_group], pad(LANES - 8 - per_group)], axis=1)
    small1 = jnp.concatenate([pad(8), gn[:, per_group:], pad(LANES - 8 - per_group)], axis=1)
    cols = [qm, km, vm, og, qn, gu, gv, kc, vc, ks, vs, kw, vw, w_gate[0], w_gate[1], w_gate[2], small0, small1]
    w = jnp.concatenate(cols, axis=1)
    w = jnp.concatenate([w, pad(Z_WIDTH - w.shape[1])], axis=1)
    return w.astype(BF16)


def kernel(x, c, ada_w, ada_b, norm1_g, norm2_g, w_in, conv_w, conv_b, mlstm_gate_b, mlstm_norm_g, cmp_pos,
           cmp_k_w1, cmp_k_w2, cmp_v_w1, cmp_v_w2, qnorm_g, knorm_g, rel_bias, gmlp_norm_g, gmlp_ws, gmlp_b,
           w_branch, w_gate, w_out, router_w, router_b, exp_w1, exp_b1, exp_w2, exp_b2):
    bsz, seq, d = x.shape
    depth = ada_w.shape[0]
    x2d = x.reshape(bsz * seq, d)
    bt, rel = nsa_tables(rel_bias, seq)
    ovt = nsa_overlap(seq)
    et = nsa_block_indicator(seq)
    xb = None
    for l in range(depth):
        mod = ada_mod(c, ada_w, ada_b, l)
        sh1, sc1, g1, sh2, sc2, g2 = jnp.split(mod, 6, axis=-1)
        z = in_proj(x2d, norm1_g[l], sc1, sh1, fused_in_weight(w_in[l], w_gate[l]), seq)
        ya = mlstm(z, bsz, seq, conv_w[l], conv_b[l], mlstm_gate_b[l], mlstm_norm_g[l])
        pq, pkv, pw = nsa_prep(z, bsz, seq, qnorm_g[l], knorm_g[l])
        cmp = nsa_compress(z, bsz, seq, cmp_pos[l], cmp_k_w1[l], cmp_k_w2[l], cmp_v_w1[l], cmp_v_w2[l], knorm_g[l])
        yb = nsa_attend(z, pq, pkv, pw, cmp, bt, rel, ovt, et, bsz, seq)
        yc = gmlp(z, gmlp_norm_g[l], gmlp_ws[l], gmlp_b[l])
        merged = merge(ya, yb, yc, w_branch[l].astype(BF16), z)
        x2d = out_proj(merged, w_out[l].astype(BF16), x2d, g1, seq)
        x2d, xb = moe(x2d, norm2_g[l], sc2, sh2, g2, router_w[l], router_b[l], exp_w1, exp_b1, exp_w2, exp_b2, l, seq,
                      xb_init=xb)
    return x2d.reshape(bsz, seq, d)
```

```python
import functools
import math

import jax
import jax.numpy as jnp
import numpy as np
from jax import lax
from jax.experimental import pallas as pl
from jax.experimental.pallas import tpu as pltpu

F32 = jnp.float32
BF16 = jnp.bfloat16

LANES = 128
VMEM_LIMIT = 56 * 1024 * 1024

D_MODEL = 2048
BRANCH_WIDTH = 1024
MLSTM_HEADS = 4
MLSTM_DV = 256
MLSTM_DQK = 128
MLSTM_TILE = 256
CONV_WIDTH = 4
NSA_HEADS = 8
NSA_G = 2
NSA_HPG = 4
NSA_DH = 128
CMP_LEN = 32
CMP_STRIDE = 16
SLC_LEN = 64
NSA_N_SELECT = 16
WINDOW = 512
Q_BLOCK = 128
NUM_BUCKETS = 32
MAX_DISTANCE = 1024
GMLP_CHUNK = 128
GMLP_GROUPS = 8
N_EXPERTS = 32
TOP_K = 4
D_FF = 1536
SWIGLU_LIMIT = 7.0
SWIGLU_ALPHA = 1.702
NEG_INF = -1e30
FORCE_SELECT = 1e4
EPS = 1e-6

LOG2E = math.log2(math.e)
SEL_TILE = 512
N_DIST_TILES = 10
TAB_MASKED, TAB_WINDOW_FAR, N_TAB = 0, N_DIST_TILES + 1, N_DIST_TILES + 2

Z_QK, Z_V, Z_OG, Z_NQ, Z_GU, Z_GV = 0, 1024, 2048, 3072, 4096, 5120
Z_KV = 6144
Z_GATE = 7680
Z_SMALL = 13824
Z_WIDTH = 14336


def _cp(*sem):
    return pltpu.CompilerParams(dimension_semantics=sem, vmem_limit_bytes=VMEM_LIMIT)


def _ada_kernel(c_ref, w_ref, b_ref, o_ref):
    a = jax.nn.silu(c_ref[...]).astype(BF16)
    o_ref[...] = jnp.dot(a, w_ref[0].astype(BF16), preferred_element_type=F32) + b_ref[0]


def ada_mod(c, w, b, layer):
    bsz, d = c.shape
    n = w.shape[2]
    tn = 1024
    cp = jnp.zeros((8, d), F32).at[:bsz].set(c)
    out = pl.pallas_call(
        _ada_kernel,
        grid=(n // tn,),
        in_specs=[pl.BlockSpec((8, d), lambda j: (0, 0)),
                  pl.BlockSpec((1, d, tn), lambda j: (layer, 0, j)),
                  pl.BlockSpec((1, 1, tn), lambda j: (layer, 0, j))],
        out_specs=pl.BlockSpec((8, tn), lambda j: (0, j)),
        out_shape=jax.ShapeDtypeStruct((8, n), F32),
        compiler_params=_cp("arbitrary"),
        name="ada_mod",
    )(cp, w, b[:, None, :])
    return out[:bsz]


def _in_kernel(x_ref, g_ref, sc_ref, sh_ref, w_ref, o_ref, h_ref):
    @pl.when(pl.program_id(1) == 0)
    def _():
        x = x_ref[...]
        y = x * lax.rsqrt(jnp.mean(x * x, axis=-1, keepdims=True) + EPS)
        h = (y * g_ref[...]) * (1.0 + sc_ref[0]) + sh_ref[0]
        h_ref[...] = h.astype(BF16)

    o_ref[...] = jnp.dot(h_ref[...], w_ref[...], preferred_element_type=F32)


def in_proj(x2d, g, sc, sh, wcat, seq, tm=1024, tn=1024):
    n, d = x2d.shape
    w = wcat.shape[1]
    per_b = seq // tm
    return pl.pallas_call(
        _in_kernel,
        grid=(n // tm, w // tn),
        in_specs=[pl.BlockSpec((tm, d), lambda i, j: (i, 0)),
                  pl.BlockSpec((1, d), lambda i, j: (0, 0)),
                  pl.BlockSpec((1, 1, d), lambda i, j: (i // per_b, 0, 0)),
                  pl.BlockSpec((1, 1, d), lambda i, j: (i // per_b, 0, 0)),
                  pl.BlockSpec((d, tn), lambda i, j: (0, j))],
        out_specs=pl.BlockSpec((tm, tn), lambda i, j: (i, j)),
        out_shape=jax.ShapeDtypeStruct((n, w), F32),
        scratch_shapes=[pltpu.VMEM((tm, d), BF16)],
        compiler_params=_cp("arbitrary", "arbitrary"),
        name="in_proj",
    )(x2d, g.reshape(1, d), sc[:, None, :], sh[:, None, :], wcat)


def _cumsum_rows(x, n):
    row = lax.broadcasted_iota(jnp.int32, x.shape, 0)
    s = 1
    while s < n:
        x = x + jnp.where(row >= s, pltpu.roll(x, s, 0), 0.0)
        s *= 2
    return x


def _mlstm_kernel(qk_ref, v_ref, og_ref, sm_ref, cw_ref, cb_ref, gb_ref, ng_ref, o_ref,
                  buf_ref, ct_ref, n_ref, m_ref):
    L = qk_ref.shape[0]
    H, DQK, DV = MLSTM_HEADS, MLSTM_DQK, MLSTM_DV

    @pl.when(pl.program_id(1) == 0)
    def _():
        buf_ref[0:8, :] = jnp.zeros((8, buf_ref.shape[1]), F32)
        ct_ref[...] = jnp.zeros(ct_ref.shape, F32)
        n_ref[...] = jnp.zeros(n_ref.shape, F32)
        m_ref[...] = jnp.full(m_ref.shape, NEG_INF, F32)

    buf_ref[8:8 + L, :] = qk_ref[...]
    conv = cb_ref[...] + cw_ref[3:4, :] * buf_ref[8:8 + L, :]
    for j in range(CONV_WIDTH - 1):
        conv = conv + cw_ref[j:j + 1, :] * buf_ref[5 + j:5 + j + L, :]
    qk = jax.nn.silu(conv)
    buf_ref[0:8, :] = buf_ref[L:L + 8, :]

    sm = sm_ref[...] + gb_ref[...]
    bcum = _cumsum_rows(jax.nn.log_sigmoid(sm), L)
    sm_t = sm.T
    bcum_t = bcum.T
    row = lax.broadcasted_iota(jnp.int32, (L, L), 0)
    col = lax.broadcasted_iota(jnp.int32, (L, L), 1)
    causal = row >= col

    for h in range(H):
        qf = qk[:, h * DQK:(h + 1) * DQK] * DQK ** -0.5
        q = qf.astype(BF16)
        kf = qk[:, (H + h) * DQK:(H + h + 1) * DQK]
        k = kf.astype(BF16)
        v = v_ref[:, h * DV:(h + 1) * DV].astype(BF16)
        b_col = bcum[:, H + h:H + h + 1]
        i_col = sm[:, h:h + 1]
        b_row = bcum_t[H + h:H + h + 1, :]
        i_row = sm_t[h:h + 1, :]
        m_prev = m_ref[h:h + 1, 0:1]
        a = b_col + m_prev
        dmat = jnp.where(causal, b_col - b_row + i_row, -jnp.inf)
        m_t = jnp.maximum(a, jnp.max(dmat, axis=1, keepdims=True))
        w_intra = jnp.exp(dmat - m_t)
        w_inter = jnp.exp(a - m_t)
        s = lax.dot_general(q, k, (((1,), (1,)), ((), ())), preferred_element_type=F32) * w_intra
        ct = ct_ref[h]
        nvec = n_ref[h:h + 1, :]
        num = jnp.dot(s.astype(BF16), v, preferred_element_type=F32) + w_inter * jnp.dot(
            q, ct.astype(BF16), preferred_element_type=F32)
        qn = jnp.sum(qf * nvec, axis=1, keepdims=True)
        den = jnp.sum(s, axis=1, keepdims=True) + w_inter * qn
        hm = num / jnp.maximum(jnp.abs(den), jnp.exp(-m_t))
        b_last = b_col[L - 1:L, :]
        gdec = b_last - b_col + i_col
        m_new = jnp.maximum(b_last + m_prev, jnp.max(gdec, axis=0, keepdims=True))
        ws = jnp.exp(gdec - m_new)
        wc = jnp.exp(b_last + m_prev - m_new)
        kw = ws * kf
        ct_ref[h] = wc * ct + jnp.dot(kw.T.astype(BF16), v, preferred_element_type=F32)
        n_ref[h:h + 1, :] = wc * nvec + jnp.sum(kw, axis=0, keepdims=True)
        m_ref[h:h + 1, :] = jnp.broadcast_to(m_new, (1, LANES))
        y = hm * lax.rsqrt(jnp.mean(hm * hm, axis=-1, keepdims=True) + EPS) * ng_ref[:, h * DV:(h + 1) * DV]
        og = og_ref[:, h * DV:(h + 1) * DV]
        o_ref[:, h * DV:(h + 1) * DV] = (jax.nn.sigmoid(og) * y).astype(o_ref.dtype)


def mlstm(z, bsz, seq, conv_w, conv_b, gate_b, norm_g):
    L = MLSTM_TILE
    nch = seq // L
    gb = jnp.zeros((1, LANES), F32).at[0, :2 * MLSTM_HEADS].set(gate_b)
    wide = BRANCH_WIDTH
    rowmap = lambda col: (lambda b, c: (b * nch + c, col))
    const = lambda b, c: (0, 0)
    return pl.pallas_call(
        _mlstm_kernel,
        grid=(bsz, nch),
        in_specs=[pl.BlockSpec((L, wide), rowmap(Z_QK // wide)),
                  pl.BlockSpec((L, wide), rowmap(Z_V // wide)),
                  pl.BlockSpec((L, wide), rowmap(Z_OG // wide)),
                  pl.BlockSpec((L, LANES), rowmap(Z_SMALL // LANES)),
                  pl.BlockSpec((CONV_WIDTH, wide), const),
                  pl.BlockSpec((1, wide), const),
                  pl.BlockSpec((1, LANES), const),
                  pl.BlockSpec((1, wide), const)],
        out_specs=pl.BlockSpec((L, wide), lambda b, c: (b * nch + c, 0)),
        out_shape=jax.ShapeDtypeStruct((bsz * seq, wide), BF16),
        scratch_shapes=[pltpu.VMEM((L + 8, wide), F32),
                        pltpu.VMEM((MLSTM_HEADS, MLSTM_DQK, MLSTM_DV), F32),
                        pltpu.VMEM((8, MLSTM_DQK), F32),
                        pltpu.VMEM((8, LANES), F32)],
        compiler_params=_cp("arbitrary", "arbitrary"),
        name="mlstm",
    )(z, z, z, z, conv_w, conv_b.reshape(1, wide), gb, norm_g.reshape(1, wide))


def _gmlp_kernel(u_ref, v_ref, ng_ref, ws_ref, b_ref, o_ref):
    T = u_ref.shape[0]
    C = GMLP_CHUNK
    u = jax.nn.gelu(u_ref[...])
    vg = jax.nn.gelu(v_ref[...])
    mu = jnp.mean(vg, axis=-1, keepdims=True)
    var = jnp.mean(jnp.square(vg - mu), axis=-1, keepdims=True)
    v = ((vg - mu) * lax.rsqrt(var + EPS) * ng_ref[...]).astype(BF16)
    row = lax.broadcasted_iota(jnp.int32, (C, C), 0)
    col = lax.broadcasted_iota(jnp.int32, (C, C), 1)
    for g in range(GMLP_GROUPS):
        wsc = jnp.where(row >= col, ws_ref[g], 0.0).astype(BF16)
        bcol = b_ref[:, g:g + 1]
        for n in range(T // C):
            vv = v[n * C:(n + 1) * C, g * C:(g + 1) * C]
            s = jnp.dot(wsc, vv, preferred_element_type=F32) + bcol
            o_ref[n * C:(n + 1) * C, g * C:(g + 1) * C] = (
                u[n * C:(n + 1) * C, g * C:(g + 1) * C] * s).astype(o_ref.dtype)


def gmlp(z, norm_g, ws, b, tg=512):
    n = z.shape[0]
    wide = BRANCH_WIDTH
    return pl.pallas_call(
        _gmlp_kernel,
        grid=(n // tg,),
        in_specs=[pl.BlockSpec((tg, wide), lambda i: (i, Z_GU // wide)),
                  pl.BlockSpec((tg, wide), lambda i: (i, Z_GV // wide)),
                  pl.BlockSpec((1, wide), lambda i: (0, 0)),
                  pl.BlockSpec((GMLP_GROUPS, GMLP_CHUNK, GMLP_CHUNK), lambda i: (0, 0, 0)),
                  pl.BlockSpec((GMLP_CHUNK, GMLP_GROUPS), lambda i: (0, 0))],
        out_specs=pl.BlockSpec((tg, wide), lambda i: (i, 0)),
        out_shape=jax.ShapeDtypeStruct((n, wide), BF16),
        compiler_params=_cp("arbitrary"),
        name="gmlp",
    )(z, z, norm_g.reshape(1, wide), ws, b.T)


def _prep_kernel(n_norm, npad, x_ref, g_ref, o_ref):
    nh = o_ref.shape[1]

    @pl.when(pl.program_id(1) < npad)
    def _():
        o_ref[...] = jnp.zeros(o_ref.shape, o_ref.dtype)

    @pl.when(pl.program_id(1) >= npad)
    def _():
        for h in range(nh):
            x = x_ref[:, h * LANES:(h + 1) * LANES]
            if h < n_norm:
                x = x * lax.rsqrt(jnp.mean(x * x, axis=-1, keepdims=True) + EPS) * g_ref[h]
            o_ref[0, h] = x.astype(o_ref.dtype)


def _prep_call(z, bsz, seq, col0, heads, n_norm, gains, pad, ts, name):
    width = heads * LANES
    nblk = seq // ts
    npad = pad // ts
    return pl.pallas_call(
        functools.partial(_prep_kernel, n_norm, npad),
        grid=(bsz, nblk + npad),
        in_specs=[pl.BlockSpec((ts, width), lambda b, s: (b * nblk + jnp.maximum(s - npad, 0), col0 // width)),
                  pl.BlockSpec((heads, 1, LANES), lambda b, s: (0, 0, 0))],
        out_specs=pl.BlockSpec((1, heads, ts, LANES), lambda b, s: (b, 0, s, 0)),
        out_shape=jax.ShapeDtypeStruct((bsz, heads, seq + pad, LANES), BF16),
        compiler_params=_cp("arbitrary", "arbitrary"),
        name=name,
    )(z, gains)


def nsa_prep(z, bsz, seq, qnorm_g, knorm_g):
    gq = jnp.tile((qnorm_g * (NSA_DH ** -0.5 * LOG2E)).reshape(1, 1, LANES), (NSA_HEADS, 1, 1))
    gk = jnp.tile(knorm_g.reshape(1, 1, LANES), (2 * NSA_G, 1, 1))
    pq = _prep_call(z, bsz, seq, Z_NQ, NSA_HEADS, NSA_HEADS, gq, 0, 1024, "nsa_prep_q")
    pkv = _prep_call(z, bsz, seq, Z_KV + 512, 2 * NSA_G, NSA_G, gk, 0, 1024, "nsa_prep_sel")
    pw = _prep_call(z, bsz, seq, Z_KV + 1024, 2 * NSA_G, NSA_G, gk, WINDOW, WINDOW, "nsa_prep_win")
    return pq, pkv, pw


def _compress_kernel(x_ref, pos_ref, w1_ref, w2_ref, g_ref, o_ref):
    half = x_ref.shape[-1]
    x = x_ref[0, 0, 0]
    lo = jnp.dot((x + pos_ref[0, :, 0:half]).astype(BF16), w1_ref[0, 0:half, :], preferred_element_type=F32)
    hi = jnp.dot((x + pos_ref[0, :, half:2 * half]).astype(BF16), w1_ref[0, half:2 * half, :],
                 preferred_element_type=F32)
    ncp = x.shape[0]
    pre = lo + pltpu.roll(hi, ncp - 1, 0)
    y = jnp.dot(jax.nn.gelu(pre).astype(BF16), w2_ref[0], preferred_element_type=F32)
    yn = y * lax.rsqrt(jnp.mean(y * y, axis=-1, keepdims=True) + EPS) * g_ref[...]
    o_ref[0, 0, 0] = jnp.where(pl.program_id(0) == 0, yn, y).astype(o_ref.dtype)


def nsa_compress(z, bsz, seq, cmp_pos, w1k, w2k, w1v, w2v, knorm_g):
    ncp = seq // CMP_STRIDE
    kv = z[:, Z_KV:Z_KV + 512].reshape(bsz, seq, 2, NSA_G, NSA_DH)
    kv = kv.transpose(2, 0, 3, 1, 4).reshape(2, bsz, NSA_G, ncp, CMP_STRIDE * NSA_DH)
    pos = cmp_pos.reshape(1, 1, CMP_LEN * NSA_DH)
    w1 = jnp.stack([w1k, w1v]).astype(BF16)
    w2 = jnp.stack([w2k, w2v]).astype(BF16)
    kdim = CMP_STRIDE * NSA_DH
    return pl.pallas_call(
        _compress_kernel,
        grid=(2, bsz, NSA_G),
        in_specs=[pl.BlockSpec((1, 1, 1, ncp, kdim), lambda t, b, g: (t, b, g, 0, 0)),
                  pl.BlockSpec((1, 1, 2 * kdim), lambda t, b, g: (0, 0, 0)),
                  pl.BlockSpec((1, 2 * kdim, NSA_DH), lambda t, b, g: (t, 0, 0)),
                  pl.BlockSpec((1, NSA_DH, NSA_DH), lambda t, b, g: (t, 0, 0)),
                  pl.BlockSpec((1, NSA_DH), lambda t, b, g: (0, 0))],
        out_specs=pl.BlockSpec((1, 1, 1, ncp, NSA_DH), lambda t, b, g: (t, b, g, 0, 0)),
        out_shape=jax.ShapeDtypeStruct((2, bsz, NSA_G, ncp, NSA_DH), BF16),
        compiler_params=_cp("arbitrary", "arbitrary", "arbitrary"),
        name="nsa_compress",
    )(kv, pos, w1, w2, knorm_g.reshape(1, NSA_DH))


def _dist_tile(bt_ref, h, rho):
    return bt_ref[0, h, jnp.clip(rho, -1, N_DIST_TILES - 1) + 1]


def _nsa_kernel(q_ref, kc_ref, vc_ref, rel_ref, ovt_ref, ks_ref, vs_ref, et_ref, kw_ref, vw_ref, bt_ref, gt_ref,
                o_ref, sc_ref, *state):
    m_refs, acc_refs = state[0:NSA_HPG], state[NSA_HPG:2 * NSA_HPG]
    j = pl.program_id(2)
    QB, HPG, DH = Q_BLOCK, NSA_HPG, NSA_DH
    R = HPG * QB
    ncp = kc_ref.shape[3]
    nsel = ovt_ref.shape[0]
    q = q_ref[0].reshape(R, DH)
    nt = (((1,), (1,)), ((), ()))
    qi = lax.broadcasted_iota(jnp.int32, (QB, 1), 0)
    tpos = j * QB + qi

    cidx = lax.broadcasted_iota(jnp.int32, (QB, ncp), 1)
    mask_c = jnp.logical_and(cidx * CMP_STRIDE + (CMP_LEN - 1) <= tpos, cidx < ncp - 1)
    has_c = tpos >= CMP_LEN - 1
    band0 = (j * (QB // CMP_STRIDE) + ncp - (LANES - QB // CMP_STRIDE)) % ncp
    s_c = lax.dot_general(q, kc_ref[0, 0, 0], nt, preferred_element_type=F32)
    p_c = []
    for h in range(HPG):
        sat = bt_ref[0, h, N_DIST_TILES]
        bias = jnp.concatenate([rel_ref[0, h]] + [sat] * (ncp // LANES - 1), axis=1)
        bias = pltpu.roll(bias, band0, 1)
        sh = jnp.where(mask_c, s_c[h * QB:(h + 1) * QB, :] + bias, NEG_INF)
        mx = jnp.max(sh, axis=-1, keepdims=True)
        e = jnp.exp2(sh - mx)
        den = jnp.sum(e, axis=-1, keepdims=True)
        p_c.append(e * jnp.where(has_c, 1.0 / den, 0.0))
    o_c = jnp.dot(jnp.concatenate(p_c, axis=0).astype(BF16), vc_ref[0, 0, 0], preferred_element_type=F32)

    psum = p_c[0] + p_c[1] + p_c[2] + p_c[3]
    p_hi = psum.astype(BF16)
    p_lo = (psum - p_hi.astype(F32)).astype(BF16)
    imp = (lax.dot_general(ovt_ref[...], p_hi, nt, preferred_element_type=F32)
           + lax.dot_general(ovt_ref[...], p_lo, nt, preferred_element_type=F32))
    nidx = lax.broadcasted_iota(jnp.int32, (nsel, QB), 0)
    tpos_l = j * QB + lax.broadcasted_iota(jnp.int32, (1, QB), 1)
    blk_t = tpos_l // SLC_LEN
    valid = nidx * SLC_LEN <= tpos_l
    forced = (nidx == 0) | (nidx == blk_t) | (nidx == blk_t - 1)
    nidx_f = nidx.astype(F32)
    n_forced = 3

    def select_all_valid():
        return jnp.where(valid, 0.0, NEG_INF)

    def select_rounds():
        score = jnp.where(jnp.logical_and(valid, jnp.logical_not(forced)), imp, NEG_INF)
        sel = jnp.where(forced, 0.0, NEG_INF)
        for _ in range(NSA_N_SELECT - n_forced):
            best = jnp.max(score, axis=0, keepdims=True)
            first = jnp.min(jnp.where(score == best, nidx_f, float(nsel)), axis=0, keepdims=True)
            hit = nidx_f == first
            sel = jnp.where(hit, 0.0, sel)
            score = jnp.where(hit, -jnp.inf, score)
        return sel

    sel_t = lax.cond((j + 1) * QB <= NSA_N_SELECT * SLC_LEN, select_all_valid, select_rounds)
    sel_neg = sel_t.T.astype(BF16)

    for h in range(HPG):
        m_refs[h][...] = jnp.full((QB, LANES), -jnp.inf, F32)
        acc_refs[h][...] = jnp.zeros((QB, 2 * DH), F32)
    KT = SEL_TILE
    sub = KT // QB
    q_sel = jnp.concatenate([q, jnp.concatenate([sel_neg] * HPG, axis=0)], axis=1)

    def lane_fold(x, op):
        parts = [x[:, u * LANES:(u + 1) * LANES] for u in range(KT // LANES)]
        while len(parts) > 1:
            parts = [op(parts[a], parts[a + 1]) for a in range(0, len(parts), 2)]
        return parts[0]

    def score_step(t, carry):
        k0 = pl.multiple_of(t * KT, KT)
        k_sel = jnp.concatenate([ks_ref[0, 0, pl.ds(k0, KT), :], et_ref[pl.ds(k0, KT), :]], axis=1)
        s = lax.dot_general(q_sel, k_sel, nt, preferred_element_type=F32)
        for h in range(HPG):
            bias = jnp.concatenate([_dist_tile(bt_ref, h, j - t * sub - u) for u in range(sub)], axis=1)
            sh = s[h * QB:(h + 1) * QB, :] + bias
            sc_ref[t, h * QB:(h + 1) * QB, :] = sh
            m_refs[h][...] = jnp.maximum(m_refs[h][...], lane_fold(sh, jnp.maximum))
        return carry

    ones_col = jnp.ones((KT, DH), BF16)

    def value_step(t, carry):
        k0 = pl.multiple_of(t * KT, KT)
        v_ext = jnp.concatenate([vs_ref[0, 0, pl.ds(k0, KT), :], ones_col], axis=1)
        for h in range(HPG):
            mb = m_refs[h][...]
            x = sc_ref[t, h * QB:(h + 1) * QB, :] - jnp.concatenate([mb] * (KT // LANES), axis=1)
            acc_refs[h][...] += jnp.dot(jnp.exp2(x.astype(BF16)), v_ext, preferred_element_type=F32)
        return carry

    n_pairs = ((j * QB) // KT + 2) // 2

    def pair(step):
        def body(p, carry):
            return step(2 * p + 1, step(2 * p, carry))
        return body

    lax.fori_loop(0, n_pairs, pair(score_step), 0)
    for h in range(HPG):
        m_refs[h][...] = jnp.broadcast_to(jnp.max(m_refs[h][...], axis=-1, keepdims=True), (QB, LANES))
    lax.fori_loop(0, n_pairs, pair(value_step), 0)
    o_s = [acc_refs[h][:, 0:DH] / acc_refs[h][:, DH:2 * DH] for h in range(HPG)]

    WK = WINDOW + QB
    w0 = pl.multiple_of(j * QB, QB)
    k_w = kw_ref[0, 0, pl.ds(w0, WK), :]
    v_w = vw_ref[0, 0, pl.ds(w0, WK), :]
    s_w = lax.dot_general(q, k_w, nt, preferred_element_type=F32)
    nwt = WK // QB
    tab_w = [jnp.where(j + u >= nwt - 1, TAB_WINDOW_FAR if u == 0 else nwt - u, TAB_MASKED) for u in range(nwt)]
    o_w = []
    for h in range(HPG):
        bias = jnp.concatenate([bt_ref[0, h, tab_w[u]] for u in range(nwt)], axis=1)
        sh = s_w[h * QB:(h + 1) * QB, :] + bias
        mxw = jnp.max(sh, axis=-1, keepdims=True)
        pw = jnp.exp2(sh - mxw)
        lw = jnp.sum(pw, axis=-1, keepdims=True)
        o_w.append(jnp.dot(pw.astype(BF16), v_w, preferred_element_type=F32) / lw)

    gate = jax.nn.sigmoid(gt_ref[...])
    for h in range(HPG):
        gc = gate[:, 8 + 3 * h:9 + 3 * h]
        gs = gate[:, 9 + 3 * h:10 + 3 * h]
        gw = gate[:, 10 + 3 * h:11 + 3 * h]
        y = gc * o_c[h * QB:(h + 1) * QB, :] + gs * o_s[h] + gw * o_w[h]
        o_ref[:, h * DH:(h + 1) * DH] = y.astype(o_ref.dtype)


def nsa_tables(rel_bias, seq):
    def bias_of(dist):
        dist = jnp.maximum(dist, 0)
        max_exact = NUM_BUCKETS // 2
        log_ratio = jnp.log(jnp.maximum(dist, 1).astype(F32) / max_exact) / math.log(MAX_DISTANCE / max_exact)
        large = jnp.minimum(max_exact + (log_ratio * (NUM_BUCKETS - max_exact)).astype(jnp.int32), NUM_BUCKETS - 1)
        bucket = jnp.where(dist < max_exact, dist, large)
        onehot = (bucket[..., None] == jnp.arange(NUM_BUCKETS)).astype(F32)
        out = jnp.dot(onehot, rel_bias.astype(F32), precision=lax.Precision.HIGHEST)
        return jnp.moveaxis(out, -1, 0)

    assert seq // CMP_STRIDE >= LANES and Q_BLOCK * (N_DIST_TILES - 1) - (Q_BLOCK - 1) >= MAX_DISTANCE
    i = jnp.arange(Q_BLOCK)
    delta = i[None, :, None] - i[None, None, :]
    dist = jnp.arange(N_DIST_TILES)[:, None, None] * Q_BLOCK + delta
    tiles = bias_of(dist) * LOG2E
    causal = jnp.where(dist >= 0, tiles, NEG_INF)
    far = jnp.where(delta < 0, tiles[:, WINDOW // Q_BLOCK], NEG_INF)[:, None]
    masked = jnp.full_like(far, NEG_INF)
    bt = jnp.concatenate([masked, causal, far], axis=1).reshape(NSA_G, NSA_HPG, N_TAB, Q_BLOCK, Q_BLOCK)
    per_tile = Q_BLOCK // CMP_STRIDE
    assert (LANES - per_tile) * CMP_STRIDE - (CMP_LEN - 1) >= MAX_DISTANCE
    dist_c = i[:, None] - ((jnp.arange(LANES)[None, :] - (LANES - per_tile)) * CMP_STRIDE + CMP_LEN - 1)
    rel = (bias_of(dist_c) * LOG2E).reshape(NSA_G, NSA_HPG, Q_BLOCK, LANES)
    return bt, rel


def nsa_block_indicator(seq):
    nsel = seq // SLC_LEN
    return jnp.asarray(np.arange(seq)[:, None] // SLC_LEN == np.arange(nsel)[None, :], BF16)


def nsa_overlap(seq):
    ncp = seq // CMP_STRIDE
    ns = seq // SLC_LEN
    cstart = np.arange(ncp) * CMP_STRIDE
    sstart = np.arange(ns) * SLC_LEN
    ov = np.clip(np.minimum(cstart[:, None] + CMP_LEN, sstart[None, :] + SLC_LEN)
                 - np.maximum(cstart[:, None], sstart[None, :]), 0, None).astype(np.float32) / CMP_LEN
    ov[ncp - 1] = 0.0
    return jnp.asarray(ov.T, BF16)


def nsa_attend(z, pq, pkv, pw, cmp, bt, rel, ovt, et, bsz, seq):
    nqb = seq // Q_BLOCK
    ncp = seq // CMP_STRIDE
    nsel = seq // SLC_LEN
    R = NSA_HPG * Q_BLOCK
    assert (seq // SEL_TILE) % 2 == 0
    return pl.pallas_call(
        _nsa_kernel,
        grid=(bsz, NSA_G, nqb),
        in_specs=[pl.BlockSpec((1, NSA_HPG, Q_BLOCK, NSA_DH), lambda b, g, j: (b, g, j, 0)),
                  pl.BlockSpec((1, 1, 1, ncp, NSA_DH), lambda b, g, j: (0, b, g, 0, 0)),
                  pl.BlockSpec((1, 1, 1, ncp, NSA_DH), lambda b, g, j: (1, b, g, 0, 0)),
                  pl.BlockSpec((1, NSA_HPG, Q_BLOCK, LANES), lambda b, g, j: (g, 0, 0, 0)),
                  pl.BlockSpec((nsel, ncp), lambda b, g, j: (0, 0)),
                  pl.BlockSpec((1, 1, seq, NSA_DH), lambda b, g, j: (b, g, 0, 0)),
                  pl.BlockSpec((1, 1, seq, NSA_DH), lambda b, g, j: (b, NSA_G + g, 0, 0)),
                  pl.BlockSpec((seq, nsel), lambda b, g, j: (0, 0)),
                  pl.BlockSpec((1, 1, seq + WINDOW, NSA_DH), lambda b, g, j: (b, g, 0, 0)),
                  pl.BlockSpec((1, 1, seq + WINDOW, NSA_DH), lambda b, g, j: (b, NSA_G + g, 0, 0)),
                  pl.BlockSpec((1, NSA_HPG, N_TAB, Q_BLOCK, Q_BLOCK), lambda b, g, j: (g, 0, 0, 0, 0)),
                  pl.BlockSpec((Q_BLOCK, LANES), lambda b, g, j: (b * nqb + j, Z_SMALL // LANES + g))],
        out_specs=pl.BlockSpec((Q_BLOCK, NSA_HPG * NSA_DH), lambda b, g, j: (b * nqb + j, g)),
        out_shape=jax.ShapeDtypeStruct((bsz * seq, BRANCH_WIDTH), BF16),
        scratch_shapes=([pltpu.VMEM((seq // SEL_TILE, R, SEL_TILE), F32)]
                        + [pltpu.VMEM((Q_BLOCK, LANES), F32)] * NSA_HPG
                        + [pltpu.VMEM((Q_BLOCK, 2 * NSA_DH), F32)] * NSA_HPG),
        compiler_params=_cp("arbitrary", "arbitrary", "arbitrary"),
        name="nsa_attend",
    )(pq, cmp, cmp, rel, ovt, pkv, pkv, et, pw, pw, bt, z)


def _merge_kernel(ya_ref, yb_ref, yc_ref, w_ref, ga_ref, gb_ref, gc_ref, o_ref):
    acc = jax.nn.sigmoid(ga_ref[...]) * jnp.dot(ya_ref[...], w_ref[0], preferred_element_type=F32)
    acc = acc + jax.nn.sigmoid(gb_ref[...]) * jnp.dot(yb_ref[...], w_ref[1], preferred_element_type=F32)
    acc = acc + jax.nn.sigmoid(gc_ref[...]) * jnp.dot(yc_ref[...], w_ref[2], preferred_element_type=F32)
    o_ref[...] = acc.astype(o_ref.dtype)


def merge(ya, yb, yc, w_branch, z, tm=1024, tn=512):
    n = ya.shape[0]
    d = w_branch.shape[2]
    bw = ya.shape[1]
    ymap = lambda i, j: (i, 0)
    gmap = lambda k: (lambda i, j: (i, (Z_GATE + k * d) // tn + j))
    return pl.pallas_call(
        _merge_kernel,
        grid=(n // tm, d // tn),
        in_specs=[pl.BlockSpec((tm, bw), ymap), pl.BlockSpec((tm, bw), ymap), pl.BlockSpec((tm, bw), ymap),
                  pl.BlockSpec((3, bw, tn), lambda i, j: (0, 0, j)),
                  pl.BlockSpec((tm, tn), gmap(0)), pl.BlockSpec((tm, tn), gmap(1)), pl.BlockSpec((tm, tn), gmap(2))],
        out_specs=pl.BlockSpec((tm, tn), lambda i, j: (i, j)),
        out_shape=jax.ShapeDtypeStruct((n, d), BF16),
        compiler_params=_cp("arbitrary", "arbitrary"),
        name="merge",
    )(ya, yb, yc, w_branch, z, z, z)


def _outproj_kernel(a_ref, w_ref, x_ref, g_ref, o_ref):
    o_ref[...] = x_ref[...] + g_ref[0] * jnp.dot(a_ref[...], w_ref[...], preferred_element_type=F32)


def out_proj(a, w, x2d, gate, seq, tm=1024, tn=512):
    n, k = a.shape
    d = w.shape[1]
    per_b = seq // tm
    return pl.pallas_call(
        _outproj_kernel,
        grid=(n // tm, d // tn),
        in_specs=[pl.BlockSpec((tm, k), lambda i, j: (i, 0)),
                  pl.BlockSpec((k, tn), lambda i, j: (0, j)),
                  pl.BlockSpec((tm, tn), lambda i, j: (i, j)),
                  pl.BlockSpec((1, 1, tn), lambda i, j: (i // per_b, 0, j))],
        out_specs=pl.BlockSpec((tm, tn), lambda i, j: (i, j)),
        out_shape=jax.ShapeDtypeStruct((n, d), F32),
        compiler_params=_cp("arbitrary", "arbitrary"),
        name="out_proj",
    )(a, w, x2d, gate[:, None, :])


def _pack_bf16_pairs(y):
    half = y.shape[1] // 2
    lo = lax.bitcast_convert_type(y[:, :half].astype(BF16).astype(F32), jnp.uint32)
    hi = lax.bitcast_convert_type(y[:, half:].astype(BF16).astype(F32), jnp.uint32)
    return lax.shift_right_logical(lo, jnp.uint32(16)) | (hi & jnp.uint32(0xFFFF0000))


def _unpack_bf16_pairs(w):
    lo = lax.bitcast_convert_type(lax.shift_left(w, jnp.uint32(16)), F32)
    hi = lax.bitcast_convert_type(w & jnp.uint32(0xFFFF0000), F32)
    return jnp.concatenate([lo, hi], axis=1)


def _router_kernel(x_ref, g_ref, sc_ref, sh_ref, rw_ref, rwl_ref, rb_ref, h_ref, idx_ref, p_ref):
    x = x_ref[...]
    y = x * lax.rsqrt(jnp.mean(x * x, axis=-1, keepdims=True) + EPS)
    hf = (y * g_ref[...]) * (1.0 + sc_ref[0]) + sh_ref[0]
    h = hf.astype(BF16)
    h_ref[...] = _pack_bf16_pairs(hf)
    h_lo = (hf - h.astype(F32)).astype(BF16)
    logits = (jnp.dot(h, rw_ref[...], preferred_element_type=F32)
              + jnp.dot(h_lo, rw_ref[...], preferred_element_type=F32)
              + jnp.dot(h, rwl_ref[...], preferred_element_type=F32)) + rb_ref[...]
    lane = lax.broadcasted_iota(jnp.int32, logits.shape, 1)
    lane_f = lane.astype(F32)
    idx_out = jnp.zeros(logits.shape, F32)
    val_out = jnp.full(logits.shape, -jnp.inf, F32)
    for k in range(TOP_K):
        best = jnp.max(logits, axis=-1, keepdims=True)
        first = jnp.min(jnp.where(logits == best, lane_f, float(LANES)), axis=-1, keepdims=True)
        idx_out = jnp.where(lane == k, first, idx_out)
        val_out = jnp.where(lane == k, best, val_out)
        logits = jnp.where(lane_f == first, -jnp.inf, logits)
    e = jnp.exp(val_out - jnp.max(val_out, axis=-1, keepdims=True))
    idx_ref[...] = idx_out.astype(jnp.int32)
    p_ref[...] = e / jnp.sum(e, axis=-1, keepdims=True)


def router(x2d, g, sc, sh, rw, rb, seq, tm=1024):
    n, d = x2d.shape
    per_b = seq // tm
    rw_hi = rw.astype(BF16)
    rw_lo = (rw - rw_hi.astype(F32)).astype(BF16)
    rwp = jnp.zeros((d, LANES), BF16).at[:, :N_EXPERTS].set(rw_hi)
    rwl = jnp.zeros((d, LANES), BF16).at[:, :N_EXPERTS].set(rw_lo)
    rbp = jnp.full((1, LANES), -jnp.inf, F32).at[0, :N_EXPERTS].set(rb)
    return pl.pallas_call(
        _router_kernel,
        grid=(n // tm,),
        in_specs=[pl.BlockSpec((tm, d), lambda i: (i, 0)),
                  pl.BlockSpec((1, d), lambda i: (0, 0)),
                  pl.BlockSpec((1, 1, d), lambda i: (i // per_b, 0, 0)),
                  pl.BlockSpec((1, 1, d), lambda i: (i // per_b, 0, 0)),
                  pl.BlockSpec((d, LANES), lambda i: (0, 0)),
                  pl.BlockSpec((d, LANES), lambda i: (0, 0)),
                  pl.BlockSpec((1, LANES), lambda i: (0, 0))],
        out_specs=[pl.BlockSpec((tm, d // 2), lambda i: (i, 0)),
                   pl.BlockSpec((tm, LANES), lambda i: (i, 0)),
                   pl.BlockSpec((tm, LANES), lambda i: (i, 0))],
        out_shape=[jax.ShapeDtypeStruct((n, d // 2), jnp.uint32),
                   jax.ShapeDtypeStruct((n, LANES), jnp.int32),
                   jax.ShapeDtypeStruct((n, LANES), F32)],
        compiler_params=_cp("arbitrary"),
        name="router",
    )(x2d, g.reshape(1, d), sc[:, None, :], sh[:, None, :], rwp, rwl, rbp)


def _dispatch_kernel(dest_ref, h_ref, init_ref, xb_ref, sem):
    del init_ref
    tm = h_ref.shape[0]

    def row_copy(k, r):
        dst = dest_ref[0, 0, k * tm + r]
        return pltpu.make_async_copy(h_ref.at[pl.ds(r, 1), :], xb_ref.at[pl.ds(dst, 1), :], sem)

    def start(r, carry):
        for k in range(TOP_K):
            row_copy(k, r).start()
        return carry

    def wait(r, carry):
        for k in range(TOP_K):
            row_copy(k, r).wait()
        return carry

    lax.fori_loop(0, tm, start, 0, unroll=8)
    lax.fori_loop(0, tm, wait, 0, unroll=8)


def dispatch(hp, dest_tiles, init):
    n, w = hp.shape
    rows = init.shape[0]
    nb, _, per_tile = dest_tiles.shape
    tm = per_tile // TOP_K
    return pl.pallas_call(
        _dispatch_kernel,
        grid=(nb,),
        in_specs=[pl.BlockSpec((1, 1, per_tile), lambda i: (i, 0, 0), memory_space=pltpu.SMEM),
                  pl.BlockSpec((tm, w), lambda i: (i, 0)),
                  pl.BlockSpec(memory_space=pl.ANY)],
        out_specs=pl.BlockSpec(memory_space=pl.ANY),
        out_shape=jax.ShapeDtypeStruct((rows, w), jnp.uint32),
        scratch_shapes=[pltpu.SemaphoreType.DMA(())],
        input_output_aliases={2: 0},
        compiler_params=_cp("arbitrary"),
        name="dispatch",
    )(dest_tiles, hp, init)


def _expert_kernel(be_ref, nv_ref, x_ref, wg_ref, wu_ref, bg_ref, bu_ref, w2_ref, b2_ref, o_ref, acc_ref, xs_ref):
    i = pl.program_id(0)
    f = pl.program_id(1)

    @pl.when(i < nv_ref[0])
    def _():
        @pl.when(f == 0)
        def _():
            acc_ref[...] = jnp.zeros(acc_ref.shape, F32)
            xs_ref[...] = _unpack_bf16_pairs(x_ref[...]).astype(BF16)

        x = xs_ref[...]
        gate = jnp.dot(x, wg_ref[0, 0].astype(BF16), preferred_element_type=F32) + bg_ref[0, 0]
        up = jnp.dot(x, wu_ref[0, 0].astype(BF16), preferred_element_type=F32) + bu_ref[0, 0]
        gate = jnp.minimum(gate, SWIGLU_LIMIT)
        up = jnp.clip(up, -SWIGLU_LIMIT, SWIGLU_LIMIT)
        act = (up + 1.0) * (gate * jax.nn.sigmoid(SWIGLU_ALPHA * gate))
        acc_ref[...] += jnp.dot(act.astype(BF16), w2_ref[0, 0].astype(BF16), preferred_element_type=F32)

        @pl.when(f == pl.num_programs(1) - 1)
        def _():
            o_ref[...] = _pack_bf16_pairs(acc_ref[...] + b2_ref[0, 0])

    @pl.when(i >= nv_ref[0])
    def _():
        o_ref[...] = jnp.zeros(o_ref.shape, o_ref.dtype)


def experts(xb, block_exp, n_valid, w1, b1, w2, b2, layer, tm, tf=256):
    rows = xb.shape[0]
    d = 2 * xb.shape[1]
    nf = D_FF // tf
    nblk = rows // tm
    n_exp = w1.shape[1]
    fe = lambda i, f, nv: jnp.where(i < nv[0], f, nf - 1)
    grid_spec = pltpu.PrefetchScalarGridSpec(
        num_scalar_prefetch=2,
        grid=(nblk, nf),
        in_specs=[pl.BlockSpec((tm, d // 2), lambda i, f, be, nv: (jnp.minimum(i, nv[0] - 1), 0)),
                  pl.BlockSpec((1, 1, d, tf), lambda i, f, be, nv: (layer, be[i], 0, fe(i, f, nv))),
                  pl.BlockSpec((1, 1, d, tf), lambda i, f, be, nv: (layer, be[i], 0, nf + fe(i, f, nv))),
                  pl.BlockSpec((1, 1, 1, tf), lambda i, f, be, nv: (layer * n_exp + be[i], 0, 0, fe(i, f, nv))),
                  pl.BlockSpec((1, 1, 1, tf), lambda i, f, be, nv: (layer * n_exp + be[i], 0, 0, nf + fe(i, f, nv))),
                  pl.BlockSpec((1, 1, tf, d), lambda i, f, be, nv: (layer, be[i], fe(i, f, nv), 0)),
                  pl.BlockSpec((1, 1, 1, d), lambda i, f, be, nv: (layer * n_exp + be[i], 0, 0, 0))],
        out_specs=pl.BlockSpec((tm, d // 2), lambda i, f, be, nv: (i, 0)),
        scratch_shapes=[pltpu.VMEM((tm, d), F32), pltpu.VMEM((tm, d), BF16)],
    )
    b1r = b1.reshape(-1, 1, 1, b1.shape[-1])
    b2r = b2.reshape(-1, 1, 1, d)
    return pl.pallas_call(
        _expert_kernel,
        grid_spec=grid_spec,
        out_shape=jax.ShapeDtypeStruct((rows, d // 2), jnp.uint32),
        compiler_params=_cp("arbitrary", "arbitrary"),
        name="experts",
    )(block_exp, n_valid, xb, w1, w1, b1r, b1r, w2, b2r)


def _combine_kernel(dest_ref, p_ref, x_ref, g_ref, yb_ref, o_ref, ybuf_ref, sem):
    tm = x_ref.shape[0]

    def row_copy(k, r):
        src = dest_ref[0, 0, k * tm + r]
        return pltpu.make_async_copy(yb_ref.at[pl.ds(src, 1), :], ybuf_ref.at[k, pl.ds(r, 1), :], sem)

    def start(r, carry):
        for k in range(TOP_K):
            row_copy(k, r).start()
        return carry

    def wait(r, carry):
        for k in range(TOP_K):
            row_copy(k, r).wait()
        return carry

    lax.fori_loop(0, tm, start, 0, unroll=8)
    lax.fori_loop(0, tm, wait, 0, unroll=8)
    p = p_ref[...]
    y = p[:, 0:1] * _unpack_bf16_pairs(ybuf_ref[0])
    for k in range(1, TOP_K):
        y = y + p[:, k:k + 1] * _unpack_bf16_pairs(ybuf_ref[k])
    o_ref[...] = x_ref[...] + g_ref[0] * y


def combine(yb, dest_tiles, prob, x2d, gate, seq):
    n, d = x2d.shape
    nb = dest_tiles.shape[0]
    tm = n // nb
    per_b = seq // tm
    return pl.pallas_call(
        _combine_kernel,
        grid=(nb,),
        in_specs=[pl.BlockSpec((1, 1, TOP_K * tm), lambda i: (i, 0, 0), memory_space=pltpu.SMEM),
                  pl.BlockSpec((tm, LANES), lambda i: (i, 0)),
                  pl.BlockSpec((tm, d), lambda i: (i, 0)),
                  pl.BlockSpec((1, 1, d), lambda i: (i // per_b, 0, 0)),
                  pl.BlockSpec(memory_space=pl.ANY)],
        out_specs=pl.BlockSpec((tm, d), lambda i: (i, 0)),
        out_shape=jax.ShapeDtypeStruct((n, d), F32),
        scratch_shapes=[pltpu.VMEM((TOP_K, tm, d // 2), jnp.uint32), pltpu.SemaphoreType.DMA(())],
        compiler_params=_cp("arbitrary"),
        name="combine",
    )(dest_tiles, prob, x2d, gate[:, None, :], yb)


def moe(x2d, g, sc, sh, gate, rw, rb, w1, b1, w2, b2, layer, seq, xb_init=None, tm_e=1024, tm_t=512):
    n, d = x2d.shape
    h, idx, prob = router(x2d, g, sc, sh, rw, rb, seq)
    e_flat = idx[:, :TOP_K].T.reshape(-1)
    na = n * TOP_K
    onehot = (e_flat[:, None] == jnp.arange(N_EXPERTS)[None, :]).astype(jnp.int32)
    csum = jnp.cumsum(onehot, axis=0)
    counts = csum[-1]
    rank = jnp.take_along_axis(csum, e_flat[:, None], axis=1)[:, 0] - 1
    padded = (counts + tm_e - 1) // tm_e * tm_e
    pend = jnp.cumsum(padded)
    pstart = pend - padded
    dest = pstart[e_flat] + rank
    nblk = na // tm_e + N_EXPERTS
    rows = nblk * tm_e
    starts = jnp.arange(nblk, dtype=jnp.int32) * tm_e
    block_exp = jnp.minimum(jnp.sum(pend[None, :] <= starts[:, None], axis=1), N_EXPERTS - 1).astype(jnp.int32)
    n_valid = (pend[-1:] // tm_e).astype(jnp.int32)
    block_exp = jnp.where(starts < pend[-1], block_exp, block_exp[jnp.maximum(n_valid[0] - 1, 0)])
    nb = n // tm_t
    dest_tiles = dest.reshape(TOP_K, nb, tm_t).transpose(1, 0, 2).reshape(nb, 1, TOP_K * tm_t)
    if xb_init is None:
        xb_init = jnp.zeros((rows, h.shape[1]), jnp.uint32)
    xb = dispatch(h, dest_tiles, xb_init)
    yb = experts(xb, block_exp, n_valid, w1, b1, w2, b2, layer, tm_e)
    return combine(yb, dest_tiles, prob, x2d, gate, seq), xb


def fused_in_weight(w_in, w_gate):
    sizes = (512, 512, 1024, 4, 4, 1024, 1024, 256, 256, 256, 256, 256, 256, 24, 1024, 1024)
    pts = np.cumsum(sizes)[:-1].tolist()
    (qm, km, vm, ig, fg, og, qn, kc, vc, ks, vs, kw, vw, gn, gu, gv) = jnp.split(w_in, pts, axis=1)
    d = w_in.shape[0]
    pad = lambda k: jnp.zeros((d, k), w_in.dtype)
    per_group = NSA_HPG * 3
    small0 = jnp.concatenate([ig, fg, gn[:, :per_group], pad(LANES - 8 - per_group)], axis=1)
    small1 = jnp.concatenate([pad(8), gn[:, per_group:], pad(LANES - 8 - per_group)], axis=1)
    cols = [qm, km, vm, og, qn, gu, gv, kc, vc, ks, vs, kw, vw, w_gate[0], w_gate[1], w_gate[2], small0, small1]
    w = jnp.concatenate(cols, axis=1)
    w = jnp.concatenate([w, pad(Z_WIDTH - w.shape[1])], axis=1)
    return w.astype(BF16)


def kernel(x, c, ada_w, ada_b, norm1_g, norm2_g, w_in, conv_w, conv_b, mlstm_gate_b, mlstm_norm_g, cmp_pos,
           cmp_k_w1, cmp_k_w2, cmp_v_w1, cmp_v_w2, qnorm_g, knorm_g, rel_bias, gmlp_norm_g, gmlp_ws, gmlp_b,
           w_branch, w_gate, w_out, router_w, router_b, exp_w1, exp_b1, exp_w2, exp_b2):
    bsz, seq, d = x.shape
    depth = ada_w.shape[0]
    x2d = x.reshape(bsz * seq, d)
    bt, rel = nsa_tables(rel_bias, seq)
    ovt = nsa_overlap(seq)
    et = nsa_block_indicator(seq)
    xb = None
    for l in range(depth):
        mod = ada_mod(c, ada_w, ada_b, l)
        sh1, sc1, g1, sh2, sc2, g2 = jnp.split(mod, 6, axis=-1)
        z = in_proj(x2d, norm1_g[l], sc1, sh1, fused_in_weight(w_in[l], w_gate[l]), seq)
        ya = mlstm(z, bsz, seq, conv_w[l], conv_b[l], mlstm_gate_b[l], mlstm_norm_g[l])
        pq, pkv, pw = nsa_prep(z, bsz, seq, qnorm_g[l], knorm_g[l])
        cmp = nsa_compress(z, bsz, seq, cmp_pos[l], cmp_k_w1[l], cmp_k_w2[l], cmp_v_w1[l], cmp_v_w2[l], knorm_g[l])
        yb = nsa_attend(z, pq, pkv, pw, cmp, bt, rel, ovt, et, bsz, seq)
        yc = gmlp(z, gmlp_norm_g[l], gmlp_ws[l], gmlp_b[l])
        merged = merge(ya, yb, yc, w_branch[l].astype(BF16), z)
        x2d = out_proj(merged, w_out[l].astype(BF16), x2d, g1, seq)
        x2d, xb = moe(x2d, norm2_g[l], sc2, sh2, g2, router_w[l], router_b[l], exp_w1, exp_b1, exp_w2, exp_b2, l, seq,
                      xb_init=xb)
    return x2d.reshape(bsz, seq, d)
```

```python
import functools
import math

import jax
import jax.numpy as jnp
import numpy as np
from jax import lax
from jax.experimental import pallas as pl
from jax.experimental.pallas import tpu as pltpu

F32 = jnp.float32
BF16 = jnp.bfloat16

LANES = 128
VMEM_LIMIT = 56 * 1024 * 1024

D_MODEL = 2048
BRANCH_WIDTH = 1024
MLSTM_HEADS = 4
MLSTM_DV = 256
MLSTM_DQK = 128
MLSTM_TILE = 256
CONV_WIDTH = 4
NSA_HEADS = 8
NSA_G = 2
NSA_HPG = 4
NSA_DH = 128
CMP_LEN = 32
CMP_STRIDE = 16
SLC_LEN = 64
NSA_N_SELECT = 16
WINDOW = 512
Q_BLOCK = 128
NUM_BUCKETS = 32
MAX_DISTANCE = 1024
GMLP_CHUNK = 128
GMLP_GROUPS = 8
N_EXPERTS = 32
TOP_K = 4
D_FF = 1536
SWIGLU_LIMIT = 7.0
SWIGLU_ALPHA = 1.702
NEG_INF = -1e30
FORCE_SELECT = 1e4
EPS = 1e-6

LOG2E = math.log2(math.e)
SEL_TILE = 512
N_DIST_TILES = 10
TAB_MASKED, TAB_WINDOW_FAR, N_TAB = 0, N_DIST_TILES + 1, N_DIST_TILES + 2

Z_QK, Z_V, Z_OG, Z_NQ, Z_GU, Z_GV = 0, 1024, 2048, 3072, 4096, 5120
Z_KV = 6144
Z_GATE = 7680
Z_SMALL = 13824
Z_WIDTH = 14336


def _cp(*sem):
    return pltpu.CompilerParams(dimension_semantics=sem, vmem_limit_bytes=VMEM_LIMIT)


def _ada_kernel(c_ref, w_ref, b_ref, o_ref):
    a = jax.nn.silu(c_ref[...]).astype(BF16)
    o_ref[...] = jnp.dot(a, w_ref[0].astype(BF16), preferred_element_type=F32) + b_ref[0]


def ada_mod(c, w, b, layer):
    bsz, d = c.shape
    n = w.shape[2]
    tn = 1024
    cp = jnp.zeros((8, d), F32).at[:bsz].set(c)
    out = pl.pallas_call(
        _ada_kernel,
        grid=(n // tn,),
        in_specs=[pl.BlockSpec((8, d), lambda j: (0, 0)),
                  pl.BlockSpec((1, d, tn), lambda j: (layer, 0, j)),
                  pl.BlockSpec((1, 1, tn), lambda j: (layer, 0, j))],
        out_specs=pl.BlockSpec((8, tn), lambda j: (0, j)),
        out_shape=jax.ShapeDtypeStruct((8, n), F32),
        compiler_params=_cp("arbitrary"),
        name="ada_mod",
    )(cp, w, b[:, None, :])
    return out[:bsz]


def _in_kernel(x_ref, g_ref, sc_ref, sh_ref, w_ref, o_ref, h_ref):
    @pl.when(pl.program_id(1) == 0)
    def _():
        x = x_ref[...]
        y = x * lax.rsqrt(jnp.mean(x * x, axis=-1, keepdims=True) + EPS)
        h = (y * g_ref[...]) * (1.0 + sc_ref[0]) + sh_ref[0]
        h_ref[...] = h.astype(BF16)

    o_ref[...] = jnp.dot(h_ref[...], w_ref[...], preferred_element_type=F32)


def in_proj(x2d, g, sc, sh, wcat, seq, tm=1024, tn=1024):
    n, d = x2d.shape
    w = wcat.shape[1]
    per_b = seq // tm
    return pl.pallas_call(
        _in_kernel,
        grid=(n // tm, w // tn),
        in_specs=[pl.BlockSpec((tm, d), lambda i, j: (i, 0)),
                  pl.BlockSpec((1, d), lambda i, j: (0, 0)),
                  pl.BlockSpec((1, 1, d), lambda i, j: (i // per_b, 0, 0)),
                  pl.BlockSpec((1, 1, d), lambda i, j: (i // per_b, 0, 0)),
                  pl.BlockSpec((d, tn), lambda i, j: (0, j))],
        out_specs=pl.BlockSpec((tm, tn), lambda i, j: (i, j)),
        out_shape=jax.ShapeDtypeStruct((n, w), F32),
        scratch_shapes=[pltpu.VMEM((tm, d), BF16)],
        compiler_params=_cp("arbitrary", "arbitrary"),
        name="in_proj",
    )(x2d, g.reshape(1, d), sc[:, None, :], sh[:, None, :], wcat)


def _cumsum_rows(x, n):
    row = lax.broadcasted_iota(jnp.int32, x.shape, 0)
    s = 1
    while s < n:
        x = x + jnp.where(row >= s, pltpu.roll(x, s, 0), 0.0)
        s *= 2
    return x


def _mlstm_kernel(qk_ref, v_ref, og_ref, sm_ref, cw_ref, cb_ref, gb_ref, ng_ref, o_ref,
                  buf_ref, ct_ref, n_ref, m_ref):
    L = qk_ref.shape[0]
    H, DQK, DV = MLSTM_HEADS, MLSTM_DQK, MLSTM_DV

    @pl.when(pl.program_id(1) == 0)
    def _():
        buf_ref[0:8, :] = jnp.zeros((8, buf_ref.shape[1]), F32)
        ct_ref[...] = jnp.zeros(ct_ref.shape, F32)
        n_ref[...] = jnp.zeros(n_ref.shape, F32)
        m_ref[...] = jnp.full(m_ref.shape, NEG_INF, F32)

    buf_ref[8:8 + L, :] = qk_ref[...]
    conv = cb_ref[...] + cw_ref[3:4, :] * buf_ref[8:8 + L, :]
    for j in range(CONV_WIDTH - 1):
        conv = conv + cw_ref[j:j + 1, :] * buf_ref[5 + j:5 + j + L, :]
    qk = jax.nn.silu(conv)
    buf_ref[0:8, :] = buf_ref[L:L + 8, :]

    sm = sm_ref[...] + gb_ref[...]
    bcum = _cumsum_rows(jax.nn.log_sigmoid(sm), L)
    sm_t = sm.T
    bcum_t = bcum.T
    row = lax.broadcasted_iota(jnp.int32, (L, L), 0)
    col = lax.broadcasted_iota(jnp.int32, (L, L), 1)
    causal = row >= col

    for h in range(H):
        qf = qk[:, h * DQK:(h + 1) * DQK] * DQK ** -0.5
        q = qf.astype(BF16)
        kf = qk[:, (H + h) * DQK:(H + h + 1) * DQK]
        k = kf.astype(BF16)
        v = v_ref[:, h * DV:(h + 1) * DV].astype(BF16)
        b_col = bcum[:, H + h:H + h + 1]
        i_col = sm[:, h:h + 1]
        b_row = bcum_t[H + h:H + h + 1, :]
        i_row = sm_t[h:h + 1, :]
        m_prev = m_ref[h:h + 1, 0:1]
        a = b_col + m_prev
        dmat = jnp.where(causal, b_col - b_row + i_row, -jnp.inf)
        m_t = jnp.maximum(a, jnp.max(dmat, axis=1, keepdims=True))
        w_intra = jnp.exp(dmat - m_t)
        w_inter = jnp.exp(a - m_t)
        s = lax.dot_general(q, k, (((1,), (1,)), ((), ())), preferred_element_type=F32) * w_intra
        ct = ct_ref[h]
        nvec = n_ref[h:h + 1, :]
        num = jnp.dot(s.astype(BF16), v, preferred_element_type=F32) + w_inter * jnp.dot(
            q, ct.astype(BF16), preferred_element_type=F32)
        qn = jnp.sum(qf * nvec, axis=1, keepdims=True)
        den = jnp.sum(s, axis=1, keepdims=True) + w_inter * qn
        hm = num / jnp.maximum(jnp.abs(den), jnp.exp(-m_t))
        b_last = b_col[L - 1:L, :]
        gdec = b_last - b_col + i_col
        m_new = jnp.maximum(b_last + m_prev, jnp.max(gdec, axis=0, keepdims=True))
        ws = jnp.exp(gdec - m_new)
        wc = jnp.exp(b_last + m_prev - m_new)
        kw = ws * kf
        ct_ref[h] = wc * ct + jnp.dot(kw.T.astype(BF16), v, preferred_element_type=F32)
        n_ref[h:h + 1, :] = wc * nvec + jnp.sum(kw, axis=0, keepdims=True)
        m_ref[h:h + 1, :] = jnp.broadcast_to(m_new, (1, LANES))
        y = hm * lax.rsqrt(jnp.mean(hm * hm, axis=-1, keepdims=True) + EPS) * ng_ref[:, h * DV:(h + 1) * DV]
        og = og_ref[:, h * DV:(h + 1) * DV]
        o_ref[:, h * DV:(h + 1) * DV] = (jax.nn.sigmoid(og) * y).astype(o_ref.dtype)


def mlstm(z, bsz, seq, conv_w, conv_b, gate_b, norm_g):
    L = MLSTM_TILE
    nch = seq // L
    gb = jnp.zeros((1, LANES), F32).at[0, :2 * MLSTM_HEADS].set(gate_b)
    wide = BRANCH_WIDTH
    rowmap = lambda col: (lambda b, c: (b * nch + c, col))
    const = lambda b, c: (0, 0)
    return pl.pallas_call(
        _mlstm_kernel,
        grid=(bsz, nch),
        in_specs=[pl.BlockSpec((L, wide), rowmap(Z_QK // wide)),
                  pl.BlockSpec((L, wide), rowmap(Z_V // wide)),
                  pl.BlockSpec((L, wide), rowmap(Z_OG // wide)),
                  pl.BlockSpec((L, LANES), rowmap(Z_SMALL // LANES)),
                  pl.BlockSpec((CONV_WIDTH, wide), const),
                  pl.BlockSpec((1, wide), const),
                  pl.BlockSpec((1, LANES), const),
                  pl.BlockSpec((1, wide), const)],
        out_specs=pl.BlockSpec((L, wide), lambda b, c: (b * nch + c, 0)),
        out_shape=jax.ShapeDtypeStruct((bsz * seq, wide), BF16),
        scratch_shapes=[pltpu.VMEM((L + 8, wide), F32),
                        pltpu.VMEM((MLSTM_HEADS, MLSTM_DQK, MLSTM_DV), F32),
                        pltpu.VMEM((8, MLSTM_DQK), F32),
                        pltpu.VMEM((8, LANES), F32)],
        compiler_params=_cp("arbitrary", "arbitrary"),
        name="mlstm",
    )(z, z, z, z, conv_w, conv_b.reshape(1, wide), gb, norm_g.reshape(1, wide))


def _gmlp_kernel(u_ref, v_ref, ng_ref, ws_ref, b_ref, o_ref):
    T = u_ref.shape[0]
    C = GMLP_CHUNK
    u = jax.nn.gelu(u_ref[...])
    vg = jax.nn.gelu(v_ref[...])
    mu = jnp.mean(vg, axis=-1, keepdims=True)
    var = jnp.mean(jnp.square(vg - mu), axis=-1, keepdims=True)
    v = ((vg - mu) * lax.rsqrt(var + EPS) * ng_ref[...]).astype(BF16)
    row = lax.broadcasted_iota(jnp.int32, (C, C), 0)
    col = lax.broadcasted_iota(jnp.int32, (C, C), 1)
    for g in range(GMLP_GROUPS):
        wsc = jnp.where(row >= col, ws_ref[g], 0.0).astype(BF16)
        bcol = b_ref[:, g:g + 1]
        for n in range(T // C):
            vv = v[n * C:(n + 1) * C, g * C:(g + 1) * C]
            s = jnp.dot(wsc, vv, preferred_element_type=F32) + bcol
            o_ref[n * C:(n + 1) * C, g * C:(g + 1) * C] = (
                u[n * C:(n + 1) * C, g * C:(g + 1) * C] * s).astype(o_ref.dtype)


def gmlp(z, norm_g, ws, b, tg=512):
    n = z.shape[0]
    wide = BRANCH_WIDTH
    return pl.pallas_call(
        _gmlp_kernel,
        grid=(n // tg,),
        in_specs=[pl.BlockSpec((tg, wide), lambda i: (i, Z_GU // wide)),
                  pl.BlockSpec((tg, wide), lambda i: (i, Z_GV // wide)),
                  pl.BlockSpec((1, wide), lambda i: (0, 0)),
                  pl.BlockSpec((GMLP_GROUPS, GMLP_CHUNK, GMLP_CHUNK), lambda i: (0, 0, 0)),
                  pl.BlockSpec((GMLP_CHUNK, GMLP_GROUPS), lambda i: (0, 0))],
        out_specs=pl.BlockSpec((tg, wide), lambda i: (i, 0)),
        out_shape=jax.ShapeDtypeStruct((n, wide), BF16),
        compiler_params=_cp("arbitrary"),
        name="gmlp",
    )(z, z, norm_g.reshape(1, wide), ws, b.T)


def _prep_kernel(n_norm, npad, x_ref, g_ref, o_ref):
    nh = o_ref.shape[1]

    @pl.when(pl.program_id(1) < npad)
    def _():
        o_ref[...] = jnp.zeros(o_ref.shape, o_ref.dtype)

    @pl.when(pl.program_id(1) >= npad)
    def _():
        for h in range(nh):
            x = x_ref[:, h * LANES:(h + 1) * LANES]
            if h < n_norm:
                x = x * lax.rsqrt(jnp.mean(x * x, axis=-1, keepdims=True) + EPS) * g_ref[h]
            o_ref[0, h] = x.astype(o_ref.dtype)


def _prep_call(z, bsz, seq, col0, heads, n_norm, gains, pad, ts, name):
    width = heads * LANES
    nblk = seq // ts
    npad = pad // ts
    return pl.pallas_call(
        functools.partial(_prep_kernel, n_norm, npad),
        grid=(bsz, nblk + npad),
        in_specs=[pl.BlockSpec((ts, width), lambda b, s: (b * nblk + jnp.maximum(s - npad, 0), col0 // width)),
                  pl.BlockSpec((heads, 1, LANES), lambda b, s: (0, 0, 0))],
        out_specs=pl.BlockSpec((1, heads, ts, LANES), lambda b, s: (b, 0, s, 0)),
        out_shape=jax.ShapeDtypeStruct((bsz, heads, seq + pad, LANES), BF16),
        compiler_params=_cp("arbitrary", "arbitrary"),
        name=name,
    )(z, gains)


def nsa_prep(z, bsz, seq, qnorm_g, knorm_g):
    gq = jnp.tile((qnorm_g * (NSA_DH ** -0.5 * LOG2E)).reshape(1, 1, LANES), (NSA_HEADS, 1, 1))
    gk = jnp.tile(knorm_g.reshape(1, 1, LANES), (2 * NSA_G, 1, 1))
    pq = _prep_call(z, bsz, seq, Z_NQ, NSA_HEADS, NSA_HEADS, gq, 0, 1024, "nsa_prep_q")
    pkv = _prep_call(z, bsz, seq, Z_KV + 512, 2 * NSA_G, NSA_G, gk, 0, 1024, "nsa_prep_sel")
    pw = _prep_call(z, bsz, seq, Z_KV + 1024, 2 * NSA_G, NSA_G, gk, WINDOW, WINDOW, "nsa_prep_win")
    return pq, pkv, pw


def _compress_kernel(x_ref, pos_ref, w1_ref, w2_ref, g_ref, o_ref):
    half = x_ref.shape[-1]
    x = x_ref[0, 0, 0]
    lo = jnp.dot((x + pos_ref[0, :, 0:half]).astype(BF16), w1_ref[0, 0:half, :], preferred_element_type=F32)
    hi = jnp.dot((x + pos_ref[0, :, half:2 * half]).astype(BF16), w1_ref[0, half:2 * half, :],
                 preferred_element_type=F32)
    ncp = x.shape[0]
    pre = lo + pltpu.roll(hi, ncp - 1, 0)
    y = jnp.dot(jax.nn.gelu(pre).astype(BF16), w2_ref[0], preferred_element_type=F32)
    yn = y * lax.rsqrt(jnp.mean(y * y, axis=-1, keepdims=True) + EPS) * g_ref[...]
    o_ref[0, 0, 0] = jnp.where(pl.program_id(0) == 0, yn, y).astype(o_ref.dtype)


def nsa_compress(z, bsz, seq, cmp_pos, w1k, w2k, w1v, w2v, knorm_g):
    ncp = seq // CMP_STRIDE
    kv = z[:, Z_KV:Z_KV + 512].reshape(bsz, seq, 2, NSA_G, NSA_DH)
    kv = kv.transpose(2, 0, 3, 1, 4).reshape(2, bsz, NSA_G, ncp, CMP_STRIDE * NSA_DH)
    pos = cmp_pos.reshape(1, 1, CMP_LEN * NSA_DH)
    w1 = jnp.stack([w1k, w1v]).astype(BF16)
    w2 = jnp.stack([w2k, w2v]).astype(BF16)
    kdim = CMP_STRIDE * NSA_DH
    return pl.pallas_call(
        _compress_kernel,
        grid=(2, bsz, NSA_G),
        in_specs=[pl.BlockSpec((1, 1, 1, ncp, kdim), lambda t, b, g: (t, b, g, 0, 0)),
                  pl.BlockSpec((1, 1, 2 * kdim), lambda t, b, g: (0, 0, 0)),
                  pl.BlockSpec((1, 2 * kdim, NSA_DH), lambda t, b, g: (t, 0, 0)),
                  pl.BlockSpec((1, NSA_DH, NSA_DH), lambda t, b, g: (t, 0, 0)),
                  pl.BlockSpec((1, NSA_DH), lambda t, b, g: (0, 0))],
        out_specs=pl.BlockSpec((1, 1, 1, ncp, NSA_DH), lambda t, b, g: (t, b, g, 0, 0)),
        out_shape=jax.ShapeDtypeStruct((2, bsz, NSA_G, ncp, NSA_DH), BF16),
        compiler_params=_cp("arbitrary", "arbitrary", "arbitrary"),
        name="nsa_compress",
    )(kv, pos, w1, w2, knorm_g.reshape(1, NSA_DH))


def _dist_tile(bt_ref, h, rho):
    return bt_ref[0, h, jnp.clip(rho, -1, N_DIST_TILES - 1) + 1]


def _nsa_kernel(q_ref, kc_ref, vc_ref, rel_ref, ovt_ref, ks_ref, vs_ref, et_ref, kw_ref, vw_ref, bt_ref, gt_ref,
                o_ref, sc_ref, *state):
    m_refs, acc_refs = state[0:NSA_HPG], state[NSA_HPG:2 * NSA_HPG]
    j = pl.program_id(2)
    QB, HPG, DH = Q_BLOCK, NSA_HPG, NSA_DH
    R = HPG * QB
    ncp = kc_ref.shape[3]
    nsel = ovt_ref.shape[0]
    q = q_ref[0].reshape(R, DH)
    nt = (((1,), (1,)), ((), ()))
    qi = lax.broadcasted_iota(jnp.int32, (QB, 1), 0)
    tpos = j * QB + qi

    cidx = lax.broadcasted_iota(jnp.int32, (QB, ncp), 1)
    mask_c = jnp.logical_and(cidx * CMP_STRIDE + (CMP_LEN - 1) <= tpos, cidx < ncp - 1)
    has_c = tpos >= CMP_LEN - 1
    band0 = (j * (QB // CMP_STRIDE) + ncp - (LANES - QB // CMP_STRIDE)) % ncp
    s_c = lax.dot_general(q, kc_ref[0, 0, 0], nt, preferred_element_type=F32)
    p_c = []
    for h in range(HPG):
        sat = bt_ref[0, h, N_DIST_TILES]
        bias = jnp.concatenate([rel_ref[0, h]] + [sat] * (ncp // LANES - 1), axis=1)
        bias = pltpu.roll(bias, band0, 1)
        sh = jnp.where(mask_c, s_c[h * QB:(h + 1) * QB, :] + bias, NEG_INF)
        mx = jnp.max(sh, axis=-1, keepdims=True)
        e = jnp.exp2(sh - mx)
        den = jnp.sum(e, axis=-1, keepdims=True)
        p_c.append(e * jnp.where(has_c, 1.0 / den, 0.0))
    o_c = jnp.dot(jnp.concatenate(p_c, axis=0).astype(BF16), vc_ref[0, 0, 0], preferred_element_type=F32)

    psum = p_c[0] + p_c[1] + p_c[2] + p_c[3]
    p_hi = psum.astype(BF16)
    p_lo = (psum - p_hi.astype(F32)).astype(BF16)
    imp = (lax.dot_general(ovt_ref[...], p_hi, nt, preferred_element_type=F32)
           + lax.dot_general(ovt_ref[...], p_lo, nt, preferred_element_type=F32))
    nidx = lax.broadcasted_iota(jnp.int32, (nsel, QB), 0)
    tpos_l = j * QB + lax.broadcasted_iota(jnp.int32, (1, QB), 1)
    blk_t = tpos_l // SLC_LEN
    valid = nidx * SLC_LEN <= tpos_l
    forced = (nidx == 0) | (nidx == blk_t) | (nidx == blk_t - 1)
    nidx_f = nidx.astype(F32)
    n_forced = 3

    def select_all_valid():
        return jnp.where(valid, 0.0, NEG_INF)

    def select_rounds():
        score = jnp.where(jnp.logical_and(valid, jnp.logical_not(forced)), imp, NEG_INF)
        sel = jnp.where(forced, 0.0, NEG_INF)
        for _ in range(NSA_N_SELECT - n_forced):
            best = jnp.max(score, axis=0, keepdims=True)
            first = jnp.min(jnp.where(score == best, nidx_f, float(nsel)), axis=0, keepdims=True)
            hit = nidx_f == first
            sel = jnp.where(hit, 0.0, sel)
            score = jnp.where(hit, -jnp.inf, score)
        return sel

    sel_t = lax.cond((j + 1) * QB <= NSA_N_SELECT * SLC_LEN, select_all_valid, select_rounds)
    sel_neg = sel_t.T.astype(BF16)

    for h in range(HPG):
        m_refs[h][...] = jnp.full((QB, LANES), -jnp.inf, F32)
        acc_refs[h][...] = jnp.zeros((QB, 2 * DH), F32)
    KT = SEL_TILE
    sub = KT // QB
    q_sel = jnp.concatenate([q, jnp.concatenate([sel_neg] * HPG, axis=0)], axis=1)

    def lane_fold(x, op):
        parts = [x[:, u * LANES:(u + 1) * LANES] for u in range(KT // LANES)]
        while len(parts) > 1:
            parts = [op(parts[a], parts[a + 1]) for a in range(0, len(parts), 2)]
        return parts[0]

    def score_step(t, carry):
        k0 = pl.multiple_of(t * KT, KT)
        k_sel = jnp.concatenate([ks_ref[0, 0, pl.ds(k0, KT), :], et_ref[pl.ds(k0, KT), :]], axis=1)
        s = lax.dot_general(q_sel, k_sel, nt, preferred_element_type=F32)
        for h in range(HPG):
            bias = jnp.concatenate([_dist_tile(bt_ref, h, j - t * sub - u) for u in range(sub)], axis=1)
            sh = s[h * QB:(h + 1) * QB, :] + bias
            sc_ref[t, h * QB:(h + 1) * QB, :] = sh
            m_refs[h][...] = jnp.maximum(m_refs[h][...], lane_fold(sh, jnp.maximum))
        return carry

    ones_col = jnp.ones((KT, DH), BF16)

    def value_step(t, carry):
        k0 = pl.multiple_of(t * KT, KT)
        v_ext = jnp.concatenate([vs_ref[0, 0, pl.ds(k0, KT), :], ones_col], axis=1)
        for h in range(HPG):
            mb = m_refs[h][...]
            x = sc_ref[t, h * QB:(h + 1) * QB, :] - jnp.concatenate([mb] * (KT // LANES), axis=1)
            acc_refs[h][...] += jnp.dot(jnp.exp2(x.astype(BF16)), v_ext, preferred_element_type=F32)
        return carry

    n_pairs = ((j * QB) // KT + 2) // 2

    def pair(step):
        def body(p, carry):
            return step(2 * p + 1, step(2 * p, carry))
        return body

    lax.fori_loop(0, n_pairs, pair(score_step), 0)
    for h in range(HPG):
        m_refs[h][...] = jnp.broadcast_to(jnp.max(m_refs[h][...], axis=-1, keepdims=True), (QB, LANES))
    lax.fori_loop(0, n_pairs, pair(value_step), 0)
    o_s = [acc_refs[h][:, 0:DH] / acc_refs[h][:, DH:2 * DH] for h in range(HPG)]

    WK = WINDOW + QB
    w0 = pl.multiple_of(j * QB, QB)
    k_w = kw_ref[0, 0, pl.ds(w0, WK), :]
    v_w = vw_ref[0, 0, pl.ds(w0, WK), :]
    s_w = lax.dot_general(q, k_w, nt, preferred_element_type=F32)
    nwt = WK // QB
    tab_w = [jnp.where(j + u >= nwt - 1, TAB_WINDOW_FAR if u == 0 else nwt - u, TAB_MASKED) for u in range(nwt)]
    o_w = []
    for h in range(HPG):
        bias = jnp.concatenate([bt_ref[0, h, tab_w[u]] for u in range(nwt)], axis=1)
        sh = s_w[h * QB:(h + 1) * QB, :] + bias
        mxw = jnp.max(sh, axis=-1, keepdims=True)
        pw = jnp.exp2(sh - mxw)
        lw = jnp.sum(pw, axis=-1, keepdims=True)
        o_w.append(jnp.dot(pw.astype(BF16), v_w, preferred_element_type=F32) / lw)

    gate = jax.nn.sigmoid(gt_ref[...])
    for h in range(HPG):
        gc = gate[:, 8 + 3 * h:9 + 3 * h]
        gs = gate[:, 9 + 3 * h:10 + 3 * h]
        gw = gate[:, 10 + 3 * h:11 + 3 * h]
        y = gc * o_c[h * QB:(h + 1) * QB, :] + gs * o_s[h] + gw * o_w[h]
        o_ref[:, h * DH:(h + 1) * DH] = y.astype(o_ref.dtype)


def nsa_tables(rel_bias, seq):
    def bias_of(dist):
        dist = jnp.maximum(dist, 0)
        max_exact = NUM_BUCKETS // 2
        log_ratio = jnp.log(jnp.maximum(dist, 1).astype(F32) / max_exact) / math.log(MAX_DISTANCE / max_exact)
        large = jnp.minimum(max_exact + (log_ratio * (NUM_BUCKETS - max_exact)).astype(jnp.int32), NUM_BUCKETS - 1)
        bucket = jnp.where(dist < max_exact, dist, large)
        onehot = (bucket[..., None] == jnp.arange(NUM_BUCKETS)).astype(F32)
        out = jnp.dot(onehot, rel_bias.astype(F32), precision=lax.Precision.HIGHEST)
        return jnp.moveaxis(out, -1, 0)

    assert seq // CMP_STRIDE >= LANES and Q_BLOCK * (N_DIST_TILES - 1) - (Q_BLOCK - 1) >= MAX_DISTANCE
    i = jnp.arange(Q_BLOCK)
    delta = i[None, :, None] - i[None, None, :]
    dist = jnp.arange(N_DIST_TILES)[:, None, None] * Q_BLOCK + delta
    tiles = bias_of(dist) * LOG2E
    causal = jnp.where(dist >= 0, tiles, NEG_INF)
    far = jnp.where(delta < 0, tiles[:, WINDOW // Q_BLOCK], NEG_INF)[:, None]
    masked = jnp.full_like(far, NEG_INF)
    bt = jnp.concatenate([masked, causal, far], axis=1).reshape(NSA_G, NSA_HPG, N_TAB, Q_BLOCK, Q_BLOCK)
    per_tile = Q_BLOCK // CMP_STRIDE
    assert (LANES - per_tile) * CMP_STRIDE - (CMP_LEN - 1) >= MAX_DISTANCE
    dist_c = i[:, None] - ((jnp.arange(LANES)[None, :] - (LANES - per_tile)) * CMP_STRIDE + CMP_LEN - 1)
    rel = (bias_of(dist_c) * LOG2E).reshape(NSA_G, NSA_HPG, Q_BLOCK, LANES)
    return bt, rel


def nsa_block_indicator(seq):
    nsel = seq // SLC_LEN
    return jnp.asarray(np.arange(seq)[:, None] // SLC_LEN == np.arange(nsel)[None, :], BF16)


def nsa_overlap(seq):
    ncp = seq // CMP_STRIDE
    ns = seq // SLC_LEN
    cstart = np.arange(ncp) * CMP_STRIDE
    sstart = np.arange(ns) * SLC_LEN
    ov = np.clip(np.minimum(cstart[:, None] + CMP_LEN, sstart[None, :] + SLC_LEN)
                 - np.maximum(cstart[:, None], sstart[None, :]), 0, None).astype(np.float32) / CMP_LEN
    ov[ncp - 1] = 0.0
    return jnp.asarray(ov.T, BF16)


def nsa_attend(z, pq, pkv, pw, cmp, bt, rel, ovt, et, bsz, seq):
    nqb = seq // Q_BLOCK
    ncp = seq // CMP_STRIDE
    nsel = seq // SLC_LEN
    R = NSA_HPG * Q_BLOCK
    assert (seq // SEL_TILE) % 2 == 0
    return pl.pallas_call(
        _nsa_kernel,
        grid=(bsz, NSA_G, nqb),
        in_specs=[pl.BlockSpec((1, NSA_HPG, Q_BLOCK, NSA_DH), lambda b, g, j: (b, g, j, 0)),
                  pl.BlockSpec((1, 1, 1, ncp, NSA_DH), lambda b, g, j: (0, b, g, 0, 0)),
                  pl.BlockSpec((1, 1, 1, ncp, NSA_DH), lambda b, g, j: (1, b, g, 0, 0)),
                  pl.BlockSpec((1, NSA_HPG, Q_BLOCK, LANES), lambda b, g, j: (g, 0, 0, 0)),
                  pl.BlockSpec((nsel, ncp), lambda b, g, j: (0, 0)),
                  pl.BlockSpec((1, 1, seq, NSA_DH), lambda b, g, j: (b, g, 0, 0)),
                  pl.BlockSpec((1, 1, seq, NSA_DH), lambda b, g, j: (b, NSA_G + g, 0, 0)),
                  pl.BlockSpec((seq, nsel), lambda b, g, j: (0, 0)),
                  pl.BlockSpec((1, 1, seq + WINDOW, NSA_DH), lambda b, g, j: (b, g, 0, 0)),
                  pl.BlockSpec((1, 1, seq + WINDOW, NSA_DH), lambda b, g, j: (b, NSA_G + g, 0, 0)),
                  pl.BlockSpec((1, NSA_HPG, N_TAB, Q_BLOCK, Q_BLOCK), lambda b, g, j: (g, 0, 0, 0, 0)),
                  pl.BlockSpec((Q_BLOCK, LANES), lambda b, g, j: (b * nqb + j, Z_SMALL // LANES + g))],
        out_specs=pl.BlockSpec((Q_BLOCK, NSA_HPG * NSA_DH), lambda b, g, j: (b * nqb + j, g)),
        out_shape=jax.ShapeDtypeStruct((bsz * seq, BRANCH_WIDTH), BF16),
        scratch_shapes=([pltpu.VMEM((seq // SEL_TILE, R, SEL_TILE), F32)]
                        + [pltpu.VMEM((Q_BLOCK, LANES), F32)] * NSA_HPG
                        + [pltpu.VMEM((Q_BLOCK, 2 * NSA_DH), F32)] * NSA_HPG),
        compiler_params=_cp("arbitrary", "arbitrary", "arbitrary"),
        name="nsa_attend",
    )(pq, cmp, cmp, rel, ovt, pkv, pkv, et, pw, pw, bt, z)


def _merge_kernel(ya_ref, yb_ref, yc_ref, w_ref, ga_ref, gb_ref, gc_ref, o_ref):
    acc = jax.nn.sigmoid(ga_ref[...]) * jnp.dot(ya_ref[...], w_ref[0], preferred_element_type=F32)
    acc = acc + jax.nn.sigmoid(gb_ref[...]) * jnp.dot(yb_ref[...], w_ref[1], preferred_element_type=F32)
    acc = acc + jax.nn.sigmoid(gc_ref[...]) * jnp.dot(yc_ref[...], w_ref[2], preferred_element_type=F32)
    o_ref[...] = acc.astype(o_ref.dtype)


def merge(ya, yb, yc, w_branch, z, tm=1024, tn=512):
    n = ya.shape[0]
    d = w_branch.shape[2]
    bw = ya.shape[1]
    ymap = lambda i, j: (i, 0)
    gmap = lambda k: (lambda i, j: (i, (Z_GATE + k * d) // tn + j))
    return pl.pallas_call(
        _merge_kernel,
        grid=(n // tm, d // tn),
        in_specs=[pl.BlockSpec((tm, bw), ymap), pl.BlockSpec((tm, bw), ymap), pl.BlockSpec((tm, bw), ymap),
                  pl.BlockSpec((3, bw, tn), lambda i, j: (0, 0, j)),
                  pl.BlockSpec((tm, tn), gmap(0)), pl.BlockSpec((tm, tn), gmap(1)), pl.BlockSpec((tm, tn), gmap(2))],
        out_specs=pl.BlockSpec((tm, tn), lambda i, j: (i, j)),
        out_shape=jax.ShapeDtypeStruct((n, d), BF16),
        compiler_params=_cp("arbitrary", "arbitrary"),
        name="merge",
    )(ya, yb, yc, w_branch, z, z, z)


def _outproj_kernel(a_ref, w_ref, x_ref, g_ref, o_ref):
    o_ref[...] = x_ref[...] + g_ref[0] * jnp.dot(a_ref[...], w_ref[...], preferred_element_type=F32)


def out_proj(a, w, x2d, gate, seq, tm=1024, tn=512):
    n, k = a.shape
    d = w.shape[1]
    per_b = seq // tm
    return pl.pallas_call(
        _outproj_kernel,
        grid=(n // tm, d // tn),
        in_specs=[pl.BlockSpec((tm, k), lambda i, j: (i, 0)),
                  pl.BlockSpec((k, tn), lambda i, j: (0, j)),
                  pl.BlockSpec((tm, tn), lambda i, j: (i, j)),
                  pl.BlockSpec((1, 1, tn), lambda i, j: (i // per_b, 0, j))],
        out_specs=pl.BlockSpec((tm, tn), lambda i, j: (i, j)),
        out_shape=jax.ShapeDtypeStruct((n, d), F32),
        compiler_params=_cp("arbitrary", "arbitrary"),
        name="out_proj",
    )(a, w, x2d, gate[:, None, :])


def _pack_bf16_pairs(y):
    half = y.shape[1] // 2
    lo = lax.bitcast_convert_type(y[:, :half].astype(BF16).astype(F32), jnp.uint32)
    hi = lax.bitcast_convert_type(y[:, half:].astype(BF16).astype(F32), jnp.uint32)
    return lax.shift_right_logical(lo, jnp.uint32(16)) | (hi & jnp.uint32(0xFFFF0000))


def _unpack_bf16_pairs(w):
    lo = lax.bitcast_convert_type(lax.shift_left(w, jnp.uint32(16)), F32)
    hi = lax.bitcast_convert_type(w & jnp.uint32(0xFFFF0000), F32)
    return jnp.concatenate([lo, hi], axis=1)


def _router_kernel(x_ref, g_ref, sc_ref, sh_ref, rw_ref, rwl_ref, rb_ref, h_ref, idx_ref, p_ref):
    x = x_ref[...]
    y = x * lax.rsqrt(jnp.mean(x * x, axis=-1, keepdims=True) + EPS)
    hf = (y * g_ref[...]) * (1.0 + sc_ref[0]) + sh_ref[0]
    h = hf.astype(BF16)
    h_ref[...] = _pack_bf16_pairs(hf)
    h_lo = (hf - h.astype(F32)).astype(BF16)
    logits = (jnp.dot(h, rw_ref[...], preferred_element_type=F32)
              + jnp.dot(h_lo, rw_ref[...], preferred_element_type=F32)
              + jnp.dot(h, rwl_ref[...], preferred_element_type=F32)) + rb_ref[...]
    lane = lax.broadcasted_iota(jnp.int32, logits.shape, 1)
    lane_f = lane.astype(F32)
    idx_out = jnp.zeros(logits.shape, F32)
    val_out = jnp.full(logits.shape, -jnp.inf, F32)
    for k in range(TOP_K):
        best = jnp.max(logits, axis=-1, keepdims=True)
        first = jnp.min(jnp.where(logits == best, lane_f, float(LANES)), axis=-1, keepdims=True)
        idx_out = jnp.where(lane == k, first, idx_out)
        val_out = jnp.where(lane == k, best, val_out)
        logits = jnp.where(lane_f == first, -jnp.inf, logits)
    e = jnp.exp(val_out - jnp.max(val_out, axis=-1, keepdims=True))
    idx_ref[...] = idx_out.astype(jnp.int32)
    p_ref[...] = e / jnp.sum(e, axis=-1, keepdims=True)


def router(x2d, g, sc, sh, rw, rb, seq, tm=1024):
    n, d = x2d.shape
    per_b = seq // tm
    rw_hi = rw.astype(BF16)
    rw_lo = (rw - rw_hi.astype(F32)).astype(BF16)
    rwp = jnp.zeros((d, LANES), BF16).at[:, :N_EXPERTS].set(rw_hi)
    rwl = jnp.zeros((d, LANES), BF16).at[:, :N_EXPERTS].set(rw_lo)
    rbp = jnp.full((1, LANES), -jnp.inf, F32).at[0, :N_EXPERTS].set(rb)
    return pl.pallas_call(
        _router_kernel,
        grid=(n // tm,),
        in_specs=[pl.BlockSpec((tm, d), lambda i: (i, 0)),
                  pl.BlockSpec((1, d), lambda i: (0, 0)),
                  pl.BlockSpec((1, 1, d), lambda i: (i // per_b, 0, 0)),
                  pl.BlockSpec((1, 1, d), lambda i: (i // per_b, 0, 0)),
                  pl.BlockSpec((d, LANES), lambda i: (0, 0)),
                  pl.BlockSpec((d, LANES), lambda i: (0, 0)),
                  pl.BlockSpec((1, LANES), lambda i: (0, 0))],
        out_specs=[pl.BlockSpec((tm, d // 2), lambda i: (i, 0)),
                   pl.BlockSpec((tm, LANES), lambda i: (i, 0)),
                   pl.BlockSpec((tm, LANES), lambda i: (i, 0))],
        out_shape=[jax.ShapeDtypeStruct((n, d // 2), jnp.uint32),
                   jax.ShapeDtypeStruct((n, LANES), jnp.int32),
                   jax.ShapeDtypeStruct((n, LANES), F32)],
        compiler_params=_cp("arbitrary"),
        name="router",
    )(x2d, g.reshape(1, d), sc[:, None, :], sh[:, None, :], rwp, rwl, rbp)


def _dispatch_kernel(dest_ref, h_ref, init_ref, xb_ref, sem):
    del init_ref
    tm = h_ref.shape[0]

    def row_copy(k, r):
        dst = dest_ref[0, 0, k * tm + r]
        return pltpu.make_async_copy(h_ref.at[pl.ds(r, 1), :], xb_ref.at[pl.ds(dst, 1), :], sem)

    def start(r, carry):
        for k in range(TOP_K):
            row_copy(k, r).start(priority=k % 2)
        return carry

    def wait(r, carry):
        for k in range(TOP_K):
            row_copy(k, r).wait()
        return carry

    lax.fori_loop(0, tm, start, 0, unroll=8)
    lax.fori_loop(0, tm, wait, 0, unroll=8)


def dispatch(hp, dest_tiles, init):
    n, w = hp.shape
    rows = init.shape[0]
    nb, _, per_tile = dest_tiles.shape
    tm = per_tile // TOP_K
    return pl.pallas_call(
        _dispatch_kernel,
        grid=(nb,),
        in_specs=[pl.BlockSpec((1, 1, per_tile), lambda i: (i, 0, 0), memory_space=pltpu.SMEM),
                  pl.BlockSpec((tm, w), lambda i: (i, 0)),
                  pl.BlockSpec(memory_space=pl.ANY)],
        out_specs=pl.BlockSpec(memory_space=pl.ANY),
        out_shape=jax.ShapeDtypeStruct((rows, w), jnp.uint32),
        scratch_shapes=[pltpu.SemaphoreType.DMA(())],
        input_output_aliases={2: 0},
        compiler_params=_cp("arbitrary"),
        name="dispatch",
    )(dest_tiles, hp, init)


def _expert_kernel(be_ref, nv_ref, x_ref, wg_ref, wu_ref, bg_ref, bu_ref, w2_ref, b2_ref, o_ref, acc_ref, xs_ref):
    i = pl.program_id(0)
    f = pl.program_id(1)

    @pl.when(i < nv_ref[0])
    def _():
        @pl.when(f == 0)
        def _():
            acc_ref[...] = jnp.zeros(acc_ref.shape, F32)
            xs_ref[...] = _unpack_bf16_pairs(x_ref[...]).astype(BF16)

        x = xs_ref[...]
        gate = jnp.dot(x, wg_ref[0, 0].astype(BF16), preferred_element_type=F32) + bg_ref[0, 0]
        up = jnp.dot(x, wu_ref[0, 0].astype(BF16), preferred_element_type=F32) + bu_ref[0, 0]
        gate = jnp.minimum(gate, SWIGLU_LIMIT)
        up = jnp.clip(up, -SWIGLU_LIMIT, SWIGLU_LIMIT)
        act = (up + 1.0) * (gate * jax.nn.sigmoid(SWIGLU_ALPHA * gate))
        acc_ref[...] += jnp.dot(act.astype(BF16), w2_ref[0, 0].astype(BF16), preferred_element_type=F32)

        @pl.when(f == pl.num_programs(1) - 1)
        def _():
            o_ref[...] = _pack_bf16_pairs(acc_ref[...] + b2_ref[0, 0])

    @pl.when(i >= nv_ref[0])
    def _():
        o_ref[...] = jnp.zeros(o_ref.shape, o_ref.dtype)


def experts(xb, block_exp, n_valid, w1, b1, w2, b2, layer, tm, tf=256):
    rows = xb.shape[0]
    d = 2 * xb.shape[1]
    nf = D_FF // tf
    nblk = rows // tm
    n_exp = w1.shape[1]
    fe = lambda i, f, nv: jnp.where(i < nv[0], f, nf - 1)
    grid_spec = pltpu.PrefetchScalarGridSpec(
        num_scalar_prefetch=2,
        grid=(nblk, nf),
        in_specs=[pl.BlockSpec((tm, d // 2), lambda i, f, be, nv: (jnp.minimum(i, nv[0] - 1), 0)),
                  pl.BlockSpec((1, 1, d, tf), lambda i, f, be, nv: (layer, be[i], 0, fe(i, f, nv))),
                  pl.BlockSpec((1, 1, d, tf), lambda i, f, be, nv: (layer, be[i], 0, nf + fe(i, f, nv))),
                  pl.BlockSpec((1, 1, 1, tf), lambda i, f, be, nv: (layer * n_exp + be[i], 0, 0, fe(i, f, nv))),
                  pl.BlockSpec((1, 1, 1, tf), lambda i, f, be, nv: (layer * n_exp + be[i], 0, 0, nf + fe(i, f, nv))),
                  pl.BlockSpec((1, 1, tf, d), lambda i, f, be, nv: (layer, be[i], fe(i, f, nv), 0)),
                  pl.BlockSpec((1, 1, 1, d), lambda i, f, be, nv: (layer * n_exp + be[i], 0, 0, 0))],
        out_specs=pl.BlockSpec((tm, d // 2), lambda i, f, be, nv: (i, 0)),
        scratch_shapes=[pltpu.VMEM((tm, d), F32), pltpu.VMEM((tm, d), BF16)],
    )
    b1r = b1.reshape(-1, 1, 1, b1.shape[-1])
    b2r = b2.reshape(-1, 1, 1, d)
    return pl.pallas_call(
        _expert_kernel,
        grid_spec=grid_spec,
        out_shape=jax.ShapeDtypeStruct((rows, d // 2), jnp.uint32),
        compiler_params=_cp("arbitrary", "arbitrary"),
        name="experts",
    )(block_exp, n_valid, xb, w1, w1, b1r, b1r, w2, b2r)


def _combine_kernel(dest_ref, p_ref, x_ref, g_ref, yb_ref, o_ref, ybuf_ref, sem):
    tm = x_ref.shape[0]

    def row_copy(k, r):
        src = dest_ref[0, 0, k * tm + r]
        return pltpu.make_async_copy(yb_ref.at[pl.ds(src, 1), :], ybuf_ref.at[k, pl.ds(r, 1), :], sem)

    def start(r, carry):
        for k in range(TOP_K):
            row_copy(k, r).start(priority=k % 2)
        return carry

    def wait(r, carry):
        for k in range(TOP_K):
            row_copy(k, r).wait()
        return carry

    lax.fori_loop(0, tm, start, 0, unroll=8)
    lax.fori_loop(0, tm, wait, 0, unroll=8)
    p = p_ref[...]
    y = p[:, 0:1] * _unpack_bf16_pairs(ybuf_ref[0])
    for k in range(1, TOP_K):
        y = y + p[:, k:k + 1] * _unpack_bf16_pairs(ybuf_ref[k])
    o_ref[...] = x_ref[...] + g_ref[0] * y


def combine(yb, dest_tiles, prob, x2d, gate, seq):
    n, d = x2d.shape
    nb = dest_tiles.shape[0]
    tm = n // nb
    per_b = seq // tm
    return pl.pallas_call(
        _combine_kernel,
        grid=(nb,),
        in_specs=[pl.BlockSpec((1, 1, TOP_K * tm), lambda i: (i, 0, 0), memory_space=pltpu.SMEM),
                  pl.BlockSpec((tm, LANES), lambda i: (i, 0)),
                  pl.BlockSpec((tm, d), lambda i: (i, 0)),
                  pl.BlockSpec((1, 1, d), lambda i: (i // per_b, 0, 0)),
                  pl.BlockSpec(memory_space=pl.ANY)],
        out_specs=pl.BlockSpec((tm, d), lambda i: (i, 0)),
        out_shape=jax.ShapeDtypeStruct((n, d), F32),
        scratch_shapes=[pltpu.VMEM((TOP_K, tm, d // 2), jnp.uint32), pltpu.SemaphoreType.DMA(())],
        compiler_params=_cp("arbitrary"),
        name="combine",
    )(dest_tiles, prob, x2d, gate[:, None, :], yb)


def moe(x2d, g, sc, sh, gate, rw, rb, w1, b1, w2, b2, layer, seq, xb_init=None, tm_e=1024, tm_t=512):
    n, d = x2d.shape
    h, idx, prob = router(x2d, g, sc, sh, rw, rb, seq)
    e_flat = idx[:, :TOP_K].T.reshape(-1)
    na = n * TOP_K
    onehot = (e_flat[:, None] == jnp.arange(N_EXPERTS)[None, :]).astype(jnp.int32)
    csum = jnp.cumsum(onehot, axis=0)
    counts = csum[-1]
    rank = jnp.take_along_axis(csum, e_flat[:, None], axis=1)[:, 0] - 1
    padded = (counts + tm_e - 1) // tm_e * tm_e
    pend = jnp.cumsum(padded)
    pstart = pend - padded
    dest = pstart[e_flat] + rank
    nblk = na // tm_e + N_EXPERTS
    rows = nblk * tm_e
    starts = jnp.arange(nblk, dtype=jnp.int32) * tm_e
    block_exp = jnp.minimum(jnp.sum(pend[None, :] <= starts[:, None], axis=1), N_EXPERTS - 1).astype(jnp.int32)
    n_valid = (pend[-1:] // tm_e).astype(jnp.int32)
    block_exp = jnp.where(starts < pend[-1], block_exp, block_exp[jnp.maximum(n_valid[0] - 1, 0)])
    nb = n // tm_t
    dest_tiles = dest.reshape(TOP_K, nb, tm_t).transpose(1, 0, 2).reshape(nb, 1, TOP_K * tm_t)
    if xb_init is None:
        xb_init = jnp.zeros((rows, h.shape[1]), jnp.uint32)
    xb = dispatch(h, dest_tiles, xb_init)
    yb = experts(xb, block_exp, n_valid, w1, b1, w2, b2, layer, tm_e)
    return combine(yb, dest_tiles, prob, x2d, gate, seq), xb


def fused_in_weight(w_in, w_gate):
    sizes = (512, 512, 1024, 4, 4, 1024, 1024, 256, 256, 256, 256, 256, 256, 24, 1024, 1024)
    pts = np.cumsum(sizes)[:-1].tolist()
    (qm, km, vm, ig, fg, og, qn, kc, vc, ks, vs, kw, vw, gn, gu, gv) = jnp.split(w_in, pts, axis=1)
    d = w_in.shape[0]
    pad = lambda k: jnp.zeros((d, k), w_in.dtype)
    per_group = NSA_HPG * 3
    small0 = jnp.concatenate([ig, fg, gn[:, :per_group], pad(LANES - 8 - per_group)], axis=1)
    small1 = jnp.concatenate([pad(8), gn[:, per_group:], pad(LANES - 8 - per_group)], axis=1)
    cols = [qm, km, vm, og, qn, gu, gv, kc, vc, ks, vs, kw, vw, w_gate[0], w_gate[1], w_gate[2], small0, small1]
    w = jnp.concatenate(cols, axis=1)
    w = jnp.concatenate([w, pad(Z_WIDTH - w.shape[1])], axis=1)
    return w.astype(BF16)


def kernel(x, c, ada_w, ada_b, norm1_g, norm2_g, w_in, conv_w, conv_b, mlstm_gate_b, mlstm_norm_g, cmp_pos,
           cmp_k_w1, cmp_k_w2, cmp_v_w1, cmp_v_w2, qnorm_g, knorm_g, rel_bias, gmlp_norm_g, gmlp_ws, gmlp_b,
           w_branch, w_gate, w_out, router_w, router_b, exp_w1, exp_b1, exp_w2, exp_b2):
    bsz, seq, d = x.shape
    depth = ada_w.shape[0]
    x2d = x.reshape(bsz * seq, d)
    bt, rel = nsa_tables(rel_bias, seq)
    ovt = nsa_overlap(seq)
    et = nsa_block_indicator(seq)
    xb = None
    for l in range(depth):
        mod = ada_mod(c, ada_w, ada_b, l)
        sh1, sc1, g1, sh2, sc2, g2 = jnp.split(mod, 6, axis=-1)
        z = in_proj(x2d, norm1_g[l], sc1, sh1, fused_in_weight(w_in[l], w_gate[l]), seq)
        ya = mlstm(z, bsz, seq, conv_w[l], conv_b[l], mlstm_gate_b[l], mlstm_norm_g[l])
        pq, pkv, pw = nsa_prep(z, bsz, seq, qnorm_g[l], knorm_g[l])
        cmp = nsa_compress(z, bsz, seq, cmp_pos[l], cmp_k_w1[l], cmp_k_w2[l], cmp_v_w1[l], cmp_v_w2[l], knorm_g[l])
        yb = nsa_attend(z, pq, pkv, pw, cmp, bt, rel, ovt, et, bsz, seq)
        yc = gmlp(z, gmlp_norm_g[l], gmlp_ws[l], gmlp_b[l])
        merged = merge(ya, yb, yc, w_branch[l].astype(BF16), z)
        x2d = out_proj(merged, w_out[l].astype(BF16), x2d, g1, seq)
        x2d, xb = moe(x2d, norm2_g[l], sc2, sh2, g2, router_w[l], router_b[l], exp_w1, exp_b1, exp_w2, exp_b2, l, seq,
                      xb_init=xb)
    return x2d.reshape(bsz, seq, d)
```
